```python
import math
import jax, jax.numpy as jnp
from jax import lax
import numpy as np

D_MODEL = 1024
BATCH = 2
SEQ = 8192
DEPTH = 2

RET_HEADS = 4
RET_HEAD_DIM = 64
RET_WIDTH = RET_HEADS * RET_HEAD_DIM
RET_CHUNK = 128
SSM_GROUP_CH = 16
SSM_GROUPS = 16
SSM_WIDTH = SSM_GROUPS * SSM_GROUP_CH
SSM_STATE = 64
MLA_HEADS = 4
MLA_NOPE = 128
MLA_ROPE = 64
MLA_V = 128
MLA_QK = MLA_NOPE + MLA_ROPE
MLA_WIDTH = MLA_HEADS * MLA_V
MLA_Q_RANK = 256
MLA_KV_RANK = 128
ATTN_BLOCK = 128

D_MIX = RET_WIDTH + SSM_WIDTH + MLA_WIDTH
ROPE_DIM = 64
ROPE_BASE = 10000.0

D_FF = 3584
N_EXPERTS = 8
TOP_K = 2
MOE_BLOCK = 128
N_DENSE = (DEPTH + 1) // 2
N_MOE = DEPTH // 2
EPS = 1e-6
NEG_INF = -1e30

IN_SPLITS = (RET_WIDTH, RET_WIDTH, RET_WIDTH, RET_WIDTH, SSM_WIDTH, MLA_Q_RANK, MLA_KV_RANK, MLA_ROPE)
IN_COLS = sum(IN_SPLITS)

kernel_name = 'hybrid_retnet_s5_mla_moe_block'


def rms_norm(x, g):
    xf = x.astype(jnp.float32)
    y = xf * lax.rsqrt(jnp.mean(xf * xf, axis=-1, keepdims=True) + EPS)
    return (y * g.astype(jnp.float32)).astype(x.dtype)


def rope_tables(positions):
    inv = ROPE_BASE ** (-jnp.arange(0, ROPE_DIM, 2, dtype=jnp.float32) / ROPE_DIM)
    ang = positions.astype(jnp.float32)[..., None] * inv
    return jnp.cos(ang), jnp.sin(ang)


def apply_rope(x, cos, sin):
    shp = cos.shape[:2] + (1,) * (x.ndim - 3) + cos.shape[2:]
    cos = cos.reshape(shp)
    sin = sin.reshape(shp)
    x1, x2 = jnp.split(x.astype(jnp.float32), 2, axis=-1)
    return jnp.concatenate([x1 * cos - x2 * sin, x2 * cos + x1 * sin], axis=-1).astype(x.dtype)


def retention(q, k, v, cos, sin):
    bsz, L, H, dk = q.shape
    dv = v.shape[-1]
    C = RET_CHUNK
    N = L // C
    q = apply_rope(q, cos, sin)
    k = apply_rope(k, cos, sin) * (dk ** -0.5)
    log_g = jnp.log1p(-(2.0 ** (-5.0 - jnp.arange(H, dtype=jnp.float32))))
    i = jnp.arange(C, dtype=jnp.float32)
    diff = i[:, None] - i[None, :]
    dmask = jnp.where(diff >= 0, jnp.exp(log_g[:, None, None] * jnp.maximum(diff, 0.0)), 0.0)
    qc = q.reshape(bsz, N, C, H, dk)
    kc = k.reshape(bsz, N, C, H, dk)
    vc = v.reshape(bsz, N, C, H, dv)
    s = jnp.einsum('bnihd,bnjhd->bnhij', qc, kc) * dmask
    inner = jnp.einsum('bnhij,bnjhe->bnihe', s, vc)
    k_dec = jnp.exp(log_g[None, :] * (C - 1 - i)[:, None])
    kv = jnp.einsum('bnjhd,bnjhe->nbhde', kc * k_dec[:, :, None], vc)
    chunk_decay = jnp.exp(log_g * C)[:, None, None]

    def step(state, kv_n):
        return chunk_decay * state + kv_n, state

    _, states = lax.scan(step, jnp.zeros((bsz, H, dk, dv), kv.dtype), kv)
    q_dec = jnp.exp(log_g[None, :] * (i + 1.0)[:, None])
    cross = jnp.einsum('bnihd,nbhde->bnihe', qc * q_dec[:, :, None], states)
    return (inner + cross).reshape(bsz, L, H, dv).astype(v.dtype)


def s5_ssm(u, a_re, a_im, b_re, b_im, c_re, c_im, d, log_dt, glu_w, glu_b):
    bsz, L, _ = u.shape
    uf = u.astype(jnp.float32).reshape(bsz, L, SSM_GROUPS, SSM_GROUP_CH)
    lam = lax.complex(a_re.astype(jnp.float32), a_im.astype(jnp.float32))
    dt = jnp.exp(log_dt.astype(jnp.float32))[:, None]
    lam_bar = jnp.exp(lam * dt)
    b = lax.complex(b_re.astype(jnp.float32), b_im.astype(jnp.float32))
    b_bar = ((lam_bar - 1.0) / lam)[..., None] * b
    bu = jnp.einsum('blgh,gph->blgp', uf.astype(jnp.complex64), b_bar)
    a = jnp.broadcast_to(lam_bar, bu.shape)

    def combine(e1, e2):
        a1, b1 = e1
        a2, b2 = e2
        return a1 * a2, a2 * b1 + b2

    _, states = lax.associative_scan(combine, (a, bu), axis=1)
    cm = lax.complex(c_re.astype(jnp.float32), c_im.astype(jnp.float32))
    y = jnp.real(jnp.einsum('blgp,ghp->blgh', states, cm)) + d.astype(jnp.float32) * uf
    y = jax.nn.gelu(y.reshape(bsz, L, SSM_WIDTH))
    y = y * jax.nn.sigmoid(y @ glu_w.astype(jnp.float32) + glu_b.astype(jnp.float32))
    return y.astype(u.dtype)


def mla(c_q, c_kv, k_rope, q_norm, w_uq, kv_norm, w_ukv, cos, sin):
    bsz, L, _ = c_q.shape
    q = (rms_norm(c_q, q_norm) @ w_uq).reshape(bsz, L, MLA_HEADS, MLA_QK)
    q = jnp.concatenate([q[..., :MLA_NOPE], apply_rope(q[..., MLA_NOPE:], cos, sin)], axis=-1)
    kv = (rms_norm(c_kv, kv_norm) @ w_ukv).reshape(bsz, L, MLA_HEADS, MLA_NOPE + MLA_V)
    k_pe = apply_rope(k_rope, cos, sin)
    k = jnp.concatenate([kv[..., :MLA_NOPE], jnp.broadcast_to(k_pe[:, :, None, :], (bsz, L, MLA_HEADS, MLA_ROPE))], axis=-1)
    v = kv[..., MLA_NOPE:]
    scale = MLA_QK ** -0.5
    nb = L // ATTN_BLOCK
    q_blocks = q.reshape(bsz, nb, ATTN_BLOCK, MLA_HEADS, MLA_QK).transpose(1, 0, 2, 3, 4)
    k_pos = jnp.arange(L)

    def attend(args):
        qb, n = args
        s = jnp.einsum('bqhd,bkhd->bhqk', qb, k, preferred_element_type=jnp.float32) * scale
        q_pos = n * ATTN_BLOCK + jnp.arange(ATTN_BLOCK)
        s = jnp.where(k_pos[None, :] <= q_pos[:, None], s, NEG_INF)
        p = jax.nn.softmax(s, axis=-1).astype(v.dtype)
        return jnp.einsum('bhqk,bkhd->bqhd', p, v)

    o = lax.map(attend, (q_blocks, jnp.arange(nb)))
    return o.transpose(1, 0, 2, 3, 4).reshape(bsz, L, MLA_WIDTH)


def hybrid_mixer(h, cos, sin, w_in, ret_norm, a_re, a_im, b_re, b_im, c_re, c_im, d, log_dt,
                 glu_w, glu_b, ssm_norm, q_norm, w_uq, kv_norm, w_ukv, mla_norm, w_out):
    bsz, L, _ = h.shape
    offsets = np.cumsum(IN_SPLITS)[:-1].tolist()
    rq, rk, rv, rg, u, cq, ckv, kr = jnp.split(h @ w_in, offsets, axis=-1)
    ret = retention(rq.reshape(bsz, L, RET_HEADS, RET_HEAD_DIM), rk.reshape(bsz, L, RET_HEADS, RET_HEAD_DIM),
                    rv.reshape(bsz, L, RET_HEADS, RET_HEAD_DIM), cos, sin)
    ret = rms_norm(ret, ret_norm.reshape(RET_HEADS, RET_HEAD_DIM)).reshape(bsz, L, RET_WIDTH)
    ret = jax.nn.silu(rg) * ret
    ssm = rms_norm(s5_ssm(u, a_re, a_im, b_re, b_im, c_re, c_im, d, log_dt, glu_w, glu_b), ssm_norm)
    att = rms_norm(mla(cq, ckv, kr, q_norm, w_uq, kv_norm, w_ukv, cos, sin), mla_norm)
    return jnp.concatenate([ret, ssm, att], axis=-1) @ w_out


def swiglu(h, wg, wu, wd):
    return (jax.nn.silu(h @ wg) * (h @ wu)) @ wd


def routed_ffn(h, router, wg, wu, wd):
    bsz, L, D = h.shape
    t = h.reshape(-1, D)
    T = t.shape[0]
    A = T * TOP_K
    logits = (t @ router).astype(jnp.float32)
    top_v, top_i = lax.top_k(logits, TOP_K)
    top_w = jax.nn.softmax(top_v, axis=-1)
    expert = top_i.reshape(-1)
    tok = jnp.repeat(jnp.arange(T), TOP_K)
    order = jnp.argsort(expert)
    e_sorted = expert[order]
    tok_sorted = tok[order]
    w_sorted = top_w.reshape(-1)[order]
    counts = jnp.bincount(expert, length=N_EXPERTS)
    padded = ((counts + MOE_BLOCK - 1) // MOE_BLOCK) * MOE_BLOCK
    pad_end = jnp.cumsum(padded)
    pad_start = pad_end - padded
    raw_start = jnp.cumsum(counts) - counts
    dest = pad_start[e_sorted] + (jnp.arange(A) - raw_start[e_sorted])
    n_rows = A + N_EXPERTS * MOE_BLOCK
    n_blocks = n_rows // MOE_BLOCK
    x_buf = jnp.zeros((n_rows, D), t.dtype).at[dest].set(t[tok_sorted])
    block_start = jnp.arange(n_blocks) * MOE_BLOCK
    block_expert = jnp.clip(jnp.searchsorted(pad_end, block_start, side='right'), 0, N_EXPERTS - 1)

    def expert_block(args):
        xb, e = args
        return swiglu(xb, wg[e], wu[e], wd[e])

    y_buf = lax.map(expert_block, (x_buf.reshape(n_blocks, MOE_BLOCK, D), block_expert)).reshape(n_rows, D)
    y = jnp.zeros_like(t).at[tok_sorted].add(y_buf[dest] * w_sorted[:, None].astype(t.dtype))
    return y.reshape(bsz, L, D)


def setup_inputs(seed: int = 0) -> dict:
    key = jax.random.key(seed)
    ks = iter(jax.random.split(key, 64))

    def nrm(shape, scale):
        return jax.random.normal(next(ks), shape, jnp.float32) * scale

    def gain(shape):
        return 1.0 + 0.02 * jax.random.normal(next(ks), shape, jnp.float32)

    G, P, Hg = SSM_GROUPS, SSM_STATE, SSM_GROUP_CH
    x = nrm((BATCH, SEQ, D_MODEL), 1.0)
    c = nrm((BATCH, D_MODEL), 1.0)
    positions = jnp.broadcast_to(jnp.arange(SEQ, dtype=jnp.int32), (BATCH, SEQ))
    ada_w = nrm((DEPTH, D_MODEL, 6 * D_MODEL), 0.5 * D_MODEL ** -0.5)
    ada_b = nrm((DEPTH, 6 * D_MODEL), 0.01)
    norm_pre_mix = gain((DEPTH, D_MODEL))
    norm_post_mix = gain((DEPTH, D_MODEL))
    norm_pre_ffn = gain((DEPTH, D_MODEL))
    norm_post_ffn = gain((DEPTH, D_MODEL))
    w_in = nrm((DEPTH, D_MODEL, IN_COLS), D_MODEL ** -0.5)
    ret_norm = gain((DEPTH, RET_WIDTH))
    ssm_a_re = -0.5 + 0.01 * jax.random.normal(next(ks), (DEPTH, G, P), jnp.float32)
    ssm_a_im = math.pi * jnp.arange(P, dtype=jnp.float32) + 0.01 * jax.random.normal(next(ks), (DEPTH, G, P), jnp.float32)
    ssm_b_re = nrm((DEPTH, G, P, Hg), (2.0 * Hg) ** -0.5)
    ssm_b_im = nrm((DEPTH, G, P, Hg), (2.0 * Hg) ** -0.5)
    ssm_c_re = nrm((DEPTH, G, Hg, P), (2.0 * P) ** -0.5)
    ssm_c_im = nrm((DEPTH, G, Hg, P), (2.0 * P) ** -0.5)
    ssm_d = nrm((DEPTH, G, Hg), 1.0)
    ssm_log_dt = jax.random.uniform(next(ks), (DEPTH, G), jnp.float32, minval=math.log(1e-3), maxval=math.log(1e-1))
    ssm_glu_w = nrm((DEPTH, SSM_WIDTH, SSM_WIDTH), SSM_WIDTH ** -0.5)
    ssm_glu_b = nrm((DEPTH, SSM_WIDTH), 0.01)
    ssm_norm = gain((DEPTH, SSM_WIDTH))
    mla_q_norm = gain((DEPTH, MLA_Q_RANK))
    mla_w_uq = nrm((DEPTH, MLA_Q_RANK, MLA_HEADS * MLA_QK), MLA_Q_RANK ** -0.5)
    mla_kv_norm = gain((DEPTH, MLA_KV_RANK))
    mla_w_ukv = nrm((DEPTH, MLA_KV_RANK, MLA_HEADS * (MLA_NOPE + MLA_V)), MLA_KV_RANK ** -0.5)
    mla_norm = gain((DEPTH, MLA_WIDTH))
    w_out = nrm((DEPTH, D_MIX, D_MODEL), D_MIX ** -0.5)
    ffn_w_gate = nrm((N_DENSE, D_MODEL, D_FF), D_MODEL ** -0.5)
    ffn_w_up = nrm((N_DENSE, D_MODEL, D_FF), D_MODEL ** -0.5)
    ffn_w_down = nrm((N_DENSE, D_FF, D_MODEL), D_FF ** -0.5)
    moe_router = nrm((N_MOE, D_MODEL, N_EXPERTS), D_MODEL ** -0.5)
    moe_w_gate = nrm((N_MOE, N_EXPERTS, D_MODEL, D_FF), D_MODEL ** -0.5)
    moe_w_up = nrm((N_MOE, N_EXPERTS, D_MODEL, D_FF), D_MODEL ** -0.5)
    moe_w_down = nrm((N_MOE, N_EXPERTS, D_FF, D_MODEL), D_FF ** -0.5)
    return {'x': x, 'c': c, 'positions': positions, 'ada_w': ada_w, 'ada_b': ada_b,
            'norm_pre_mix': norm_pre_mix, 'norm_post_mix': norm_post_mix,
            'norm_pre_ffn': norm_pre_ffn, 'norm_post_ffn': norm_post_ffn,
            'w_in': w_in, 'ret_norm': ret_norm,
            'ssm_a_re': ssm_a_re, 'ssm_a_im': ssm_a_im, 'ssm_b_re': ssm_b_re, 'ssm_b_im': ssm_b_im,
            'ssm_c_re': ssm_c_re, 'ssm_c_im': ssm_c_im, 'ssm_d': ssm_d, 'ssm_log_dt': ssm_log_dt,
            'ssm_glu_w': ssm_glu_w, 'ssm_glu_b': ssm_glu_b, 'ssm_norm': ssm_norm,
            'mla_q_norm': mla_q_norm, 'mla_w_uq': mla_w_uq, 'mla_kv_norm': mla_kv_norm,
            'mla_w_ukv': mla_w_ukv, 'mla_norm': mla_norm, 'w_out': w_out,
            'ffn_w_gate': ffn_w_gate, 'ffn_w_up': ffn_w_up, 'ffn_w_down': ffn_w_down,
            'moe_router': moe_router, 'moe_w_gate': moe_w_gate, 'moe_w_up': moe_w_up,
            'moe_w_down': moe_w_down}


def reference(x, c, positions, ada_w, ada_b, norm_pre_mix, norm_post_mix, norm_pre_ffn, norm_post_ffn,
              w_in, ret_norm, ssm_a_re, ssm_a_im, ssm_b_re, ssm_b_im, ssm_c_re, ssm_c_im, ssm_d,
              ssm_log_dt, ssm_glu_w, ssm_glu_b, ssm_norm, mla_q_norm, mla_w_uq, mla_kv_norm,
              mla_w_ukv, mla_norm, w_out, ffn_w_gate, ffn_w_up, ffn_w_down, moe_router,
              moe_w_gate, moe_w_up, moe_w_down):
    cos, sin = rope_tables(positions)
    cond = jax.nn.silu(c)
    for layer in range(DEPTH):
        mod = (cond @ ada_w[layer] + ada_b[layer])[:, None, :]
        sh_m, sc_m, gt_m, sh_f, sc_f, gt_f = jnp.split(mod, 6, axis=-1)
        h = rms_norm(x, norm_pre_mix[layer]) * (1.0 + sc_m) + sh_m
        y = hybrid_mixer(h, cos, sin, w_in[layer], ret_norm[layer], ssm_a_re[layer], ssm_a_im[layer],
                         ssm_b_re[layer], ssm_b_im[layer], ssm_c_re[layer], ssm_c_im[layer], ssm_d[layer],
                         ssm_log_dt[layer], ssm_glu_w[layer], ssm_glu_b[layer], ssm_norm[layer],
                         mla_q_norm[layer], mla_w_uq[layer], mla_kv_norm[layer], mla_w_ukv[layer],
                         mla_norm[layer], w_out[layer])
        x = x + gt_m * rms_norm(y, norm_post_mix[layer])
        h = rms_norm(x, norm_pre_ffn[layer]) * (1.0 + sc_f) + sh_f
        j = layer // 2
        if layer % 2 == 0:
            y = swiglu(h, ffn_w_gate[j], ffn_w_up[j], ffn_w_down[j])
        else:
            y = routed_ffn(h, moe_router[j], moe_w_gate[j], moe_w_up[j], moe_w_down[j])
        x = x + gt_f * rms_norm(y, norm_post_ffn[layer])
    return x
```

```python
import functools
import math

import numpy as np
import jax
import jax.numpy as jnp
from jax import lax
from jax.experimental import pallas as pl
from jax.experimental.pallas import tpu as pltpu

D_MODEL = 1024
RET_HEADS = 4
RET_HEAD_DIM = 64
RET_WIDTH = RET_HEADS * RET_HEAD_DIM
RET_CHUNK = 128
SSM_GROUP_CH = 16
SSM_GROUPS = 16
SSM_WIDTH = SSM_GROUPS * SSM_GROUP_CH
SSM_STATE = 64
MLA_HEADS = 4
MLA_NOPE = 128
MLA_ROPE = 64
MLA_V = 128
MLA_QK = MLA_NOPE + MLA_ROPE
MLA_WIDTH = MLA_HEADS * MLA_V
MLA_Q_RANK = 256
MLA_KV_RANK = 128
ROPE_DIM = 64
ROPE_BASE = 10000.0
D_FF = 3584
N_EXPERTS = 8
TOP_K = 2
EPS = 1e-6
NEG_INF = -1e30
IN_COLS = 4 * RET_WIDTH + SSM_WIDTH + MLA_Q_RANK + MLA_KV_RANK + MLA_ROPE

LANES = 128
VMEM_LIMIT_BYTES = 56 * 1024 * 1024

SSM_CHUNK = 64
MOE_TILE = 512
FF_TILE = 512

BF16 = jnp.bfloat16
F32 = jnp.float32


def _cparams(sem):
    return pltpu.CompilerParams(dimension_semantics=sem, vmem_limit_bytes=VMEM_LIMIT_BYTES)


def _tile(n, pref):
    t = min(n, pref)
    while n % t:
        t //= 2
    return t


def _dot(a, b):
    return jnp.dot(a, b, preferred_element_type=F32)


def _dot3(a, b):
    a_hi = a.astype(BF16)
    a_lo = (a - a_hi.astype(F32)).astype(BF16)
    b_hi = b.astype(BF16)
    b_lo = (b - b_hi.astype(F32)).astype(BF16)
    return _dot(a_hi, b_hi) + (_dot(a_hi, b_lo) + _dot(a_lo, b_hi))


def _rms(x, g):
    return x * lax.rsqrt(jnp.mean(x * x, axis=-1, keepdims=True) + EPS) * g


def _silu(x):
    return x * (1.0 / (1.0 + jnp.exp(-x)))


def _rope128(x, cos, sin_signed):
    lane = lax.broadcasted_iota(jnp.int32, x.shape, 1)
    first_half = (lane % ROPE_DIM) < (ROPE_DIM // 2)
    partner = jnp.where(first_half, pltpu.roll(x, LANES - ROPE_DIM // 2, 1), pltpu.roll(x, ROPE_DIM // 2, 1))
    return x * cos + partner * sin_signed


def _mod_kernel(c_ref, w_ref, b_ref, o_ref):
    cond = _silu(c_ref[...])
    o_ref[...] = _dot3(cond, w_ref[...]) + b_ref[...]


def _modulation(c, ada_w, ada_b):
    depth, d, n = ada_w.shape
    bsz = c.shape[0]
    rows = 8
    c_pad = jnp.zeros((rows, d), F32).at[:bsz].set(c)
    tn = _tile(n, 1536)
    out = pl.pallas_call(
        _mod_kernel,
        grid=(depth, n // tn),
        in_specs=[pl.BlockSpec((rows, d), lambda l, j: (0, 0)),
                  pl.BlockSpec((None, d, tn), lambda l, j: (l, 0, j)),
                  pl.BlockSpec((None, 1, tn), lambda l, j: (l, 0, j))],
        out_specs=pl.BlockSpec((None, rows, tn), lambda l, j: (l, 0, j)),
        out_shape=jax.ShapeDtypeStruct((depth, rows, n), F32),
        compiler_params=_cparams(("arbitrary", "arbitrary")),
        name="adaln_mod",
    )(c_pad, ada_w, ada_b.reshape(depth, 1, n))
    return out[:, :bsz].reshape(depth, bsz, 1, n)


def _rope_kernel(pos_ref, inv_ref, cos_ref, sin_ref):
    pos = pos_ref[...].astype(F32)
    lane = lax.broadcasted_iota(jnp.int32, cos_ref.shape, 1)
    half = ROPE_DIM // 2
    p = jnp.where(lane < half, pos[:, 0:1],
                  jnp.where(lane < 2 * half, pos[:, 1:2],
                            jnp.where(lane < 3 * half, pos[:, 2:3], pos[:, 3:4])))
    ang = p * inv_ref[...]
    cos_ref[...] = jnp.cos(ang)
    sin_ref[...] = jnp.sin(ang)


def _rope_tables(positions):
    t = positions.size
    half = ROPE_DIM // 2
    per_row = LANES // half
    inv = ROPE_BASE ** (-jnp.arange(0, ROPE_DIM, 2, dtype=F32) / ROPE_DIM)
    inv_row = jnp.tile(inv, per_row).reshape(1, LANES)
    rows = t // per_row
    tr = _tile(rows, 1024)
    cos, sin = pl.pallas_call(
        _rope_kernel,
        grid=(rows // tr,),
        in_specs=[pl.BlockSpec((tr, per_row), lambda i: (i, 0)),
                  pl.BlockSpec((1, LANES), lambda i: (0, 0))],
        out_specs=[pl.BlockSpec((tr, LANES), lambda i: (i, 0))] * 2,
        out_shape=[jax.ShapeDtypeStruct((rows, LANES), F32)] * 2,
        compiler_params=_cparams(("arbitrary",)),
        name="rope_tables",
    )(positions.reshape(rows, per_row), inv_row)
    cos = cos.reshape(t, half)
    sin = sin.reshape(t, half)
    return jnp.tile(cos, (1, per_row)), jnp.concatenate([-sin, sin, -sin, sin], axis=1)


def _inproj_kernel(x_ref, g_ref, mod_ref, w_ref, ret_ref, u_ref, cq_ref, ckv_ref, kr_ref):
    d = D_MODEL
    x = x_ref[...]
    h = _rms(x, g_ref[...]) * (1.0 + mod_ref[:, d:2 * d]) + mod_ref[:, 0:d]
    p = _dot(h.astype(BF16), w_ref[...])
    o = 4 * RET_WIDTH
    ret_ref[...] = p[:, :o].astype(BF16)
    u_ref[...] = p[:, o:o + SSM_WIDTH].astype(BF16)
    o += SSM_WIDTH
    cq_ref[...] = p[:, o:o + MLA_Q_RANK].astype(BF16)
    o += MLA_Q_RANK
    ckv_ref[...] = p[:, o:o + MLA_KV_RANK].astype(BF16)
    o += MLA_KV_RANK
    kr_ref[...] = p[:, o:o + MLA_ROPE].astype(BF16)


def _inproj(x2, g, mod_l, w_in, seq):
    t, d = x2.shape
    tm = _tile(seq, 512)
    tpb = seq // tm
    widths = (4 * RET_WIDTH, SSM_WIDTH, MLA_Q_RANK, MLA_KV_RANK, MLA_ROPE)
    return pl.pallas_call(
        _inproj_kernel,
        grid=(t // tm,),
        in_specs=[pl.BlockSpec((tm, d), lambda i: (i, 0)),
                  pl.BlockSpec((1, d), lambda i: (0, 0)),
                  pl.BlockSpec((None, 1, 6 * d), lambda i: (i // tpb, 0, 0)),
                  pl.BlockSpec((d, IN_COLS), lambda i: (0, 0))],
        out_specs=[pl.BlockSpec((tm, w), lambda i: (i, 0)) for w in widths],
        out_shape=[jax.ShapeDtypeStruct((t, w), BF16) for w in widths],
        compiler_params=_cparams(("arbitrary",)),
        name="mixer_inproj",
    )(x2, g.reshape(1, d), mod_l, w_in.astype(BF16))


def _retention_kernel(r_ref, cos_ref, sin_ref, dmask_ref, qdec_ref, kdec_ref, cdec_ref, bd_ref, gn_ref,
                      o_ref, state_ref, *, chunks):
    w = RET_WIDTH

    @pl.when(pl.program_id(1) == 0)
    def _():
        state_ref[...] = jnp.zeros_like(state_ref)

    lane = lax.broadcasted_iota(jnp.int32, (RET_CHUNK, w), 1)
    bd = bd_ref[...]
    for c in range(chunks):
        rows = pl.ds(c * RET_CHUNK, RET_CHUNK)
        cos = cos_ref[rows, :]
        sin = sin_ref[rows, :]

        def rope(v):
            return jnp.concatenate([_rope128(v[:, :LANES], cos, sin), _rope128(v[:, LANES:], cos, sin)], axis=1)

        q = rope(r_ref[rows, 0:w].astype(F32))
        k = rope(r_ref[rows, w:2 * w].astype(F32)) * (RET_HEAD_DIM ** -0.5)
        v = r_ref[rows, 2 * w:3 * w]
        gate = r_ref[rows, 3 * w:4 * w].astype(F32)
        kb = k.astype(BF16)
        state = state_ref[...]
        y = _dot((q * qdec_ref[...]).astype(BF16), state.astype(BF16))
        for h in range(RET_HEADS):
            in_head = (lane // RET_HEAD_DIM) == h
            qh = jnp.where(in_head, q, 0.0).astype(BF16)
            s = lax.dot_general(qh, kb, (((1,), (1,)), ((), ())), preferred_element_type=F32)
            s = s * dmask_ref[h]
            y = y + jnp.where(in_head, _dot(s.astype(BF16), v), 0.0)
        kv = lax.dot_general((k * kdec_ref[...]).astype(BF16), v, (((0,), (0,)), ((), ())),
                             preferred_element_type=F32)
        state_ref[...] = cdec_ref[...] * state + bd * kv
        y2 = y * y
        y2_hi = y2.astype(BF16)
        y2_lo = (y2 - y2_hi.astype(F32)).astype(BF16)
        bdb = bd.astype(BF16)
        ms = (_dot(y2_hi, bdb) + _dot(y2_lo, bdb)) * (1.0 / RET_HEAD_DIM)
        yn = y * lax.rsqrt(ms + EPS) * gn_ref[...]
        o_ref[rows, :] = (_silu(gate) * yn).astype(BF16)


def _retention_consts():
    c = RET_CHUNK
    log_g = np.log1p(-(2.0 ** (-5.0 - np.arange(RET_HEADS, dtype=np.float32)))).astype(np.float32)
    i = np.arange(c, dtype=np.float32)
    diff = i[:, None] - i[None, :]
    dmask = np.where(diff >= 0, np.exp(log_g[:, None, None] * np.maximum(diff, 0.0)), 0.0).astype(np.float32)
    k_dec = np.exp(log_g[None, :] * (c - 1 - i)[:, None]).astype(np.float32)
    q_dec = np.exp(log_g[None, :] * (i + 1.0)[:, None]).astype(np.float32)
    c_dec = np.exp(log_g * c).astype(np.float32)
    rep = lambda a: np.repeat(a, RET_HEAD_DIM, axis=-1)
    head = np.arange(RET_WIDTH) // RET_HEAD_DIM
    bd = (head[:, None] == head[None, :]).astype(np.float32)
    return dmask, rep(q_dec), rep(k_dec), rep(c_dec[None, :]), bd


def _retention(r, cos128, sin128, ret_norm, bsz, seq):
    t = r.shape[0]
    w = RET_WIDTH
    tr = _tile(seq, 512)
    chunks = tr // RET_CHUNK
    nt = seq // tr
    dmask, qdec, kdec, cdec, bd = (jnp.asarray(a) for a in _retention_consts())
    const = lambda shape: pl.BlockSpec(shape, lambda b, i: (0,) * len(shape))
    return pl.pallas_call(
        functools.partial(_retention_kernel, chunks=chunks),
        grid=(bsz, nt),
        in_specs=[pl.BlockSpec((tr, 4 * w), lambda b, i: (b * nt + i, 0)),
                  pl.BlockSpec((tr, LANES), lambda b, i: (b * nt + i, 0)),
                  pl.BlockSpec((tr, LANES), lambda b, i: (b * nt + i, 0)),
                  const((RET_HEADS, RET_CHUNK, RET_CHUNK)),
                  const((RET_CHUNK, w)), const((RET_CHUNK, w)), const((1, w)), const((w, w)), const((1, w))],
        out_specs=pl.BlockSpec((tr, w), lambda b, i: (b * nt + i, 0)),
        out_shape=jax.ShapeDtypeStruct((t, w), BF16),
        scratch_shapes=[pltpu.VMEM((w, w), F32)],
        compiler_params=_cparams(("arbitrary", "arbitrary")),
        name="retention",
    )(r, cos128, sin128, dmask, qdec, kdec, cdec, bd, ret_norm.reshape(1, w))


def _ssm_kernel(u_ref, toep_ref, win_ref, vout_ref, apow_ref, y_ref, *, chunks_per_seq, steps):
    u = u_ref[...]
    x = _dot(u, win_ref[...])
    row = lax.broadcasted_iota(jnp.int32, x.shape, 0) % chunks_per_seq
    half = SSM_STATE

    def shift_rows(a, dist):
        return jnp.where(row >= dist, pltpu.roll(a, dist, 0), 0.0)

    for k in range(steps):
        dist = 1 << k
        prev = shift_rows(x, dist)
        x = x + prev * apow_ref[k, 0:1, :] + pltpu.roll(prev, half, 1) * apow_ref[k, 1:2, :]
    x_in = shift_rows(x, 1)
    y_ref[...] = _dot(u, toep_ref[...]) + _dot(x_in.astype(BF16), vout_ref[...])


def _ssm_operators(a_re, a_im, b_re, b_im, c_re, c_im, log_dt, steps):
    g, p, hg = b_re.shape
    tc = SSM_CHUNK
    a_re, a_im = a_re.astype(F32), a_im.astype(F32)
    dt = jnp.exp(log_dt.astype(F32))[:, None]

    def lam_bar_pow(n):
        n = n[:, None, None]
        mag = jnp.exp(a_re * dt * n)
        return mag * jnp.cos(a_im * dt * n), mag * jnp.sin(a_im * dt * n)

    one = jnp.ones((1,), F32)
    lb_re, lb_im = (a[0] for a in lam_bar_pow(one))
    den = a_re * a_re + a_im * a_im
    f_re = ((lb_re - 1.0) * a_re + lb_im * a_im) / den
    f_im = (lb_im * a_re - (lb_re - 1.0) * a_im) / den
    bb_re = f_re[..., None] * b_re - f_im[..., None] * b_im
    bb_im = f_re[..., None] * b_im + f_im[..., None] * b_re
    pw_re, pw_im = lam_bar_pow(jnp.arange(tc + 1, dtype=F32))
    m_re = pw_re[..., None] * bb_re - pw_im[..., None] * bb_im
    m_im = pw_re[..., None] * bb_im + pw_im[..., None] * bb_re
    hp = lax.Precision.HIGHEST
    taps = (jnp.einsum('gop,tgpi->tgoi', c_re.astype(F32), m_re[:tc], precision=hp)
            - jnp.einsum('gop,tgpi->tgoi', c_im.astype(F32), m_im[:tc], precision=hp))
    s_idx = jnp.arange(tc)
    lag = s_idx[None, :] - s_idx[:, None]
    toep = jnp.where((lag >= 0)[:, :, None, None, None], taps[jnp.clip(lag, 0, tc - 1)], 0.0)
    toep = toep.transpose(2, 0, 4, 1, 3).reshape(g, tc * hg, tc * hg)
    rev = tc - 1 - s_idx
    w_in = jnp.concatenate([m_re[rev].transpose(1, 0, 3, 2).reshape(g, tc * hg, p),
                            m_im[rev].transpose(1, 0, 3, 2).reshape(g, tc * hg, p)], axis=-1)
    v_re = c_re[None] * pw_re[1 + s_idx][:, :, None, :] - c_im[None] * pw_im[1 + s_idx][:, :, None, :]
    v_im = c_re[None] * pw_im[1 + s_idx][:, :, None, :] + c_im[None] * pw_re[1 + s_idx][:, :, None, :]
    v_out = jnp.concatenate([v_re.transpose(1, 3, 0, 2).reshape(g, p, tc * hg),
                             -v_im.transpose(1, 3, 0, 2).reshape(g, p, tc * hg)], axis=1)
    ap_re, ap_im = lam_bar_pow((2.0 ** jnp.arange(steps, dtype=F32)) * tc)
    a_rows = jnp.stack([jnp.concatenate([ap_re, ap_re], -1), jnp.concatenate([-ap_im, ap_im], -1)], axis=2)
    return toep.astype(BF16), w_in.astype(BF16), v_out.astype(BF16), a_rows.transpose(1, 0, 2, 3)


def _ssm_scan(u, ops, seq):
    t = u.shape[0]
    g, hg, tc = SSM_GROUPS, SSM_GROUP_CH, SSM_CHUNK
    nc = t // tc
    cps = seq // tc
    steps = max(1, int(math.ceil(math.log2(cps))))
    toep, w_in, v_out, a_rows = ops
    k = tc * hg
    ug = u.reshape(nc, tc, g, hg).transpose(2, 0, 1, 3).reshape(g, nc, k)
    y = pl.pallas_call(
        functools.partial(_ssm_kernel, chunks_per_seq=cps, steps=steps),
        grid=(g,),
        in_specs=[pl.BlockSpec((None, nc, k), lambda i: (i, 0, 0)),
                  pl.BlockSpec((None, k, k), lambda i: (i, 0, 0)),
                  pl.BlockSpec((None, k, 2 * SSM_STATE), lambda i: (i, 0, 0)),
                  pl.BlockSpec((None, 2 * SSM_STATE, k), lambda i: (i, 0, 0)),
                  pl.BlockSpec((None, steps, 2, 2 * SSM_STATE), lambda i: (i, 0, 0, 0))],
        out_specs=pl.BlockSpec((None, nc, k), lambda i: (i, 0, 0)),
        out_shape=jax.ShapeDtypeStruct((g, nc, k), F32),
        compiler_params=_cparams(("arbitrary",)),
        name="s5_chunk_scan",
    )(ug, toep, w_in, v_out, a_rows)
    return y.reshape(g, nc, tc, hg).transpose(1, 2, 0, 3).reshape(t, g * hg)


def _ssm_post_kernel(y_ref, u_ref, d_ref, gw_ref, gb_ref, gn_ref, o_ref):
    y = y_ref[...] + d_ref[...] * u_ref[...].astype(F32)
    y = 0.5 * y * (1.0 + jnp.tanh(math.sqrt(2.0 / math.pi) * (y + 0.044715 * (y * y * y))))
    z = _dot(y.astype(BF16), gw_ref[...]) + gb_ref[...]
    y = y * (1.0 / (1.0 + jnp.exp(-z)))
    o_ref[...] = _rms(y, gn_ref[...]).astype(BF16)


def _ssm_post(y, u, d, glu_w, glu_b, ssm_norm):
    t, w = y.shape
    tm = _tile(t, 1024)
    row = lambda: pl.BlockSpec((1, w), lambda i: (0, 0))
    return pl.pallas_call(
        _ssm_post_kernel,
        grid=(t // tm,),
        in_specs=[pl.BlockSpec((tm, w), lambda i: (i, 0)), pl.BlockSpec((tm, w), lambda i: (i, 0)),
                  row(), pl.BlockSpec((w, w), lambda i: (0, 0)), row(), row()],
        out_specs=pl.BlockSpec((tm, w), lambda i: (i, 0)),
        out_shape=jax.ShapeDtypeStruct((t, w), BF16),
        compiler_params=_cparams(("arbitrary",)),
        name="s5_gelu_glu_norm",
    )(y, u, d.reshape(1, w), glu_w.astype(BF16), glu_b.reshape(1, w), ssm_norm.reshape(1, w))


def _mla_prep_kernel(cq_ref, ckv_ref, kr_ref, cos_ref, sin_ref, qn_ref, kvn_ref, wq_ref, wkv_ref,
                     q_ref, k_ref, v_ref):
    cos = cos_ref[...]
    sin = sin_ref[...]
    nope_w = MLA_HEADS * MLA_NOPE
    q = _dot(_rms(cq_ref[...].astype(F32), qn_ref[...]).astype(BF16), wq_ref[...])
    q = q * (MLA_QK ** -0.5)
    q_rope = jnp.concatenate([_rope128(q[:, nope_w:nope_w + LANES], cos, sin),
                              _rope128(q[:, nope_w + LANES:], cos, sin)], axis=1)
    kv = _dot(_rms(ckv_ref[...].astype(F32), kvn_ref[...]).astype(BF16), wkv_ref[...])
    kr = kr_ref[...].astype(F32)
    k_pe = _rope128(jnp.concatenate([kr, kr], axis=1), cos, sin)[:, :MLA_ROPE]
    for h in range(MLA_HEADS):
        q_ref[h] = jnp.concatenate([q[:, h * MLA_NOPE:(h + 1) * MLA_NOPE],
                                    q_rope[:, h * MLA_ROPE:(h + 1) * MLA_ROPE]], axis=1).astype(BF16)
        k_ref[h] = jnp.concatenate([kv[:, h * MLA_NOPE:(h + 1) * MLA_NOPE], k_pe], axis=1).astype(BF16)
        v_ref[h] = kv[:, nope_w + h * MLA_V:nope_w + (h + 1) * MLA_V].astype(BF16)


def _mla_prep(cq, ckv, kr, cos128, sin128, q_norm, w_uq, kv_norm, w_ukv):
    t = cq.shape[0]
    tm = _tile(t, 512)
    hh = MLA_HEADS
    wq = w_uq.reshape(MLA_Q_RANK, hh, MLA_QK)
    wq = jnp.concatenate([wq[:, :, :MLA_NOPE].reshape(MLA_Q_RANK, -1), wq[:, :, MLA_NOPE:].reshape(MLA_Q_RANK, -1)], 1)
    wkv = w_ukv.reshape(MLA_KV_RANK, hh, MLA_NOPE + MLA_V)
    wkv = jnp.concatenate([wkv[:, :, :MLA_NOPE].reshape(MLA_KV_RANK, -1), wkv[:, :, MLA_NOPE:].reshape(MLA_KV_RANK, -1)], 1)
    tok = lambda w: pl.BlockSpec((tm, w), lambda i: (i, 0))
    full = lambda a, b: pl.BlockSpec((a, b), lambda i: (0, 0))
    return pl.pallas_call(
        _mla_prep_kernel,
        grid=(t // tm,),
        in_specs=[tok(MLA_Q_RANK), tok(MLA_KV_RANK), tok(MLA_ROPE), tok(LANES), tok(LANES),
                  full(1, MLA_Q_RANK), full(1, MLA_KV_RANK),
                  full(MLA_Q_RANK, hh * MLA_QK), full(MLA_KV_RANK, hh * (MLA_NOPE + MLA_V))],
        out_specs=[pl.BlockSpec((hh, tm, MLA_QK), lambda i: (0, i, 0)),
                   pl.BlockSpec((hh, tm, MLA_QK), lambda i: (0, i, 0)),
                   pl.BlockSpec((hh, tm, MLA_V), lambda i: (0, i, 0))],
        out_shape=[jax.ShapeDtypeStruct((hh, t, MLA_QK), BF16), jax.ShapeDtypeStruct((hh, t, MLA_QK), BF16),
                   jax.ShapeDtypeStruct((hh, t, MLA_V), BF16)],
        compiler_params=_cparams(("arbitrary",)),
        name="mla_prep",
    )(cq, ckv, kr, cos128, sin128, q_norm.reshape(1, -1), kv_norm.reshape(1, -1), wq.astype(BF16), wkv.astype(BF16))


def _attn_kernel(qi_ref, kj_ref, q_ref, k_ref, v_ref, o_ref, m_ref, l_ref, acc_ref):
    p_id = pl.program_id(2)
    qi = qi_ref[p_id]
    kj = kj_ref[p_id]

    @pl.when(kj == 0)
    def _():
        m_ref[...] = jnp.full_like(m_ref, NEG_INF)
        l_ref[...] = jnp.zeros_like(l_ref)
        acc_ref[...] = jnp.zeros_like(acc_ref)

    def update(masked):
        s = lax.dot_general(q_ref[...], k_ref[...], (((1,), (1,)), ((), ())), preferred_element_type=F32)
        if masked:
            r = lax.broadcasted_iota(jnp.int32, s.shape, 0)
            c = lax.broadcasted_iota(jnp.int32, s.shape, 1)
            s = jnp.where(c <= r, s, NEG_INF)
        m_prev = m_ref[...]
        m_new = jnp.maximum(m_prev, jnp.max(s, axis=-1, keepdims=True))
        alpha = jnp.exp(m_prev - m_new)
        p = jnp.exp(s - m_new)
        l_ref[...] = alpha * l_ref[...] + jnp.sum(p, axis=-1, keepdims=True)
        acc_ref[...] = alpha * acc_ref[...] + _dot(p.astype(BF16), v_ref[...])
        m_ref[...] = m_new

    @pl.when(kj < qi)
    def _():
        update(False)

    @pl.when(kj == qi)
    def _():
        update(True)
        o_ref[...] = (acc_ref[...] / l_ref[...]).astype(o_ref.dtype)


def _attention(q3, k3, v3, bsz, seq):
    hh, t, _ = q3.shape
    tq = _tile(seq, 512)
    nq = seq // tq
    pairs = [(i, j) for i in range(nq) for j in range(i + 1)]
    qi = jnp.asarray([p[0] for p in pairs], jnp.int32)
    kj = jnp.asarray([p[1] for p in pairs], jnp.int32)
    grid_spec = pltpu.PrefetchScalarGridSpec(
        num_scalar_prefetch=2,
        grid=(bsz, hh, len(pairs)),
        in_specs=[pl.BlockSpec((None, tq, MLA_QK), lambda b, h, p, qi, kj: (h, b * nq + qi[p], 0)),
                  pl.BlockSpec((None, tq, MLA_QK), lambda b, h, p, qi, kj: (h, b * nq + kj[p], 0)),
                  pl.BlockSpec((None, tq, MLA_V), lambda b, h, p, qi, kj: (h, b * nq + kj[p], 0))],
        out_specs=pl.BlockSpec((tq, MLA_V), lambda b, h, p, qi, kj: (b * nq + qi[p], h)),
        scratch_shapes=[pltpu.VMEM((tq, 1), F32), pltpu.VMEM((tq, 1), F32), pltpu.VMEM((tq, MLA_V), F32)],
    )
    return pl.pallas_call(
        _attn_kernel,
        grid_spec=grid_spec,
        out_shape=jax.ShapeDtypeStruct((t, hh * MLA_V), BF16),
        compiler_params=_cparams(("arbitrary", "arbitrary", "arbitrary")),
        name="mla_flash_attention",
    )(qi, kj, q3, k3, v3)


def _outproj_kernel(*refs, routed):
    if routed:
        (x_ref, ret_ref, ssm_ref, att_ref, an_ref, w_ref, gpost_ref, mod_ref, gpre_ref, router_ref,
         xo_ref, h_ref, logit_ref) = refs
    else:
        x_ref, ret_ref, ssm_ref, att_ref, an_ref, w_ref, gpost_ref, mod_ref, gpre_ref, xo_ref, h_ref = refs
    d = D_MODEL
    att = _rms(att_ref[...].astype(F32), an_ref[...]).astype(BF16)
    o1 = RET_WIDTH
    o2 = RET_WIDTH + SSM_WIDTH
    y = _dot(ret_ref[...], w_ref[0:o1, :]) + _dot(ssm_ref[...], w_ref[o1:o2, :]) + _dot(att, w_ref[o2:, :])
    x = x_ref[...] + mod_ref[:, 2 * d:3 * d] * _rms(y, gpost_ref[...])
    xo_ref[...] = x
    h = _rms(x, gpre_ref[...]) * (1.0 + mod_ref[:, 4 * d:5 * d]) + mod_ref[:, 3 * d:4 * d]
    h_ref[...] = h.astype(h_ref.dtype)
    if routed:
        logit_ref[...] = _dot3(h, router_ref[...])


def _outproj(x2, ret, ssm, att, mla_norm, w_out, g_post, mod_l, g_pre, seq, router=None):
    t, d = x2.shape
    tm = _tile(seq, 512)
    tpb = seq // tm
    routed = router is not None
    tok = lambda w: pl.BlockSpec((tm, w), lambda i: (i, 0))
    full = lambda a, b: pl.BlockSpec((a, b), lambda i: (0, 0))
    in_specs = [tok(d), tok(RET_WIDTH), tok(SSM_WIDTH), tok(MLA_WIDTH), full(1, MLA_WIDTH), full(d, d), full(1, d),
                pl.BlockSpec((None, 1, 6 * d), lambda i: (i // tpb, 0, 0)), full(1, d)]
    args = [x2, ret, ssm, att, mla_norm.reshape(1, -1), w_out.astype(BF16), g_post.reshape(1, d), mod_l,
            g_pre.reshape(1, d)]
    out_specs = [tok(d), tok(d)]
    out_shape = [jax.ShapeDtypeStruct((t, d), F32), jax.ShapeDtypeStruct((t, d), F32 if routed else BF16)]
    if routed:
        in_specs.append(full(d, LANES))
        args.append(jnp.zeros((d, LANES), F32).at[:, :N_EXPERTS].set(router))
        out_specs.append(tok(LANES))
        out_shape.append(jax.ShapeDtypeStruct((t, LANES), F32))
    return pl.pallas_call(
        functools.partial(_outproj_kernel, routed=routed),
        grid=(t // tm,),
        in_specs=in_specs, out_specs=out_specs, out_shape=out_shape,
        compiler_params=_cparams(("arbitrary",)),
        name="mixer_outproj_routed" if routed else "mixer_outproj",
    )(*args)


def _ffn_kernel(h_ref, wg_ref, wu_ref, wd_ref, x_ref, g_ref, mod_ref, o_ref, acc_ref):
    f = pl.program_id(1)

    @pl.when(f == 0)
    def _():
        acc_ref[...] = jnp.zeros_like(acc_ref)

    h = h_ref[...]
    a = _silu(_dot(h, wg_ref[...])) * _dot(h, wu_ref[...])
    acc_ref[...] += _dot(a.astype(BF16), wd_ref[...])

    @pl.when(f == pl.num_programs(1) - 1)
    def _():
        d = D_MODEL
        o_ref[...] = x_ref[...] + mod_ref[:, 5 * d:6 * d] * _rms(acc_ref[...], g_ref[...])


def _dense_ffn(h, wg, wu, wd, x2, g_post, mod_l, seq):
    t, d = h.shape
    ff = wg.shape[1]
    tm = _tile(seq, 1024)
    tf = _tile(ff, FF_TILE)
    tpb = seq // tm
    return pl.pallas_call(
        _ffn_kernel,
        grid=(t // tm, ff // tf),
        in_specs=[pl.BlockSpec((tm, d), lambda i, f: (i, 0)),
                  pl.BlockSpec((d, tf), lambda i, f: (0, f)),
                  pl.BlockSpec((d, tf), lambda i, f: (0, f)),
                  pl.BlockSpec((tf, d), lambda i, f: (f, 0)),
                  pl.BlockSpec((tm, d), lambda i, f: (i, 0)),
                  pl.BlockSpec((1, d), lambda i, f: (0, 0)),
                  pl.BlockSpec((None, 1, 6 * d), lambda i, f: (i // tpb, 0, 0))],
        out_specs=pl.BlockSpec((tm, d), lambda i, f: (i, 0)),
        out_shape=jax.ShapeDtypeStruct((t, d), F32),
        scratch_shapes=[pltpu.VMEM((tm, d), F32)],
        compiler_params=_cparams(("arbitrary", "arbitrary")),
        name="dense_swiglu",
    )(h, wg.astype(BF16), wu.astype(BF16), wd.astype(BF16), x2, g_post.reshape(1, d), mod_l)


def _route_kernel(logit_ref, tri_ref, info_ref, count_ref, carry_ref):
    @pl.when(pl.program_id(0) == 0)
    def _():
        carry_ref[...] = jnp.zeros_like(carry_ref)

    lg = logit_ref[...]
    lane = lax.broadcasted_iota(jnp.int32, lg.shape, 1)
    lanef = lane.astype(F32)
    valid = lane < N_EXPERTS
    big = float(LANES)
    lg = jnp.where(valid, lg, -jnp.inf)
    m1 = jnp.max(lg, axis=-1, keepdims=True)
    e1 = jnp.min(jnp.where(lg == m1, lanef, big), axis=-1, keepdims=True)
    lg2 = jnp.where(lanef == e1, -jnp.inf, lg)
    m2 = jnp.max(lg2, axis=-1, keepdims=True)
    e2 = jnp.min(jnp.where(lg2 == m2, lanef, big), axis=-1, keepdims=True)
    z = jnp.exp(m2 - m1)
    w1 = 1.0 / (1.0 + z)
    w2 = z / (1.0 + z)
    oh1 = (lanef == e1).astype(F32)
    oh2 = (lanef == e2).astype(F32)
    both = oh1 + oh2
    before = _dot(tri_ref[...], both.astype(BF16)) + carry_ref[0:1, :]
    r1 = jnp.sum(before * oh1, axis=-1, keepdims=True)
    r2 = jnp.sum(before * oh2, axis=-1, keepdims=True)
    carry_ref[0:1, :] = carry_ref[0:1, :] + jnp.sum(both, axis=0, keepdims=True)
    count_ref[...] = carry_ref[...]
    cols = (e1, e2, r1, r2, w1, w2)
    info = jnp.zeros(lg.shape, F32)
    for idx, col in enumerate(cols):
        info = jnp.where(lane == idx, col, info)
    info_ref[...] = info


def _route(logits):
    t = logits.shape[0]
    tm = _tile(t, 512)
    tri = jnp.asarray(np.tril(np.ones((tm, tm), np.float32), -1), BF16)
    info, counts = pl.pallas_call(
        _route_kernel,
        grid=(t // tm,),
        in_specs=[pl.BlockSpec((tm, LANES), lambda i: (i, 0)), pl.BlockSpec((tm, tm), lambda i: (0, 0))],
        out_specs=[pl.BlockSpec((tm, LANES), lambda i: (i, 0)), pl.BlockSpec((8, LANES), lambda i: (0, 0))],
        out_shape=[jax.ShapeDtypeStruct((t, LANES), F32), jax.ShapeDtypeStruct((8, LANES), F32)],
        scratch_shapes=[pltpu.VMEM((8, LANES), F32)],
        compiler_params=_cparams(("arbitrary",)),
        name="moe_route",
    )(logits, tri)
    return info, counts[0, :N_EXPERTS]


def _dispatch_kernel(d1_ref, d2_ref, h_ref, xin_ref, xbuf_ref, sem, *, tm):
    del xin_ref

    def row_copy(r, dref):
        return pltpu.make_async_copy(h_ref.at[pl.ds(r, 1), :], xbuf_ref.at[pl.ds(dref[0, 0, r], 1), :], sem)

    def start(r, carry):
        row_copy(r, d1_ref).start()
        row_copy(r, d2_ref).start()
        return carry

    def wait(r, carry):
        row_copy(r, d1_ref).wait()
        row_copy(r, d2_ref).wait()
        return carry

    lax.fori_loop(0, tm, start, 0)
    lax.fori_loop(0, tm, wait, 0)


def _dispatch(h, dest1, dest2, n_rows):
    t, d = h.shape
    tm = _tile(t, 512)
    nt = t // tm
    smem = lambda: pl.BlockSpec((1, 1, tm), lambda i: (i, 0, 0), memory_space=pltpu.SMEM)
    return pl.pallas_call(
        functools.partial(_dispatch_kernel, tm=tm),
        grid=(nt,),
        in_specs=[smem(), smem(), pl.BlockSpec((tm, d), lambda i: (i, 0)), pl.BlockSpec(memory_space=pl.ANY)],
        out_specs=pl.BlockSpec(memory_space=pl.ANY),
        out_shape=jax.ShapeDtypeStruct((n_rows, d), h.dtype),
        scratch_shapes=[pltpu.SemaphoreType.DMA(())],
        input_output_aliases={3: 0},
        compiler_params=_cparams(("arbitrary",)),
        name="moe_dispatch",
    )(dest1.reshape(nt, 1, tm), dest2.reshape(nt, 1, tm), h, jnp.zeros((n_rows, d), h.dtype))


def _moe_kernel(be_ref, nb_ref, x_ref, wg_ref, wu_ref, wd_ref, o_ref, xb_ref, acc_ref):
    i = pl.program_id(0)
    f = pl.program_id(1)
    active = i < nb_ref[0]

    @pl.when(jnp.logical_and(active, f == 0))
    def _():
        xb_ref[...] = x_ref[...].astype(BF16)
        acc_ref[...] = jnp.zeros_like(acc_ref)

    @pl.when(active)
    def _():
        xb = xb_ref[...]
        a = _silu(_dot(xb, wg_ref[...])) * _dot(xb, wu_ref[...])
        acc_ref[...] += _dot(a.astype(BF16), wd_ref[...])

    @pl.when(f == pl.num_programs(1) - 1)
    def _():
        o_ref[...] = jnp.where(active, acc_ref[...], 0.0)


def _moe_ffn(x_buf, block_expert, n_active, wg, wu, wd):
    n_rows, d = x_buf.shape
    ff = wg.shape[2]
    tm = MOE_TILE
    tf = _tile(ff, FF_TILE)
    nf = ff // tf

    def wmap(i, f, be, nb):
        return be[i], jnp.where(i < nb[0], f, nf - 1)

    grid_spec = pltpu.PrefetchScalarGridSpec(
        num_scalar_prefetch=2,
        grid=(n_rows // tm, nf),
        in_specs=[pl.BlockSpec((tm, d), lambda i, f, be, nb: (jnp.minimum(i, nb[0] - 1), 0)),
                  pl.BlockSpec((None, d, tf), lambda i, f, be, nb: (wmap(i, f, be, nb)[0], 0, wmap(i, f, be, nb)[1])),
                  pl.BlockSpec((None, d, tf), lambda i, f, be, nb: (wmap(i, f, be, nb)[0], 0, wmap(i, f, be, nb)[1])),
                  pl.BlockSpec((None, tf, d), lambda i, f, be, nb: (wmap(i, f, be, nb)[0], wmap(i, f, be, nb)[1], 0))],
        out_specs=pl.BlockSpec((tm, d), lambda i, f, be, nb: (i, 0)),
        scratch_shapes=[pltpu.VMEM((tm, d), BF16), pltpu.VMEM((tm, d), F32)],
    )
    return pl.pallas_call(
        _moe_kernel,
        grid_spec=grid_spec,
        out_shape=jax.ShapeDtypeStruct((n_rows, d), F32),
        compiler_params=_cparams(("arbitrary", "arbitrary")),
        name="moe_grouped_swiglu",
    )(block_expert, n_active, x_buf, wg.astype(BF16), wu.astype(BF16), wd.astype(BF16))


def _combine_kernel(d1_ref, d2_ref, ybuf_ref, info_ref, x_ref, g_ref, mod_ref, o_ref, y1_ref, y2_ref, sem, *, tm):
    def row_copy(r, dref, dst):
        return pltpu.make_async_copy(ybuf_ref.at[pl.ds(dref[0, 0, r], 1), :], dst.at[pl.ds(r, 1), :], sem)

    def start(r, carry):
        row_copy(r, d1_ref, y1_ref).start()
        row_copy(r, d2_ref, y2_ref).start()
        return carry

    def wait(r, carry):
        row_copy(r, d1_ref, y1_ref).wait()
        row_copy(r, d2_ref, y2_ref).wait()
        return carry

    lax.fori_loop(0, tm, start, 0)
    lax.fori_loop(0, tm, wait, 0)
    d = D_MODEL
    info = info_ref[...]
    y = y1_ref[...] * info[:, 4:5] + y2_ref[...] * info[:, 5:6]
    o_ref[...] = x_ref[...] + mod_ref[:, 5 * d:6 * d] * _rms(y, g_ref[...])


def _combine(y_buf, dest1, dest2, info, x2, g_post, mod_l, seq):
    t, d = x2.shape
    tm = _tile(seq, 512)
    nt = t // tm
    tpb = seq // tm
    smem = lambda: pl.BlockSpec((1, 1, tm), lambda i: (i, 0, 0), memory_space=pltpu.SMEM)
    return pl.pallas_call(
        functools.partial(_combine_kernel, tm=tm),
        grid=(nt,),
        in_specs=[smem(), smem(), pl.BlockSpec(memory_space=pl.ANY),
                  pl.BlockSpec((tm, LANES), lambda i: (i, 0)),
                  pl.BlockSpec((tm, d), lambda i: (i, 0)),
                  pl.BlockSpec((1, d), lambda i: (0, 0)),
                  pl.BlockSpec((None, 1, 6 * d), lambda i: (i // tpb, 0, 0))],
        out_specs=pl.BlockSpec((tm, d), lambda i: (i, 0)),
        out_shape=jax.ShapeDtypeStruct((t, d), F32),
        scratch_shapes=[pltpu.VMEM((tm, d), F32), pltpu.VMEM((tm, d), F32), pltpu.SemaphoreType.DMA(())],
        compiler_params=_cparams(("arbitrary",)),
        name="moe_combine",
    )(dest1.reshape(nt, 1, tm), dest2.reshape(nt, 1, tm), y_buf, info, x2, g_post.reshape(1, d), mod_l)


def _routed_ffn(h, logits, wg, wu, wd, x2, g_post, mod_l, seq):
    t, d = h.shape
    tm = MOE_TILE
    info, counts = _route(logits)
    counts = counts.astype(jnp.int32)
    padded = ((counts + tm - 1) // tm) * tm
    pad_end = jnp.cumsum(padded)
    pad_start = pad_end - padded
    e1 = info[:, 0].astype(jnp.int32)
    e2 = info[:, 1].astype(jnp.int32)
    dest1 = pad_start[e1] + info[:, 2].astype(jnp.int32)
    dest2 = pad_start[e2] + info[:, 3].astype(jnp.int32)
    n_rows = t * TOP_K + N_EXPERTS * tm
    n_blocks = n_rows // tm
    block_start = jnp.arange(n_blocks, dtype=jnp.int32) * tm
    block_expert = jnp.clip(jnp.searchsorted(pad_end, block_start, side='right'), 0, N_EXPERTS - 1).astype(jnp.int32)
    n_active = (pad_end[-1] // tm).astype(jnp.int32).reshape(1)
    x_buf = _dispatch(h, dest1, dest2, n_rows)
    y_buf = _moe_ffn(x_buf, block_expert, n_active, wg, wu, wd)
    return _combine(y_buf, dest1, dest2, info, x2, g_post, mod_l, seq)


def kernel(x, c, positions, ada_w, ada_b, norm_pre_mix, norm_post_mix, norm_pre_ffn, norm_post_ffn, w_in, ret_norm, ssm_a_re, ssm_a_im, ssm_b_re, ssm_b_im, ssm_c_re, ssm_c_im, ssm_d, ssm_log_dt, ssm_glu_w, ssm_glu_b, ssm_norm, mla_q_norm, mla_w_uq, mla_kv_norm, mla_w_ukv, mla_norm, w_out, ffn_w_gate, ffn_w_up, ffn_w_down, moe_router, moe_w_gate, moe_w_up, moe_w_down):
    bsz, seq, d = x.shape
    depth = ada_w.shape[0]
    assert d == D_MODEL and seq % SSM_CHUNK == 0 and seq % RET_CHUNK == 0
    t = bsz * seq
    x2 = x.reshape(t, d)
    mod = _modulation(c, ada_w, ada_b)
    cos128, sin128 = _rope_tables(positions)
    ssm_steps = max(1, int(math.ceil(math.log2(seq // SSM_CHUNK))))
    for layer in range(depth):
        mod_l = mod[layer]
        r, u, cq, ckv, kr = _inproj(x2, norm_pre_mix[layer], mod_l, w_in[layer], seq)
        ret = _retention(r, cos128, sin128, ret_norm[layer], bsz, seq)
        ops = _ssm_operators(ssm_a_re[layer], ssm_a_im[layer], ssm_b_re[layer], ssm_b_im[layer],
                             ssm_c_re[layer], ssm_c_im[layer], ssm_log_dt[layer], ssm_steps)
        ssm = _ssm_post(_ssm_scan(u, ops, seq), u, ssm_d[layer], ssm_glu_w[layer], ssm_glu_b[layer], ssm_norm[layer])
        q3, k3, v3 = _mla_prep(cq, ckv, kr, cos128, sin128, mla_q_norm[layer], mla_w_uq[layer],
                               mla_kv_norm[layer], mla_w_ukv[layer])
        att = _attention(q3, k3, v3, bsz, seq)
        j = layer // 2
        if layer % 2 == 0:
            x2, h = _outproj(x2, ret, ssm, att, mla_norm[layer], w_out[layer], norm_post_mix[layer], mod_l,
                             norm_pre_ffn[layer], seq)
            x2 = _dense_ffn(h, ffn_w_gate[j], ffn_w_up[j], ffn_w_down[j], x2, norm_post_ffn[layer], mod_l, seq)
        else:
            x2, h, logits = _outproj(x2, ret, ssm, att, mla_norm[layer], w_out[layer], norm_post_mix[layer], mod_l,
                                     norm_pre_ffn[layer], seq, router=moe_router[j])
            x2 = _routed_ffn(h, logits, moe_w_gate[j], moe_w_up[j], moe_w_down[j], x2, norm_post_ffn[layer],
                             mod_l, seq)
    return x2.reshape(bsz, seq, d)
```

```python
import functools
import math

import numpy as np
import jax
import jax.numpy as jnp
from jax import lax
from jax.experimental import pallas as pl
from jax.experimental.pallas import tpu as pltpu

D_MODEL = 1024
RET_HEADS = 4
RET_HEAD_DIM = 64
RET_WIDTH = RET_HEADS * RET_HEAD_DIM
RET_CHUNK = 128
SSM_GROUP_CH = 16
SSM_GROUPS = 16
SSM_WIDTH = SSM_GROUPS * SSM_GROUP_CH
SSM_STATE = 64
MLA_HEADS = 4
MLA_NOPE = 128
MLA_ROPE = 64
MLA_V = 128
MLA_QK = MLA_NOPE + MLA_ROPE
MLA_WIDTH = MLA_HEADS * MLA_V
MLA_Q_RANK = 256
MLA_KV_RANK = 128
ROPE_DIM = 64
ROPE_BASE = 10000.0
D_FF = 3584
N_EXPERTS = 8
TOP_K = 2
EPS = 1e-6
NEG_INF = -1e30
IN_COLS = 4 * RET_WIDTH + SSM_WIDTH + MLA_Q_RANK + MLA_KV_RANK + MLA_ROPE

LANES = 128
VMEM_LIMIT_BYTES = 56 * 1024 * 1024

SSM_CHUNK = 64
MOE_TILE = 512
FF_TILE = 512

BF16 = jnp.bfloat16
F32 = jnp.float32


def _cparams(sem):
    return pltpu.CompilerParams(dimension_semantics=sem, vmem_limit_bytes=VMEM_LIMIT_BYTES)


def _tile(n, pref):
    t = min(n, pref)
    while n % t:
        t //= 2
    return t


def _dot(a, b):
    return jnp.dot(a, b, preferred_element_type=F32)


def _dot3(a, b):
    a_hi = a.astype(BF16)
    a_lo = (a - a_hi.astype(F32)).astype(BF16)
    b_hi = b.astype(BF16)
    b_lo = (b - b_hi.astype(F32)).astype(BF16)
    return _dot(a_hi, b_hi) + (_dot(a_hi, b_lo) + _dot(a_lo, b_hi))


def _rms(x, g):
    return x * lax.rsqrt(jnp.mean(x * x, axis=-1, keepdims=True) + EPS) * g


def _silu(x):
    return x * (1.0 / (1.0 + jnp.exp(-x)))


def _rope128(x, cos, sin_signed):
    lane = lax.broadcasted_iota(jnp.int32, x.shape, 1)
    first_half = (lane % ROPE_DIM) < (ROPE_DIM // 2)
    partner = jnp.where(first_half, pltpu.roll(x, LANES - ROPE_DIM // 2, 1), pltpu.roll(x, ROPE_DIM // 2, 1))
    return x * cos + partner * sin_signed


def _mod_kernel(c_ref, w_ref, b_ref, o_ref):
    cond = _silu(c_ref[...])
    o_ref[...] = _dot3(cond, w_ref[...]) + b_ref[...]


def _modulation(c, ada_w, ada_b):
    depth, d, n = ada_w.shape
    bsz = c.shape[0]
    rows = 8
    c_pad = jnp.zeros((rows, d), F32).at[:bsz].set(c)
    tn = _tile(n, 1536)
    out = pl.pallas_call(
        _mod_kernel,
        grid=(depth, n // tn),
        in_specs=[pl.BlockSpec((rows, d), lambda l, j: (0, 0)),
                  pl.BlockSpec((None, d, tn), lambda l, j: (l, 0, j)),
                  pl.BlockSpec((None, 1, tn), lambda l, j: (l, 0, j))],
        out_specs=pl.BlockSpec((None, rows, tn), lambda l, j: (l, 0, j)),
        out_shape=jax.ShapeDtypeStruct((depth, rows, n), F32),
        compiler_params=_cparams(("arbitrary", "arbitrary")),
        name="adaln_mod",
    )(c_pad, ada_w, ada_b.reshape(depth, 1, n))
    return out[:, :bsz].reshape(depth, bsz, 1, n)


def _rope_kernel(pos_ref, inv_ref, cos_ref, sin_ref):
    pos = pos_ref[...].astype(F32)
    lane = lax.broadcasted_iota(jnp.int32, cos_ref.shape, 1)
    half = ROPE_DIM // 2
    p = jnp.where(lane < half, pos[:, 0:1],
                  jnp.where(lane < 2 * half, pos[:, 1:2],
                            jnp.where(lane < 3 * half, pos[:, 2:3], pos[:, 3:4])))
    ang = p * inv_ref[...]
    cos_ref[...] = jnp.cos(ang)
    sin_ref[...] = jnp.sin(ang)


def _rope_tables(positions):
    t = positions.size
    half = ROPE_DIM // 2
    per_row = LANES // half
    inv = ROPE_BASE ** (-jnp.arange(0, ROPE_DIM, 2, dtype=F32) / ROPE_DIM)
    inv_row = jnp.tile(inv, per_row).reshape(1, LANES)
    rows = t // per_row
    tr = _tile(rows, 1024)
    cos, sin = pl.pallas_call(
        _rope_kernel,
        grid=(rows // tr,),
        in_specs=[pl.BlockSpec((tr, per_row), lambda i: (i, 0)),
                  pl.BlockSpec((1, LANES), lambda i: (0, 0))],
        out_specs=[pl.BlockSpec((tr, LANES), lambda i: (i, 0))] * 2,
        out_shape=[jax.ShapeDtypeStruct((rows, LANES), F32)] * 2,
        compiler_params=_cparams(("arbitrary",)),
        name="rope_tables",
    )(positions.reshape(rows, per_row), inv_row)
    cos = cos.reshape(t, half)
    sin = sin.reshape(t, half)
    return jnp.tile(cos, (1, per_row)), jnp.concatenate([-sin, sin, -sin, sin], axis=1)


def _inproj_kernel(x_ref, g_ref, mod_ref, w_ref, ret_ref, u_ref, cq_ref, ckv_ref, kr_ref):
    d = D_MODEL
    x = x_ref[...]
    h = _rms(x, g_ref[...]) * (1.0 + mod_ref[:, d:2 * d]) + mod_ref[:, 0:d]
    p = _dot(h.astype(BF16), w_ref[...])
    o = 4 * RET_WIDTH
    ret_ref[...] = p[:, :o].astype(BF16)
    u_ref[...] = p[:, o:o + SSM_WIDTH].astype(BF16)
    o += SSM_WIDTH
    cq_ref[...] = p[:, o:o + MLA_Q_RANK].astype(BF16)
    o += MLA_Q_RANK
    ckv_ref[...] = p[:, o:o + MLA_KV_RANK].astype(BF16)
    o += MLA_KV_RANK
    kr_ref[...] = p[:, o:o + MLA_ROPE].astype(BF16)


def _inproj(x2, g, mod_l, w_in, seq):
    t, d = x2.shape
    tm = _tile(seq, 512)
    tpb = seq // tm
    widths = (4 * RET_WIDTH, SSM_WIDTH, MLA_Q_RANK, MLA_KV_RANK, MLA_ROPE)
    return pl.pallas_call(
        _inproj_kernel,
        grid=(t // tm,),
        in_specs=[pl.BlockSpec((tm, d), lambda i: (i, 0)),
                  pl.BlockSpec((1, d), lambda i: (0, 0)),
                  pl.BlockSpec((None, 1, 6 * d), lambda i: (i // tpb, 0, 0)),
                  pl.BlockSpec((d, IN_COLS), lambda i: (0, 0))],
        out_specs=[pl.BlockSpec((tm, w), lambda i: (i, 0)) for w in widths],
        out_shape=[jax.ShapeDtypeStruct((t, w), BF16) for w in widths],
        compiler_params=_cparams(("arbitrary",)),
        name="mixer_inproj",
    )(x2, g.reshape(1, d), mod_l, w_in.astype(BF16))


def _retention_kernel(r_ref, cos_ref, sin_ref, dmask_ref, qdec_ref, kdec_ref, cdec_ref, bd_ref, gn_ref,
                      o_ref, state_ref, *, chunks):
    w = RET_WIDTH

    @pl.when(pl.program_id(1) == 0)
    def _():
        state_ref[...] = jnp.zeros_like(state_ref)

    lane = lax.broadcasted_iota(jnp.int32, (RET_CHUNK, w), 1)
    bd = bd_ref[...]
    for c in range(chunks):
        rows = pl.ds(c * RET_CHUNK, RET_CHUNK)
        cos = cos_ref[rows, :]
        sin = sin_ref[rows, :]

        def rope(v):
            return jnp.concatenate([_rope128(v[:, :LANES], cos, sin), _rope128(v[:, LANES:], cos, sin)], axis=1)

        q = rope(r_ref[rows, 0:w].astype(F32))
        k = rope(r_ref[rows, w:2 * w].astype(F32)) * (RET_HEAD_DIM ** -0.5)
        v = r_ref[rows, 2 * w:3 * w]
        gate = r_ref[rows, 3 * w:4 * w].astype(F32)
        kb = k.astype(BF16)
        state = state_ref[...]
        y = _dot((q * qdec_ref[...]).astype(BF16), state.astype(BF16))
        for h in range(RET_HEADS):
            in_head = (lane // RET_HEAD_DIM) == h
            qh = jnp.where(in_head, q, 0.0).astype(BF16)
            s = lax.dot_general(qh, kb, (((1,), (1,)), ((), ())), preferred_element_type=F32)
            s = s * dmask_ref[h]
            y = y + jnp.where(in_head, _dot(s.astype(BF16), v), 0.0)
        kv = lax.dot_general((k * kdec_ref[...]).astype(BF16), v, (((0,), (0,)), ((), ())),
                             preferred_element_type=F32)
        state_ref[...] = cdec_ref[...] * state + bd * kv
        y2 = y * y
        y2_hi = y2.astype(BF16)
        y2_lo = (y2 - y2_hi.astype(F32)).astype(BF16)
        bdb = bd.astype(BF16)
        ms = (_dot(y2_hi, bdb) + _dot(y2_lo, bdb)) * (1.0 / RET_HEAD_DIM)
        yn = y * lax.rsqrt(ms + EPS) * gn_ref[...]
        o_ref[rows, :] = (_silu(gate) * yn).astype(BF16)


def _retention_consts():
    c = RET_CHUNK
    log_g = np.log1p(-(2.0 ** (-5.0 - np.arange(RET_HEADS, dtype=np.float32)))).astype(np.float32)
    i = np.arange(c, dtype=np.float32)
    diff = i[:, None] - i[None, :]
    dmask = np.where(diff >= 0, np.exp(log_g[:, None, None] * np.maximum(diff, 0.0)), 0.0).astype(np.float32)
    k_dec = np.exp(log_g[None, :] * (c - 1 - i)[:, None]).astype(np.float32)
    q_dec = np.exp(log_g[None, :] * (i + 1.0)[:, None]).astype(np.float32)
    c_dec = np.exp(log_g * c).astype(np.float32)
    rep = lambda a: np.repeat(a, RET_HEAD_DIM, axis=-1)
    head = np.arange(RET_WIDTH) // RET_HEAD_DIM
    bd = (head[:, None] == head[None, :]).astype(np.float32)
    return dmask, rep(q_dec), rep(k_dec), rep(c_dec[None, :]), bd


def _retention(r, cos128, sin128, ret_norm, bsz, seq):
    t = r.shape[0]
    w = RET_WIDTH
    tr = _tile(seq, 512)
    chunks = tr // RET_CHUNK
    nt = seq // tr
    dmask, qdec, kdec, cdec, bd = (jnp.asarray(a) for a in _retention_consts())
    const = lambda shape: pl.BlockSpec(shape, lambda b, i: (0,) * len(shape))
    return pl.pallas_call(
        functools.partial(_retention_kernel, chunks=chunks),
        grid=(bsz, nt),
        in_specs=[pl.BlockSpec((tr, 4 * w), lambda b, i: (b * nt + i, 0)),
                  pl.BlockSpec((tr, LANES), lambda b, i: (b * nt + i, 0)),
                  pl.BlockSpec((tr, LANES), lambda b, i: (b * nt + i, 0)),
                  const((RET_HEADS, RET_CHUNK, RET_CHUNK)),
                  const((RET_CHUNK, w)), const((RET_CHUNK, w)), const((1, w)), const((w, w)), const((1, w))],
        out_specs=pl.BlockSpec((tr, w), lambda b, i: (b * nt + i, 0)),
        out_shape=jax.ShapeDtypeStruct((t, w), BF16),
        scratch_shapes=[pltpu.VMEM((w, w), F32)],
        compiler_params=_cparams(("arbitrary", "arbitrary")),
        name="retention",
    )(r, cos128, sin128, dmask, qdec, kdec, cdec, bd, ret_norm.reshape(1, w))


def _expand(x, sel):
    hi = x.astype(BF16)
    lo = (x - hi.astype(F32)).astype(BF16)
    return _dot(hi, sel) + _dot(lo, sel)


def _ssm_kernel(u_ref, acol_ref, arow_ref, ldt_ref, b_ref, bt_ref, ct_ref, esel_ref, ech_ref, y_ref, toep_ref,
                *, chunks_per_seq, steps):
    tc, hg, p = SSM_CHUNK, SSM_GROUP_CH, SSM_STATE
    k = tc * hg
    dt = jnp.exp(ldt_ref[...])
    ar, ai = acol_ref[:, 0:1], acol_ref[:, 1:2]
    n = lax.broadcasted_iota(jnp.int32, (p, LANES), 1).astype(F32)
    mag = jnp.exp(ar * dt * n)
    pw_re, pw_im = mag * jnp.cos(ai * dt * n), mag * jnp.sin(ai * dt * n)
    def zoh(a_r, a_i):
        m1 = jnp.exp(a_r * dt)
        l_r, l_i = m1 * jnp.cos(a_i * dt), m1 * jnp.sin(a_i * dt)
        den = a_r * a_r + a_i * a_i
        return ((l_r - 1.0) * a_r + l_i * a_i) / den, (l_i * a_r - (l_r - 1.0) * a_i) / den

    f_re, f_im = zoh(ar, ai)
    bb_re = f_re * b_ref[0] - f_im * b_ref[1]
    bb_im = f_re * b_ref[1] + f_im * b_ref[0]
    fr_re, fr_im = zoh(arow_ref[0:1, 0:p], arow_ref[1:2, 0:p])
    bbt_re = fr_re * bt_ref[0] - fr_im * bt_ref[1]
    bbt_im = fr_re * bt_ref[1] + fr_im * bt_ref[0]
    ech = ech_ref[...]
    pt_re, pt_im = _expand(pw_re, esel_ref[0]), _expand(pw_im, esel_ref[0])
    p1_re, p1_im = _expand(pw_re, esel_ref[1]), _expand(pw_im, esel_ref[1])
    pr_re, pr_im = _expand(pw_re, esel_ref[2]), _expand(pw_im, esel_ref[2])
    bbr, bbi = _expand(bb_re, ech), _expand(bb_im, ech)
    cr, ci = _expand(ct_ref[0], ech), _expand(ct_ref[1], ech)
    z_re = cr * pt_re - ci * pt_im
    z_im = cr * pt_im + ci * pt_re
    strip = _dot3(bbt_re, z_re) - _dot3(bbt_im, z_im)
    w_in_t = jnp.concatenate([pr_re * bbr - pr_im * bbi, pr_re * bbi + pr_im * bbr], axis=0).astype(BF16)
    v_out = jnp.concatenate([cr * p1_re - ci * p1_im, -(cr * p1_im + ci * p1_re)], axis=0).astype(BF16)
    padded = jnp.concatenate([jnp.zeros((hg, k), F32), strip], axis=1)
    for s in range(tc):
        toep_ref[s * hg:(s + 1) * hg, :] = padded[:, k - s * hg:2 * k - s * hg].astype(BF16)
    dist = (tc * (1 << lax.broadcasted_iota(jnp.int32, (steps, 2 * p), 0))).astype(F32)
    lane = lax.broadcasted_iota(jnp.int32, (steps, 2 * p), 1)
    amag = jnp.exp(arow_ref[0:1, :] * dt * dist)
    a_cos = amag * jnp.cos(arow_ref[1:2, :] * dt * dist)
    a_sin = jnp.where(lane < p, -1.0, 1.0) * amag * jnp.sin(arow_ref[1:2, :] * dt * dist)

    u = u_ref[...]
    x = lax.dot_general(u, w_in_t, (((1,), (1,)), ((), ())), preferred_element_type=F32)
    row = lax.broadcasted_iota(jnp.int32, x.shape, 0) % chunks_per_seq

    def shift_rows(a, d):
        return jnp.where(row >= d, pltpu.roll(a, d, 0), 0.0)

    for j in range(steps):
        prev = shift_rows(x, 1 << j)
        x = x + prev * a_cos[j:j + 1, :] + pltpu.roll(prev, p, 1) * a_sin[j:j + 1, :]
    x_in = shift_rows(x, 1)
    y_ref[...] = _dot(u, toep_ref[...]) + _dot(x_in.astype(BF16), v_out)


def _ssm_params(a_re, a_im, b_re, b_im, c_re, c_im, log_dt):
    a_col = jnp.stack([a_re, a_im], axis=-1)
    a_row = jnp.stack([jnp.concatenate([a_re, a_re], -1), jnp.concatenate([a_im, a_im], -1)], axis=2)
    b = jnp.stack([b_re, b_im], axis=2)
    bt = jnp.swapaxes(b, -1, -2)
    ct = jnp.swapaxes(jnp.stack([c_re, c_im], axis=2), -1, -2)
    return a_col, a_row, log_dt[..., None, None], b, bt, ct


def _ssm_selectors():
    tc, hg = SSM_CHUNK, SSM_GROUP_CH
    t_of_lane = np.arange(tc * hg) // hg
    n = np.arange(LANES)[:, None]
    esel = np.stack([n == t_of_lane, n == t_of_lane + 1, n == tc - 1 - t_of_lane]).astype(np.float32)
    ech = (np.arange(hg)[:, None] == (np.arange(tc * hg) % hg)[None, :]).astype(np.float32)
    return jnp.asarray(esel, BF16), jnp.asarray(ech, BF16)


def _ssm_scan(u, params, layer, seq):
    t = u.shape[0]
    g, hg, tc, p = SSM_GROUPS, SSM_GROUP_CH, SSM_CHUNK, SSM_STATE
    nc = t // tc
    cps = seq // tc
    steps = max(1, int(math.ceil(math.log2(cps))))
    k = tc * hg
    esel, ech = _ssm_selectors()
    ug = u.reshape(nc, tc, g, hg).transpose(2, 0, 1, 3).reshape(g, nc, k)
    per_group = lambda *blk: pl.BlockSpec((None, None) + blk, lambda i: (layer, i) + (0,) * len(blk))
    y = pl.pallas_call(
        functools.partial(_ssm_kernel, chunks_per_seq=cps, steps=steps),
        grid=(g,),
        in_specs=[pl.BlockSpec((None, nc, k), lambda i: (i, 0, 0)),
                  per_group(p, 2), per_group(2, 2 * p), per_group(1, 1),
                  per_group(2, p, hg), per_group(2, hg, p), per_group(2, p, hg),
                  pl.BlockSpec((3, LANES, k), lambda i: (0, 0, 0)),
                  pl.BlockSpec((hg, k), lambda i: (0, 0))],
        out_specs=pl.BlockSpec((None, nc, k), lambda i: (i, 0, 0)),
        out_shape=jax.ShapeDtypeStruct((g, nc, k), F32),
        scratch_shapes=[pltpu.VMEM((k, k), BF16)],
        compiler_params=_cparams(("arbitrary",)),
        name="s5_chunk_scan",
    )(ug, *params, esel, ech)
    return y.reshape(g, nc, tc, hg).transpose(1, 2, 0, 3).reshape(t, g * hg)


def _ssm_post_kernel(y_ref, u_ref, d_ref, gw_ref, gb_ref, gn_ref, o_ref):
    y = y_ref[...] + d_ref[...] * u_ref[...].astype(F32)
    y = 0.5 * y * (1.0 + jnp.tanh(math.sqrt(2.0 / math.pi) * (y + 0.044715 * (y * y * y))))
    z = _dot(y.astype(BF16), gw_ref[...]) + gb_ref[...]
    y = y * (1.0 / (1.0 + jnp.exp(-z)))
    o_ref[...] = _rms(y, gn_ref[...]).astype(BF16)


def _ssm_post(y, u, d, glu_w, glu_b, ssm_norm):
    t, w = y.shape
    tm = _tile(t, 1024)
    row = lambda: pl.BlockSpec((1, w), lambda i: (0, 0))
    return pl.pallas_call(
        _ssm_post_kernel,
        grid=(t // tm,),
        in_specs=[pl.BlockSpec((tm, w), lambda i: (i, 0)), pl.BlockSpec((tm, w), lambda i: (i, 0)),
                  row(), pl.BlockSpec((w, w), lambda i: (0, 0)), row(), row()],
        out_specs=pl.BlockSpec((tm, w), lambda i: (i, 0)),
        out_shape=jax.ShapeDtypeStruct((t, w), BF16),
        compiler_params=_cparams(("arbitrary",)),
        name="s5_gelu_glu_norm",
    )(y, u, d.reshape(1, w), glu_w.astype(BF16), glu_b.reshape(1, w), ssm_norm.reshape(1, w))


def _mla_prep_kernel(cq_ref, ckv_ref, kr_ref, cos_ref, sin_ref, qn_ref, kvn_ref, wq_ref, wkv_ref,
                     q_ref, k_ref, v_ref):
    cos = cos_ref[...]
    sin = sin_ref[...]
    nope_w = MLA_HEADS * MLA_NOPE
    q = _dot(_rms(cq_ref[...].astype(F32), qn_ref[...]).astype(BF16), wq_ref[...])
    q = q * (MLA_QK ** -0.5 * math.log2(math.e))
    q_rope = jnp.concatenate([_rope128(q[:, nope_w:nope_w + LANES], cos, sin),
                              _rope128(q[:, nope_w + LANES:], cos, sin)], axis=1)
    kv = _dot(_rms(ckv_ref[...].astype(F32), kvn_ref[...]).astype(BF16), wkv_ref[...])
    kr = kr_ref[...].astype(F32)
    k_pe = _rope128(jnp.concatenate([kr, kr], axis=1), cos, sin)[:, :MLA_ROPE]
    for h in range(MLA_HEADS):
        q_ref[h] = jnp.concatenate([q[:, h * MLA_NOPE:(h + 1) * MLA_NOPE],
                                    q_rope[:, h * MLA_ROPE:(h + 1) * MLA_ROPE]], axis=1).astype(BF16)
        k_ref[h] = jnp.concatenate([kv[:, h * MLA_NOPE:(h + 1) * MLA_NOPE], k_pe], axis=1).astype(BF16)
        v_ref[h, :, 0:MLA_V] = kv[:, nope_w + h * MLA_V:nope_w + (h + 1) * MLA_V].astype(BF16)
        v_ref[h, :, MLA_V:] = jnp.ones((kv.shape[0], LANES), BF16)


def _mla_prep(cq, ckv, kr, cos128, sin128, q_norm, w_uq, kv_norm, w_ukv):
    t = cq.shape[0]
    tm = _tile(t, 512)
    hh = MLA_HEADS
    wq = w_uq.reshape(MLA_Q_RANK, hh, MLA_QK)
    wq = jnp.concatenate([wq[:, :, :MLA_NOPE].reshape(MLA_Q_RANK, -1), wq[:, :, MLA_NOPE:].reshape(MLA_Q_RANK, -1)], 1)
    wkv = w_ukv.reshape(MLA_KV_RANK, hh, MLA_NOPE + MLA_V)
    wkv = jnp.concatenate([wkv[:, :, :MLA_NOPE].reshape(MLA_KV_RANK, -1), wkv[:, :, MLA_NOPE:].reshape(MLA_KV_RANK, -1)], 1)
    tok = lambda w: pl.BlockSpec((tm, w), lambda i: (i, 0))
    full = lambda a, b: pl.BlockSpec((a, b), lambda i: (0, 0))
    return pl.pallas_call(
        _mla_prep_kernel,
        grid=(t // tm,),
        in_specs=[tok(MLA_Q_RANK), tok(MLA_KV_RANK), tok(MLA_ROPE), tok(LANES), tok(LANES),
                  full(1, MLA_Q_RANK), full(1, MLA_KV_RANK),
                  full(MLA_Q_RANK, hh * MLA_QK), full(MLA_KV_RANK, hh * (MLA_NOPE + MLA_V))],
        out_specs=[pl.BlockSpec((hh, tm, MLA_QK), lambda i: (0, i, 0)),
                   pl.BlockSpec((hh, tm, MLA_QK), lambda i: (0, i, 0)),
                   pl.BlockSpec((hh, tm, MLA_V + LANES), lambda i: (0, i, 0))],
        out_shape=[jax.ShapeDtypeStruct((hh, t, MLA_QK), BF16), jax.ShapeDtypeStruct((hh, t, MLA_QK), BF16),
                   jax.ShapeDtypeStruct((hh, t, MLA_V + LANES), BF16)],
        compiler_params=_cparams(("arbitrary",)),
        name="mla_prep",
    )(cq, ckv, kr, cos128, sin128, q_norm.reshape(1, -1), kv_norm.reshape(1, -1), wq.astype(BF16), wkv.astype(BF16))


def _attn_kernel(q_ref, k_ref, v_ref, o_ref, m_ref, l_ref, acc_ref, *, tq, th, tk):
    qi = pl.program_id(2)
    m_ref[...] = jnp.full_like(m_ref, NEG_INF)
    l_ref[...] = jnp.zeros_like(l_ref)
    acc_ref[...] = jnp.zeros_like(acc_ref)

    def step(k, v, half, mask_off):
        rows = pl.ds(half * th, th)
        s = lax.dot_general(q_ref[rows, :], k, (((1,), (1,)), ((), ())), preferred_element_type=F32)
        if mask_off is not None:
            r = lax.broadcasted_iota(jnp.int32, s.shape, 0)
            c = lax.broadcasted_iota(jnp.int32, s.shape, 1)
            s = jnp.where(c + mask_off <= r, s, NEG_INF)
        m_prev = m_ref[rows, :]
        m_new = jnp.maximum(m_prev, jnp.max(s, axis=-1, keepdims=True))
        alpha = jnp.exp2(m_prev - m_new)
        p = jnp.exp2(s - jnp.tile(m_new, (1, tk // LANES)))
        pv = _dot(p.astype(BF16), v)
        acc_ref[rows, :] = alpha * acc_ref[rows, :] + pv[:, :MLA_V]
        l_ref[rows, :] = alpha * l_ref[rows, :] + pv[:, MLA_V:]
        m_ref[rows, :] = m_new

    def load_kv(off):
        off = pl.multiple_of(off, tk)
        return k_ref[pl.ds(off, tk), :], v_ref[pl.ds(off, tk), :]

    def full_block(j, carry):
        k, v = load_kv(j * tk)
        for half in range(tq // th):
            step(k, v, half, None)
        return carry

    lax.fori_loop(0, qi * (tq // tk), full_block, 0)
    for d in range(tq // tk):
        k, v = load_kv(qi * tq + d * tk)
        for half in range(tq // th):
            r0, c0 = half * th, d * tk
            if c0 + tk - 1 <= r0:
                step(k, v, half, None)
            elif c0 <= r0 + th - 1:
                step(k, v, half, c0 - r0)
    o_ref[...] = (acc_ref[...] / l_ref[...]).astype(o_ref.dtype)


def _attention(q3, k3, v3, bsz, seq):
    hh, t, _ = q3.shape
    tq = _tile(seq, 1024)
    th = tq // 2
    tk = th
    nq = seq // tq
    vw = v3.shape[-1]
    return pl.pallas_call(
        functools.partial(_attn_kernel, tq=tq, th=th, tk=tk),
        grid=(bsz, hh, nq),
        in_specs=[pl.BlockSpec((None, tq, MLA_QK), lambda b, h, i: (h, b * nq + i, 0)),
                  pl.BlockSpec((None, seq, MLA_QK), lambda b, h, i: (h, b, 0)),
                  pl.BlockSpec((None, seq, vw), lambda b, h, i: (h, b, 0))],
        out_specs=pl.BlockSpec((tq, MLA_V), lambda b, h, i: (b * nq + i, h)),
        out_shape=jax.ShapeDtypeStruct((t, hh * MLA_V), BF16),
        scratch_shapes=[pltpu.VMEM((tq, LANES), F32), pltpu.VMEM((tq, LANES), F32), pltpu.VMEM((tq, MLA_V), F32)],
        compiler_params=_cparams(("arbitrary", "arbitrary", "arbitrary")),
        name="mla_flash_attention",
    )(q3, k3, v3)


def _outproj_kernel(*refs, routed):
    if routed:
        (x_ref, ret_ref, ssm_ref, att_ref, an_ref, w_ref, gpost_ref, mod_ref, gpre_ref, router_ref,
         xo_ref, h_ref, logit_ref) = refs
    else:
        x_ref, ret_ref, ssm_ref, att_ref, an_ref, w_ref, gpost_ref, mod_ref, gpre_ref, xo_ref, h_ref = refs
    d = D_MODEL
    att = _rms(att_ref[...].astype(F32), an_ref[...]).astype(BF16)
    o1 = RET_WIDTH
    o2 = RET_WIDTH + SSM_WIDTH
    y = _dot(ret_ref[...], w_ref[0:o1, :]) + _dot(ssm_ref[...], w_ref[o1:o2, :]) + _dot(att, w_ref[o2:, :])
    x = x_ref[...] + mod_ref[:, 2 * d:3 * d] * _rms(y, gpost_ref[...])
    xo_ref[...] = x
    h = _rms(x, gpre_ref[...]) * (1.0 + mod_ref[:, 4 * d:5 * d]) + mod_ref[:, 3 * d:4 * d]
    h_ref[...] = h.astype(h_ref.dtype)
    if routed:
        logit_ref[...] = _dot3(h, router_ref[...])


def _outproj(x2, ret, ssm, att, mla_norm, w_out, g_post, mod_l, g_pre, seq, router=None):
    t, d = x2.shape
    tm = _tile(seq, 512)
    tpb = seq // tm
    routed = router is not None
    tok = lambda w: pl.BlockSpec((tm, w), lambda i: (i, 0))
    full = lambda a, b: pl.BlockSpec((a, b), lambda i: (0, 0))
    in_specs = [tok(d), tok(RET_WIDTH), tok(SSM_WIDTH), tok(MLA_WIDTH), full(1, MLA_WIDTH), full(d, d), full(1, d),
                pl.BlockSpec((None, 1, 6 * d), lambda i: (i // tpb, 0, 0)), full(1, d)]
    args = [x2, ret, ssm, att, mla_norm.reshape(1, -1), w_out.astype(BF16), g_post.reshape(1, d), mod_l,
            g_pre.reshape(1, d)]
    out_specs = [tok(d), tok(d)]
    out_shape = [jax.ShapeDtypeStruct((t, d), F32), jax.ShapeDtypeStruct((t, d), F32 if routed else BF16)]
    if routed:
        in_specs.append(full(d, LANES))
        args.append(jnp.zeros((d, LANES), F32).at[:, :N_EXPERTS].set(router))
        out_specs.append(tok(LANES))
        out_shape.append(jax.ShapeDtypeStruct((t, LANES), F32))
    return pl.pallas_call(
        functools.partial(_outproj_kernel, routed=routed),
        grid=(t // tm,),
        in_specs=in_specs, out_specs=out_specs, out_shape=out_shape,
        compiler_params=_cparams(("arbitrary",)),
        name="mixer_outproj_routed" if routed else "mixer_outproj",
    )(*args)


def _ffn_kernel(h_ref, wg_ref, wu_ref, wd_ref, x_ref, g_ref, mod_ref, o_ref, acc_ref):
    f = pl.program_id(1)

    @pl.when(f == 0)
    def _():
        acc_ref[...] = jnp.zeros_like(acc_ref)

    h = h_ref[...]
    a = _silu(_dot(h, wg_ref[...])) * _dot(h, wu_ref[...])
    acc_ref[...] += _dot(a.astype(BF16), wd_ref[...])

    @pl.when(f == pl.num_programs(1) - 1)
    def _():
        d = D_MODEL
        o_ref[...] = x_ref[...] + mod_ref[:, 5 * d:6 * d] * _rms(acc_ref[...], g_ref[...])


def _dense_ffn(h, wg, wu, wd, x2, g_post, mod_l, seq):
    t, d = h.shape
    ff = wg.shape[1]
    tm = _tile(seq, 1024)
    tf = _tile(ff, FF_TILE)
    tpb = seq // tm
    return pl.pallas_call(
        _ffn_kernel,
        grid=(t // tm, ff // tf),
        in_specs=[pl.BlockSpec((tm, d), lambda i, f: (i, 0)),
                  pl.BlockSpec((d, tf), lambda i, f: (0, f)),
                  pl.BlockSpec((d, tf), lambda i, f: (0, f)),
                  pl.BlockSpec((tf, d), lambda i, f: (f, 0)),
                  pl.BlockSpec((tm, d), lambda i, f: (i, 0)),
                  pl.BlockSpec((1, d), lambda i, f: (0, 0)),
                  pl.BlockSpec((None, 1, 6 * d), lambda i, f: (i // tpb, 0, 0))],
        out_specs=pl.BlockSpec((tm, d), lambda i, f: (i, 0)),
        out_shape=jax.ShapeDtypeStruct((t, d), F32),
        scratch_shapes=[pltpu.VMEM((tm, d), F32)],
        compiler_params=_cparams(("arbitrary", "arbitrary")),
        name="dense_swiglu",
    )(h, wg.astype(BF16), wu.astype(BF16), wd.astype(BF16), x2, g_post.reshape(1, d), mod_l)


def _route_kernel(logit_ref, tri_ref, info_ref, count_ref, carry_ref):
    @pl.when(pl.program_id(0) == 0)
    def _():
        carry_ref[...] = jnp.zeros_like(carry_ref)

    lg = logit_ref[...]
    lane = lax.broadcasted_iota(jnp.int32, lg.shape, 1)
    lanef = lane.astype(F32)
    valid = lane < N_EXPERTS
    big = float(LANES)
    lg = jnp.where(valid, lg, -jnp.inf)
    m1 = jnp.max(lg, axis=-1, keepdims=True)
    e1 = jnp.min(jnp.where(lg == m1, lanef, big), axis=-1, keepdims=True)
    lg2 = jnp.where(lanef == e1, -jnp.inf, lg)
    m2 = jnp.max(lg2, axis=-1, keepdims=True)
    e2 = jnp.min(jnp.where(lg2 == m2, lanef, big), axis=-1, keepdims=True)
    z = jnp.exp(m2 - m1)
    w1 = 1.0 / (1.0 + z)
    w2 = z / (1.0 + z)
    oh1 = (lanef == e1).astype(F32)
    oh2 = (lanef == e2).astype(F32)
    both = oh1 + oh2
    before = _dot(tri_ref[...], both.astype(BF16)) + carry_ref[0:1, :]
    r1 = jnp.sum(before * oh1, axis=-1, keepdims=True)
    r2 = jnp.sum(before * oh2, axis=-1, keepdims=True)
    carry_ref[0:1, :] = carry_ref[0:1, :] + jnp.sum(both, axis=0, keepdims=True)
    count_ref[...] = carry_ref[...]
    cols = (e1, e2, r1, r2, w1, w2)
    info = jnp.zeros(lg.shape, F32)
    for idx, col in enumerate(cols):
        info = jnp.where(lane == idx, col, info)
    info_ref[...] = info


def _route(logits):
    t = logits.shape[0]
    tm = _tile(t, 512)
    tri = jnp.asarray(np.tril(np.ones((tm, tm), np.float32), -1), BF16)
    info, counts = pl.pallas_call(
        _route_kernel,
        grid=(t // tm,),
        in_specs=[pl.BlockSpec((tm, LANES), lambda i: (i, 0)), pl.BlockSpec((tm, tm), lambda i: (0, 0))],
        out_specs=[pl.BlockSpec((tm, LANES), lambda i: (i, 0)), pl.BlockSpec((8, LANES), lambda i: (0, 0))],
        out_shape=[jax.ShapeDtypeStruct((t, LANES), F32), jax.ShapeDtypeStruct((8, LANES), F32)],
        scratch_shapes=[pltpu.VMEM((8, LANES), F32)],
        compiler_params=_cparams(("arbitrary",)),
        name="moe_route",
    )(logits, tri)
    return info, counts[0, :N_EXPERTS]


def _dispatch_kernel(d1_ref, d2_ref, h_ref, xin_ref, xbuf_ref, sem, *, tm):
    del xin_ref

    def row_copy(r, dref):
        return pltpu.make_async_copy(h_ref.at[pl.ds(r, 1), :], xbuf_ref.at[pl.ds(dref[0, 0, r], 1), :], sem)

    def start(r, carry):
        row_copy(r, d1_ref).start()
        row_copy(r, d2_ref).start()
        return carry

    def wait(r, carry):
        row_copy(r, d1_ref).wait()
        row_copy(r, d2_ref).wait()
        return carry

    lax.fori_loop(0, tm, start, 0)
    lax.fori_loop(0, tm, wait, 0)


def _dispatch(h, dest1, dest2, n_rows):
    t, d = h.shape
    tm = _tile(t, 512)
    nt = t // tm
    smem = lambda: pl.BlockSpec((1, 1, tm), lambda i: (i, 0, 0), memory_space=pltpu.SMEM)
    return pl.pallas_call(
        functools.partial(_dispatch_kernel, tm=tm),
        grid=(nt,),
        in_specs=[smem(), smem(), pl.BlockSpec((tm, d), lambda i: (i, 0)), pl.BlockSpec(memory_space=pl.ANY)],
        out_specs=pl.BlockSpec(memory_space=pl.ANY),
        out_shape=jax.ShapeDtypeStruct((n_rows, d), h.dtype),
        scratch_shapes=[pltpu.SemaphoreType.DMA(())],
        input_output_aliases={3: 0},
        compiler_params=_cparams(("arbitrary",)),
        name="moe_dispatch",
    )(dest1.reshape(nt, 1, tm), dest2.reshape(nt, 1, tm), h, jnp.zeros((n_rows, d), h.dtype))


def _moe_kernel(be_ref, nb_ref, x_ref, wg_ref, wu_ref, wd_ref, o_ref, xb_ref, acc_ref):
    i = pl.program_id(0)
    f = pl.program_id(1)
    active = i < nb_ref[0]

    @pl.when(jnp.logical_and(active, f == 0))
    def _():
        xb_ref[...] = x_ref[...].astype(BF16)
        acc_ref[...] = jnp.zeros_like(acc_ref)

    @pl.when(active)
    def _():
        xb = xb_ref[...]
        a = _silu(_dot(xb, wg_ref[...])) * _dot(xb, wu_ref[...])
        acc_ref[...] += _dot(a.astype(BF16), wd_ref[...])

    @pl.when(f == pl.num_programs(1) - 1)
    def _():
        o_ref[...] = jnp.where(active, acc_ref[...], 0.0)


def _moe_ffn(x_buf, block_expert, n_active, wg, wu, wd):
    n_rows, d = x_buf.shape
    ff = wg.shape[2]
    tm = MOE_TILE
    tf = _tile(ff, FF_TILE)
    nf = ff // tf

    def wmap(i, f, be, nb):
        return be[i], jnp.where(i < nb[0], f, nf - 1)

    grid_spec = pltpu.PrefetchScalarGridSpec(
        num_scalar_prefetch=2,
        grid=(n_rows // tm, nf),
        in_specs=[pl.BlockSpec((tm, d), lambda i, f, be, nb: (jnp.minimum(i, nb[0] - 1), 0)),
                  pl.BlockSpec((None, d, tf), lambda i, f, be, nb: (wmap(i, f, be, nb)[0], 0, wmap(i, f, be, nb)[1])),
                  pl.BlockSpec((None, d, tf), lambda i, f, be, nb: (wmap(i, f, be, nb)[0], 0, wmap(i, f, be, nb)[1])),
                  pl.BlockSpec((None, tf, d), lambda i, f, be, nb: (wmap(i, f, be, nb)[0], wmap(i, f, be, nb)[1], 0))],
        out_specs=pl.BlockSpec((tm, d), lambda i, f, be, nb: (i, 0)),
        scratch_shapes=[pltpu.VMEM((tm, d), BF16), pltpu.VMEM((tm, d), F32)],
    )
    return pl.pallas_call(
        _moe_kernel,
        grid_spec=grid_spec,
        out_shape=jax.ShapeDtypeStruct((n_rows, d), F32),
        compiler_params=_cparams(("arbitrary", "arbitrary")),
        name="moe_grouped_swiglu",
    )(block_expert, n_active, x_buf, wg.astype(BF16), wu.astype(BF16), wd.astype(BF16))


def _combine_kernel(d1_ref, d2_ref, ybuf_ref, info_ref, x_ref, g_ref, mod_ref, o_ref, y1_ref, y2_ref, sem, *, tm):
    def row_copy(r, dref, dst):
        return pltpu.make_async_copy(ybuf_ref.at[pl.ds(dref[0, 0, r], 1), :], dst.at[pl.ds(r, 1), :], sem)

    def start(r, carry):
        row_copy(r, d1_ref, y1_ref).start()
        row_copy(r, d2_ref, y2_ref).start()
        return carry

    def wait(r, carry):
        row_copy(r, d1_ref, y1_ref).wait()
        row_copy(r, d2_ref, y2_ref).wait()
        return carry

    lax.fori_loop(0, tm, start, 0)
    lax.fori_loop(0, tm, wait, 0)
    d = D_MODEL
    info = info_ref[...]
    y = y1_ref[...] * info[:, 4:5] + y2_ref[...] * info[:, 5:6]
    o_ref[...] = x_ref[...] + mod_ref[:, 5 * d:6 * d] * _rms(y, g_ref[...])


def _combine(y_buf, dest1, dest2, info, x2, g_post, mod_l, seq):
    t, d = x2.shape
    tm = _tile(seq, 512)
    nt = t // tm
    tpb = seq // tm
    smem = lambda: pl.BlockSpec((1, 1, tm), lambda i: (i, 0, 0), memory_space=pltpu.SMEM)
    return pl.pallas_call(
        functools.partial(_combine_kernel, tm=tm),
        grid=(nt,),
        in_specs=[smem(), smem(), pl.BlockSpec(memory_space=pl.ANY),
                  pl.BlockSpec((tm, LANES), lambda i: (i, 0)),
                  pl.BlockSpec((tm, d), lambda i: (i, 0)),
                  pl.BlockSpec((1, d), lambda i: (0, 0)),
                  pl.BlockSpec((None, 1, 6 * d), lambda i: (i // tpb, 0, 0))],
        out_specs=pl.BlockSpec((tm, d), lambda i: (i, 0)),
        out_shape=jax.ShapeDtypeStruct((t, d), F32),
        scratch_shapes=[pltpu.VMEM((tm, d), F32), pltpu.VMEM((tm, d), F32), pltpu.SemaphoreType.DMA(())],
        compiler_params=_cparams(("arbitrary",)),
        name="moe_combine",
    )(dest1.reshape(nt, 1, tm), dest2.reshape(nt, 1, tm), y_buf, info, x2, g_post.reshape(1, d), mod_l)


def _routed_ffn(h, logits, wg, wu, wd, x2, g_post, mod_l, seq):
    t, d = h.shape
    tm = MOE_TILE
    info, counts = _route(logits)
    counts = counts.astype(jnp.int32)
    padded = ((counts + tm - 1) // tm) * tm
    pad_end = jnp.cumsum(padded)
    pad_start = pad_end - padded
    e1 = info[:, 0].astype(jnp.int32)
    e2 = info[:, 1].astype(jnp.int32)
    dest1 = pad_start[e1] + info[:, 2].astype(jnp.int32)
    dest2 = pad_start[e2] + info[:, 3].astype(jnp.int32)
    n_rows = t * TOP_K + N_EXPERTS * tm
    n_blocks = n_rows // tm
    block_start = jnp.arange(n_blocks, dtype=jnp.int32) * tm
    block_expert = jnp.clip(jnp.searchsorted(pad_end, block_start, side='right'), 0, N_EXPERTS - 1).astype(jnp.int32)
    n_active = (pad_end[-1] // tm).astype(jnp.int32).reshape(1)
    x_buf = _dispatch(h, dest1, dest2, n_rows)
    y_buf = _moe_ffn(x_buf, block_expert, n_active, wg, wu, wd)
    return _combine(y_buf, dest1, dest2, info, x2, g_post, mod_l, seq)


def kernel(x, c, positions, ada_w, ada_b, norm_pre_mix, norm_post_mix, norm_pre_ffn, norm_post_ffn, w_in, ret_norm, ssm_a_re, ssm_a_im, ssm_b_re, ssm_b_im, ssm_c_re, ssm_c_im, ssm_d, ssm_log_dt, ssm_glu_w, ssm_glu_b, ssm_norm, mla_q_norm, mla_w_uq, mla_kv_norm, mla_w_ukv, mla_norm, w_out, ffn_w_gate, ffn_w_up, ffn_w_down, moe_router, moe_w_gate, moe_w_up, moe_w_down):
    bsz, seq, d = x.shape
    depth = ada_w.shape[0]
    assert d == D_MODEL and seq % SSM_CHUNK == 0 and seq % RET_CHUNK == 0
    t = bsz * seq
    x2 = x.reshape(t, d)
    mod = _modulation(c, ada_w, ada_b)
    cos128, sin128 = _rope_tables(positions)
    ssm_params = _ssm_params(ssm_a_re, ssm_a_im, ssm_b_re, ssm_b_im, ssm_c_re, ssm_c_im, ssm_log_dt)
    for layer in range(depth):
        mod_l = mod[layer]
        r, u, cq, ckv, kr = _inproj(x2, norm_pre_mix[layer], mod_l, w_in[layer], seq)
        ret = _retention(r, cos128, sin128, ret_norm[layer], bsz, seq)
        ssm = _ssm_post(_ssm_scan(u, ssm_params, layer, seq), u, ssm_d[layer], ssm_glu_w[layer], ssm_glu_b[layer],
                        ssm_norm[layer])
        q3, k3, v3 = _mla_prep(cq, ckv, kr, cos128, sin128, mla_q_norm[layer], mla_w_uq[layer],
                               mla_kv_norm[layer], mla_w_ukv[layer])
        att = _attention(q3, k3, v3, bsz, seq)
        j = layer // 2
        if layer % 2 == 0:
            x2, h = _outproj(x2, ret, ssm, att, mla_norm[layer], w_out[layer], norm_post_mix[layer], mod_l,
                             norm_pre_ffn[layer], seq)
            x2 = _dense_ffn(h, ffn_w_gate[j], ffn_w_up[j], ffn_w_down[j], x2, norm_post_ffn[layer], mod_l, seq)
        else:
            x2, h, logits = _outproj(x2, ret, ssm, att, mla_norm[layer], w_out[layer], norm_post_mix[layer], mod_l,
                                     norm_pre_ffn[layer], seq, router=moe_router[j])
            x2 = _routed_ffn(h, logits, moe_w_gate[j], moe_w_up[j], moe_w_down[j], x2, norm_post_ffn[layer],
                             mod_l, seq)
    return x2.reshape(bsz, seq, d)
```

```python
import functools
import math

import numpy as np
import jax
import jax.numpy as jnp
from jax import lax
from jax.experimental import pallas as pl
from jax.experimental.pallas import tpu as pltpu

D_MODEL = 1024
RET_HEADS = 4
RET_HEAD_DIM = 64
RET_WIDTH = RET_HEADS * RET_HEAD_DIM
RET_CHUNK = 128
SSM_GROUP_CH = 16
SSM_GROUPS = 16
SSM_WIDTH = SSM_GROUPS * SSM_GROUP_CH
SSM_STATE = 64
MLA_HEADS = 4
MLA_NOPE = 128
MLA_ROPE = 64
MLA_V = 128
MLA_QK = MLA_NOPE + MLA_ROPE
MLA_WIDTH = MLA_HEADS * MLA_V
MLA_Q_RANK = 256
MLA_KV_RANK = 128
ROPE_DIM = 64
ROPE_BASE = 10000.0
D_FF = 3584
N_EXPERTS = 8
TOP_K = 2
EPS = 1e-6
NEG_INF = -1e30
IN_COLS = 4 * RET_WIDTH + SSM_WIDTH + MLA_Q_RANK + MLA_KV_RANK + MLA_ROPE

LANES = 128
VMEM_LIMIT_BYTES = 56 * 1024 * 1024

SSM_CHUNK = 64
MOE_TILE = 512
FF_TILE = 512
DMA_UNROLL = 8

BF16 = jnp.bfloat16
F32 = jnp.float32


def _cparams(sem):
    return pltpu.CompilerParams(dimension_semantics=sem, vmem_limit_bytes=VMEM_LIMIT_BYTES)


def _tile(n, pref):
    t = min(n, pref)
    while n % t:
        t //= 2
    return t


def _dot(a, b):
    return jnp.dot(a, b, preferred_element_type=F32)


def _dot3(a, b):
    a_hi = a.astype(BF16)
    a_lo = (a - a_hi.astype(F32)).astype(BF16)
    b_hi = b.astype(BF16)
    b_lo = (b - b_hi.astype(F32)).astype(BF16)
    return _dot(a_hi, b_hi) + (_dot(a_hi, b_lo) + _dot(a_lo, b_hi))


def _rms(x, g):
    return x * lax.rsqrt(jnp.mean(x * x, axis=-1, keepdims=True) + EPS) * g


def _silu(x):
    return x * (1.0 / (1.0 + jnp.exp(-x)))


def _rope128(x, cos, sin_signed):
    lane = lax.broadcasted_iota(jnp.int32, x.shape, 1)
    first_half = (lane % ROPE_DIM) < (ROPE_DIM // 2)
    partner = jnp.where(first_half, pltpu.roll(x, LANES - ROPE_DIM // 2, 1), pltpu.roll(x, ROPE_DIM // 2, 1))
    return x * cos + partner * sin_signed


def _mod_kernel(c_ref, w_ref, b_ref, o_ref):
    cond = _silu(c_ref[...])
    o_ref[...] = _dot3(cond, w_ref[...]) + b_ref[...]


def _modulation(c, ada_w, ada_b):
    depth, d, n = ada_w.shape
    bsz = c.shape[0]
    rows = 8
    c_pad = jnp.zeros((rows, d), F32).at[:bsz].set(c)
    tn = _tile(n, 1536)
    out = pl.pallas_call(
        _mod_kernel,
        grid=(depth, n // tn),
        in_specs=[pl.BlockSpec((rows, d), lambda l, j: (0, 0)),
                  pl.BlockSpec((None, d, tn), lambda l, j: (l, 0, j)),
                  pl.BlockSpec((None, 1, tn), lambda l, j: (l, 0, j))],
        out_specs=pl.BlockSpec((None, rows, tn), lambda l, j: (l, 0, j)),
        out_shape=jax.ShapeDtypeStruct((depth, rows, n), F32),
        compiler_params=_cparams(("arbitrary", "arbitrary")),
        name="adaln_mod",
    )(c_pad, ada_w, ada_b.reshape(depth, 1, n))
    return out[:, :bsz].reshape(depth, bsz, 1, n)


def _rope_kernel(pos_ref, inv_ref, cos_ref, sin_ref):
    pos = pos_ref[...].astype(F32)
    lane = lax.broadcasted_iota(jnp.int32, cos_ref.shape, 1)
    half = ROPE_DIM // 2
    p = jnp.where(lane < half, pos[:, 0:1],
                  jnp.where(lane < 2 * half, pos[:, 1:2],
                            jnp.where(lane < 3 * half, pos[:, 2:3], pos[:, 3:4])))
    ang = p * inv_ref[...]
    cos_ref[...] = jnp.cos(ang)
    sin_ref[...] = jnp.sin(ang)


def _rope_tables(positions):
    t = positions.size
    half = ROPE_DIM // 2
    per_row = LANES // half
    inv = ROPE_BASE ** (-jnp.arange(0, ROPE_DIM, 2, dtype=F32) / ROPE_DIM)
    inv_row = jnp.tile(inv, per_row).reshape(1, LANES)
    rows = t // per_row
    tr = _tile(rows, 1024)
    cos, sin = pl.pallas_call(
        _rope_kernel,
        grid=(rows // tr,),
        in_specs=[pl.BlockSpec((tr, per_row), lambda i: (i, 0)),
                  pl.BlockSpec((1, LANES), lambda i: (0, 0))],
        out_specs=[pl.BlockSpec((tr, LANES), lambda i: (i, 0))] * 2,
        out_shape=[jax.ShapeDtypeStruct((rows, LANES), F32)] * 2,
        compiler_params=_cparams(("arbitrary",)),
        name="rope_tables",
    )(positions.reshape(rows, per_row), inv_row)
    cos = cos.reshape(t, half)
    sin = sin.reshape(t, half)
    return jnp.tile(cos, (1, per_row)), jnp.concatenate([-sin, sin, -sin, sin], axis=1)


def _inproj_kernel(x_ref, g_ref, mod_ref, w_ref, ret_ref, u_ref, cq_ref, ckv_ref, kr_ref):
    d = D_MODEL
    x = x_ref[...]
    h = _rms(x, g_ref[...]) * (1.0 + mod_ref[:, d:2 * d]) + mod_ref[:, 0:d]
    p = _dot(h.astype(BF16), w_ref[...])
    o = 4 * RET_WIDTH
    ret_ref[...] = p[:, :o].astype(BF16)
    u_ref[...] = p[:, o:o + SSM_WIDTH].astype(BF16)
    o += SSM_WIDTH
    cq_ref[...] = p[:, o:o + MLA_Q_RANK].astype(BF16)
    o += MLA_Q_RANK
    ckv_ref[...] = p[:, o:o + MLA_KV_RANK].astype(BF16)
    o += MLA_KV_RANK
    kr_ref[...] = p[:, o:o + MLA_ROPE].astype(BF16)


def _inproj(x2, g, mod_l, w_in, seq):
    t, d = x2.shape
    tm = _tile(seq, 512)
    tpb = seq // tm
    widths = (4 * RET_WIDTH, SSM_WIDTH, MLA_Q_RANK, MLA_KV_RANK, MLA_ROPE)
    return pl.pallas_call(
        _inproj_kernel,
        grid=(t // tm,),
        in_specs=[pl.BlockSpec((tm, d), lambda i: (i, 0)),
                  pl.BlockSpec((1, d), lambda i: (0, 0)),
                  pl.BlockSpec((None, 1, 6 * d), lambda i: (i // tpb, 0, 0)),
                  pl.BlockSpec((d, IN_COLS), lambda i: (0, 0))],
        out_specs=[pl.BlockSpec((tm, w), lambda i: (i, 0)) for w in widths],
        out_shape=[jax.ShapeDtypeStruct((t, w), BF16) for w in widths],
        compiler_params=_cparams(("arbitrary",)),
        name="mixer_inproj",
    )(x2, g.reshape(1, d), mod_l, w_in.astype(BF16))


def _retention_kernel(r_ref, cos_ref, sin_ref, dmask_ref, qdec_ref, kdec_ref, cdec_ref, bd_ref, gn_ref,
                      o_ref, state_ref, *, chunks):
    w = RET_WIDTH

    @pl.when(pl.program_id(1) == 0)
    def _():
        state_ref[...] = jnp.zeros_like(state_ref)

    lane = lax.broadcasted_iota(jnp.int32, (RET_CHUNK, w), 1)
    bd = bd_ref[...]
    for c in range(chunks):
        rows = pl.ds(c * RET_CHUNK, RET_CHUNK)
        cos = cos_ref[rows, :]
        sin = sin_ref[rows, :]

        def rope(v):
            return jnp.concatenate([_rope128(v[:, :LANES], cos, sin), _rope128(v[:, LANES:], cos, sin)], axis=1)

        q = rope(r_ref[rows, 0:w].astype(F32))
        k = rope(r_ref[rows, w:2 * w].astype(F32)) * (RET_HEAD_DIM ** -0.5)
        v = r_ref[rows, 2 * w:3 * w]
        gate = r_ref[rows, 3 * w:4 * w].astype(F32)
        kb = k.astype(BF16)
        state = state_ref[...]
        y = _dot((q * qdec_ref[...]).astype(BF16), state.astype(BF16))
        for h in range(RET_HEADS):
            in_head = (lane // RET_HEAD_DIM) == h
            qh = jnp.where(in_head, q, 0.0).astype(BF16)
            s = lax.dot_general(qh, kb, (((1,), (1,)), ((), ())), preferred_element_type=F32)
            s = s * dmask_ref[h]
            y = y + jnp.where(in_head, _dot(s.astype(BF16), v), 0.0)
        kv = lax.dot_general((k * kdec_ref[...]).astype(BF16), v, (((0,), (0,)), ((), ())),
                             preferred_element_type=F32)
        state_ref[...] = cdec_ref[...] * state + bd * kv
        y2 = y * y
        y2_hi = y2.astype(BF16)
        y2_lo = (y2 - y2_hi.astype(F32)).astype(BF16)
        bdb = bd.astype(BF16)
        ms = (_dot(y2_hi, bdb) + _dot(y2_lo, bdb)) * (1.0 / RET_HEAD_DIM)
        yn = y * lax.rsqrt(ms + EPS) * gn_ref[...]
        o_ref[rows, :] = (_silu(gate) * yn).astype(BF16)


def _retention_consts():
    c = RET_CHUNK
    log_g = np.log1p(-(2.0 ** (-5.0 - np.arange(RET_HEADS, dtype=np.float32)))).astype(np.float32)
    i = np.arange(c, dtype=np.float32)
    diff = i[:, None] - i[None, :]
    dmask = np.where(diff >= 0, np.exp(log_g[:, None, None] * np.maximum(diff, 0.0)), 0.0).astype(np.float32)
    k_dec = np.exp(log_g[None, :] * (c - 1 - i)[:, None]).astype(np.float32)
    q_dec = np.exp(log_g[None, :] * (i + 1.0)[:, None]).astype(np.float32)
    c_dec = np.exp(log_g * c).astype(np.float32)
    rep = lambda a: np.repeat(a, RET_HEAD_DIM, axis=-1)
    head = np.arange(RET_WIDTH) // RET_HEAD_DIM
    bd = (head[:, None] == head[None, :]).astype(np.float32)
    return dmask, rep(q_dec), rep(k_dec), rep(c_dec[None, :]), bd


def _retention(r, cos128, sin128, ret_norm, bsz, seq):
    t = r.shape[0]
    w = RET_WIDTH
    tr = _tile(seq, 512)
    chunks = tr // RET_CHUNK
    nt = seq // tr
    dmask, qdec, kdec, cdec, bd = (jnp.asarray(a) for a in _retention_consts())
    const = lambda shape: pl.BlockSpec(shape, lambda b, i: (0,) * len(shape))
    return pl.pallas_call(
        functools.partial(_retention_kernel, chunks=chunks),
        grid=(bsz, nt),
        in_specs=[pl.BlockSpec((tr, 4 * w), lambda b, i: (b * nt + i, 0)),
                  pl.BlockSpec((tr, LANES), lambda b, i: (b * nt + i, 0)),
                  pl.BlockSpec((tr, LANES), lambda b, i: (b * nt + i, 0)),
                  const((RET_HEADS, RET_CHUNK, RET_CHUNK)),
                  const((RET_CHUNK, w)), const((RET_CHUNK, w)), const((1, w)), const((w, w)), const((1, w))],
        out_specs=pl.BlockSpec((tr, w), lambda b, i: (b * nt + i, 0)),
        out_shape=jax.ShapeDtypeStruct((t, w), BF16),
        scratch_shapes=[pltpu.VMEM((w, w), F32)],
        compiler_params=_cparams(("arbitrary", "arbitrary")),
        name="retention",
    )(r, cos128, sin128, dmask, qdec, kdec, cdec, bd, ret_norm.reshape(1, w))


def _expand(x, sel):
    hi = x.astype(BF16)
    lo = (x - hi.astype(F32)).astype(BF16)
    return _dot(hi, sel) + _dot(lo, sel)


def _ssm_kernel(u_ref, acol_ref, arow_ref, ldt_ref, b_ref, bt_ref, ct_ref, esel_ref, ech_ref, y_ref, toep_ref,
                *, chunks_per_seq, steps):
    tc, hg, p = SSM_CHUNK, SSM_GROUP_CH, SSM_STATE
    k = tc * hg
    dt = jnp.exp(ldt_ref[...])
    ar, ai = acol_ref[:, 0:1], acol_ref[:, 1:2]
    n = lax.broadcasted_iota(jnp.int32, (p, LANES), 1).astype(F32)
    mag = jnp.exp(ar * dt * n)
    pw_re, pw_im = mag * jnp.cos(ai * dt * n), mag * jnp.sin(ai * dt * n)
    def zoh(a_r, a_i):
        m1 = jnp.exp(a_r * dt)
        l_r, l_i = m1 * jnp.cos(a_i * dt), m1 * jnp.sin(a_i * dt)
        den = a_r * a_r + a_i * a_i
        return ((l_r - 1.0) * a_r + l_i * a_i) / den, (l_i * a_r - (l_r - 1.0) * a_i) / den

    f_re, f_im = zoh(ar, ai)
    bb_re = f_re * b_ref[0] - f_im * b_ref[1]
    bb_im = f_re * b_ref[1] + f_im * b_ref[0]
    fr_re, fr_im = zoh(arow_ref[0:1, 0:p], arow_ref[1:2, 0:p])
    bbt_re = fr_re * bt_ref[0] - fr_im * bt_ref[1]
    bbt_im = fr_re * bt_ref[1] + fr_im * bt_ref[0]
    ech = ech_ref[...]
    pt_re, pt_im = _expand(pw_re, esel_ref[0]), _expand(pw_im, esel_ref[0])
    p1_re, p1_im = _expand(pw_re, esel_ref[1]), _expand(pw_im, esel_ref[1])
    pr_re, pr_im = _expand(pw_re, esel_ref[2]), _expand(pw_im, esel_ref[2])
    bbr, bbi = _expand(bb_re, ech), _expand(bb_im, ech)
    cr, ci = _expand(ct_ref[0], ech), _expand(ct_ref[1], ech)
    z_re = cr * pt_re - ci * pt_im
    z_im = cr * pt_im + ci * pt_re
    strip = _dot3(bbt_re, z_re) - _dot3(bbt_im, z_im)
    w_in_t = jnp.concatenate([pr_re * bbr - pr_im * bbi, pr_re * bbi + pr_im * bbr], axis=0).astype(BF16)
    v_out = jnp.concatenate([cr * p1_re - ci * p1_im, -(cr * p1_im + ci * p1_re)], axis=0).astype(BF16)
    padded = jnp.concatenate([jnp.zeros((hg, k), F32), strip], axis=1)
    for s in range(tc):
        toep_ref[s * hg:(s + 1) * hg, :] = padded[:, k - s * hg:2 * k - s * hg].astype(BF16)
    dist = (tc * (1 << lax.broadcasted_iota(jnp.int32, (steps, 2 * p), 0))).astype(F32)
    lane = lax.broadcasted_iota(jnp.int32, (steps, 2 * p), 1)
    amag = jnp.exp(arow_ref[0:1, :] * dt * dist)
    a_cos = amag * jnp.cos(arow_ref[1:2, :] * dt * dist)
    a_sin = jnp.where(lane < p, -1.0, 1.0) * amag * jnp.sin(arow_ref[1:2, :] * dt * dist)

    u = u_ref[...]
    x = lax.dot_general(u, w_in_t, (((1,), (1,)), ((), ())), preferred_element_type=F32)
    row = lax.broadcasted_iota(jnp.int32, x.shape, 0) % chunks_per_seq

    def shift_rows(a, d):
        return jnp.where(row >= d, pltpu.roll(a, d, 0), 0.0)

    for j in range(steps):
        prev = shift_rows(x, 1 << j)
        x = x + prev * a_cos[j:j + 1, :] + pltpu.roll(prev, p, 1) * a_sin[j:j + 1, :]
    x_in = shift_rows(x, 1)
    y_ref[...] = _dot(u, toep_ref[...]) + _dot(x_in.astype(BF16), v_out)


def _ssm_params(a_re, a_im, b_re, b_im, c_re, c_im, log_dt):
    a_col = jnp.stack([a_re, a_im], axis=-1)
    a_row = jnp.stack([jnp.concatenate([a_re, a_re], -1), jnp.concatenate([a_im, a_im], -1)], axis=2)
    b = jnp.stack([b_re, b_im], axis=2)
    bt = jnp.swapaxes(b, -1, -2)
    ct = jnp.swapaxes(jnp.stack([c_re, c_im], axis=2), -1, -2)
    return a_col, a_row, log_dt[..., None, None], b, bt, ct


def _ssm_selectors():
    tc, hg = SSM_CHUNK, SSM_GROUP_CH
    t_of_lane = np.arange(tc * hg) // hg
    n = np.arange(LANES)[:, None]
    esel = np.stack([n == t_of_lane, n == t_of_lane + 1, n == tc - 1 - t_of_lane]).astype(np.float32)
    ech = (np.arange(hg)[:, None] == (np.arange(tc * hg) % hg)[None, :]).astype(np.float32)
    return jnp.asarray(esel, BF16), jnp.asarray(ech, BF16)


def _ssm_scan(u, params, layer, seq):
    t = u.shape[0]
    g, hg, tc, p = SSM_GROUPS, SSM_GROUP_CH, SSM_CHUNK, SSM_STATE
    nc = t // tc
    cps = seq // tc
    steps = max(1, int(math.ceil(math.log2(cps))))
    k = tc * hg
    esel, ech = _ssm_selectors()
    ug = u.reshape(nc, tc, g, hg).transpose(2, 0, 1, 3).reshape(g, nc, k)
    per_group = lambda *blk: pl.BlockSpec((None, None) + blk, lambda i: (layer, i) + (0,) * len(blk))
    y = pl.pallas_call(
        functools.partial(_ssm_kernel, chunks_per_seq=cps, steps=steps),
        grid=(g,),
        in_specs=[pl.BlockSpec((None, nc, k), lambda i: (i, 0, 0)),
                  per_group(p, 2), per_group(2, 2 * p), per_group(1, 1),
                  per_group(2, p, hg), per_group(2, hg, p), per_group(2, p, hg),
                  pl.BlockSpec((3, LANES, k), lambda i: (0, 0, 0)),
                  pl.BlockSpec((hg, k), lambda i: (0, 0))],
        out_specs=pl.BlockSpec((None, nc, k), lambda i: (i, 0, 0)),
        out_shape=jax.ShapeDtypeStruct((g, nc, k), F32),
        scratch_shapes=[pltpu.VMEM((k, k), BF16)],
        compiler_params=_cparams(("arbitrary",)),
        name="s5_chunk_scan",
    )(ug, *params, esel, ech)
    return y.reshape(g, nc, tc, hg).transpose(1, 2, 0, 3).reshape(t, g * hg)


def _ssm_post_kernel(y_ref, u_ref, d_ref, gw_ref, gb_ref, gn_ref, o_ref):
    y = y_ref[...] + d_ref[...] * u_ref[...].astype(F32)
    y = 0.5 * y * (1.0 + jnp.tanh(math.sqrt(2.0 / math.pi) * (y + 0.044715 * (y * y * y))))
    z = _dot(y.astype(BF16), gw_ref[...]) + gb_ref[...]
    y = y * (1.0 / (1.0 + jnp.exp(-z)))
    o_ref[...] = _rms(y, gn_ref[...]).astype(BF16)


def _ssm_post(y, u, d, glu_w, glu_b, ssm_norm):
    t, w = y.shape
    tm = _tile(t, 1024)
    row = lambda: pl.BlockSpec((1, w), lambda i: (0, 0))
    return pl.pallas_call(
        _ssm_post_kernel,
        grid=(t // tm,),
        in_specs=[pl.BlockSpec((tm, w), lambda i: (i, 0)), pl.BlockSpec((tm, w), lambda i: (i, 0)),
                  row(), pl.BlockSpec((w, w), lambda i: (0, 0)), row(), row()],
        out_specs=pl.BlockSpec((tm, w), lambda i: (i, 0)),
        out_shape=jax.ShapeDtypeStruct((t, w), BF16),
        compiler_params=_cparams(("arbitrary",)),
        name="s5_gelu_glu_norm",
    )(y, u, d.reshape(1, w), glu_w.astype(BF16), glu_b.reshape(1, w), ssm_norm.reshape(1, w))


def _mla_prep_kernel(cq_ref, ckv_ref, kr_ref, cos_ref, sin_ref, qn_ref, kvn_ref, wq_ref, wkv_ref,
                     q_ref, k_ref, v_ref):
    cos = cos_ref[...]
    sin = sin_ref[...]
    nope_w = MLA_HEADS * MLA_NOPE
    q = _dot(_rms(cq_ref[...].astype(F32), qn_ref[...]).astype(BF16), wq_ref[...])
    q = q * (MLA_QK ** -0.5 * math.log2(math.e))
    q_rope = jnp.concatenate([_rope128(q[:, nope_w:nope_w + LANES], cos, sin),
                              _rope128(q[:, nope_w + LANES:], cos, sin)], axis=1)
    kv = _dot(_rms(ckv_ref[...].astype(F32), kvn_ref[...]).astype(BF16), wkv_ref[...])
    kr = kr_ref[...].astype(F32)
    k_pe = _rope128(jnp.concatenate([kr, kr], axis=1), cos, sin)[:, :MLA_ROPE]
    for h in range(MLA_HEADS):
        q_ref[h] = jnp.concatenate([q[:, h * MLA_NOPE:(h + 1) * MLA_NOPE],
                                    q_rope[:, h * MLA_ROPE:(h + 1) * MLA_ROPE]], axis=1).astype(BF16)
        k_ref[h] = jnp.concatenate([kv[:, h * MLA_NOPE:(h + 1) * MLA_NOPE], k_pe], axis=1).astype(BF16)
        v_ref[h, :, 0:MLA_V] = kv[:, nope_w + h * MLA_V:nope_w + (h + 1) * MLA_V].astype(BF16)
        v_ref[h, :, MLA_V:] = jnp.ones((kv.shape[0], LANES), BF16)


def _mla_prep(cq, ckv, kr, cos128, sin128, q_norm, w_uq, kv_norm, w_ukv):
    t = cq.shape[0]
    tm = _tile(t, 512)
    hh = MLA_HEADS
    wq = w_uq.reshape(MLA_Q_RANK, hh, MLA_QK)
    wq = jnp.concatenate([wq[:, :, :MLA_NOPE].reshape(MLA_Q_RANK, -1), wq[:, :, MLA_NOPE:].reshape(MLA_Q_RANK, -1)], 1)
    wkv = w_ukv.reshape(MLA_KV_RANK, hh, MLA_NOPE + MLA_V)
    wkv = jnp.concatenate([wkv[:, :, :MLA_NOPE].reshape(MLA_KV_RANK, -1), wkv[:, :, MLA_NOPE:].reshape(MLA_KV_RANK, -1)], 1)
    tok = lambda w: pl.BlockSpec((tm, w), lambda i: (i, 0))
    full = lambda a, b: pl.BlockSpec((a, b), lambda i: (0, 0))
    return pl.pallas_call(
        _mla_prep_kernel,
        grid=(t // tm,),
        in_specs=[tok(MLA_Q_RANK), tok(MLA_KV_RANK), tok(MLA_ROPE), tok(LANES), tok(LANES),
                  full(1, MLA_Q_RANK), full(1, MLA_KV_RANK),
                  full(MLA_Q_RANK, hh * MLA_QK), full(MLA_KV_RANK, hh * (MLA_NOPE + MLA_V))],
        out_specs=[pl.BlockSpec((hh, tm, MLA_QK), lambda i: (0, i, 0)),
                   pl.BlockSpec((hh, tm, MLA_QK), lambda i: (0, i, 0)),
                   pl.BlockSpec((hh, tm, MLA_V + LANES), lambda i: (0, i, 0))],
        out_shape=[jax.ShapeDtypeStruct((hh, t, MLA_QK), BF16), jax.ShapeDtypeStruct((hh, t, MLA_QK), BF16),
                   jax.ShapeDtypeStruct((hh, t, MLA_V + LANES), BF16)],
        compiler_params=_cparams(("arbitrary",)),
        name="mla_prep",
    )(cq, ckv, kr, cos128, sin128, q_norm.reshape(1, -1), kv_norm.reshape(1, -1), wq.astype(BF16), wkv.astype(BF16))


def _attn_kernel(q_ref, k_ref, v_ref, o_ref, m_ref, l_ref, acc_ref, sa_ref, sb_ref, *, tq, th, tk):
    qi = pl.program_id(2)
    m_ref[...] = jnp.full_like(m_ref, NEG_INF)
    l_ref[...] = jnp.zeros_like(l_ref)
    acc_ref[...] = jnp.zeros_like(acc_ref)

    halves = (0, 1)
    assert tq == 2 * th and tk == th

    def scores(s_ref, j, which=halves):
        k = k_ref[pl.ds(pl.multiple_of(j * tk, tk), tk), :]
        for half in which:
            rows = pl.ds(half * th, th)
            s_ref[rows, :] = lax.dot_general(q_ref[rows, :], k, (((1,), (1,)), ((), ())),
                                             preferred_element_type=F32)

    def consume(s_ref, j, diag_half=None):
        v = v_ref[pl.ds(pl.multiple_of(j * tk, tk), tk), :]
        for half in halves:
            if diag_half is not None and half < diag_half:
                continue
            rows = pl.ds(half * th, th)
            s = s_ref[rows, :]
            if half == diag_half:
                r = lax.broadcasted_iota(jnp.int32, s.shape, 0)
                c = lax.broadcasted_iota(jnp.int32, s.shape, 1)
                s = jnp.where(c <= r, s, NEG_INF)
            m_prev = m_ref[rows, :]
            m_new = jnp.maximum(m_prev, jnp.max(s, axis=-1, keepdims=True))
            alpha = jnp.exp2(m_prev - m_new)
            p = jnp.exp2(s - jnp.tile(m_new, (1, tk // LANES)))
            pv = _dot(p.astype(BF16), v)
            acc_ref[rows, :] = alpha * acc_ref[rows, :] + pv[:, :MLA_V]
            l_ref[rows, :] = alpha * l_ref[rows, :] + pv[:, MLA_V:]
            m_ref[rows, :] = m_new

    scores(sa_ref, 0)

    def pair(i, carry):
        scores(sb_ref, 2 * i + 1)
        consume(sa_ref, 2 * i)
        scores(sa_ref, 2 * i + 2)
        consume(sb_ref, 2 * i + 1)
        return carry

    lax.fori_loop(0, qi, pair, 0)
    scores(sb_ref, 2 * qi + 1, which=(1,))
    consume(sa_ref, 2 * qi, diag_half=0)
    consume(sb_ref, 2 * qi + 1, diag_half=1)
    o_ref[...] = (acc_ref[...] / l_ref[...]).astype(o_ref.dtype)


def _attention(q3, k3, v3, bsz, seq):
    hh, t, _ = q3.shape
    tq = _tile(seq, 1024)
    th = tq // 2
    tk = th
    nq = seq // tq
    vw = v3.shape[-1]
    return pl.pallas_call(
        functools.partial(_attn_kernel, tq=tq, th=th, tk=tk),
        grid=(bsz, hh, nq),
        in_specs=[pl.BlockSpec((None, tq, MLA_QK), lambda b, h, i: (h, b * nq + i, 0)),
                  pl.BlockSpec((None, seq, MLA_QK), lambda b, h, i: (h, b, 0)),
                  pl.BlockSpec((None, seq, vw), lambda b, h, i: (h, b, 0))],
        out_specs=pl.BlockSpec((tq, MLA_V), lambda b, h, i: (b * nq + i, h)),
        out_shape=jax.ShapeDtypeStruct((t, hh * MLA_V), BF16),
        scratch_shapes=[pltpu.VMEM((tq, LANES), F32), pltpu.VMEM((tq, LANES), F32), pltpu.VMEM((tq, MLA_V), F32),
                        pltpu.VMEM((tq, tk), F32), pltpu.VMEM((tq, tk), F32)],
        compiler_params=_cparams(("arbitrary", "arbitrary", "arbitrary")),
        name="mla_flash_attention",
    )(q3, k3, v3)


def _outproj_kernel(*refs, routed):
    if routed:
        (x_ref, ret_ref, ssm_ref, att_ref, an_ref, w_ref, gpost_ref, mod_ref, gpre_ref, router_ref,
         xo_ref, h_ref, logit_ref) = refs
    else:
        x_ref, ret_ref, ssm_ref, att_ref, an_ref, w_ref, gpost_ref, mod_ref, gpre_ref, xo_ref, h_ref = refs
    d = D_MODEL
    att = _rms(att_ref[...].astype(F32), an_ref[...]).astype(BF16)
    o1 = RET_WIDTH
    o2 = RET_WIDTH + SSM_WIDTH
    y = _dot(ret_ref[...], w_ref[0:o1, :]) + _dot(ssm_ref[...], w_ref[o1:o2, :]) + _dot(att, w_ref[o2:, :])
    x = x_ref[...] + mod_ref[:, 2 * d:3 * d] * _rms(y, gpost_ref[...])
    xo_ref[...] = x
    h = _rms(x, gpre_ref[...]) * (1.0 + mod_ref[:, 4 * d:5 * d]) + mod_ref[:, 3 * d:4 * d]
    h_ref[...] = h.astype(h_ref.dtype)
    if routed:
        logit_ref[...] = _dot3(h, router_ref[...])


def _outproj(x2, ret, ssm, att, mla_norm, w_out, g_post, mod_l, g_pre, seq, router=None):
    t, d = x2.shape
    tm = _tile(seq, 512)
    tpb = seq // tm
    routed = router is not None
    tok = lambda w: pl.BlockSpec((tm, w), lambda i: (i, 0))
    full = lambda a, b: pl.BlockSpec((a, b), lambda i: (0, 0))
    in_specs = [tok(d), tok(RET_WIDTH), tok(SSM_WIDTH), tok(MLA_WIDTH), full(1, MLA_WIDTH), full(d, d), full(1, d),
                pl.BlockSpec((None, 1, 6 * d), lambda i: (i // tpb, 0, 0)), full(1, d)]
    args = [x2, ret, ssm, att, mla_norm.reshape(1, -1), w_out.astype(BF16), g_post.reshape(1, d), mod_l,
            g_pre.reshape(1, d)]
    out_specs = [tok(d), tok(d)]
    out_shape = [jax.ShapeDtypeStruct((t, d), F32), jax.ShapeDtypeStruct((t, d), F32 if routed else BF16)]
    if routed:
        in_specs.append(full(d, LANES))
        args.append(jnp.zeros((d, LANES), F32).at[:, :N_EXPERTS].set(router))
        out_specs.append(tok(LANES))
        out_shape.append(jax.ShapeDtypeStruct((t, LANES), F32))
    return pl.pallas_call(
        functools.partial(_outproj_kernel, routed=routed),
        grid=(t // tm,),
        in_specs=in_specs, out_specs=out_specs, out_shape=out_shape,
        compiler_params=_cparams(("arbitrary",)),
        name="mixer_outproj_routed" if routed else "mixer_outproj",
    )(*args)


def _ffn_kernel(h_ref, wg_ref, wu_ref, wd_ref, x_ref, g_ref, mod_ref, o_ref, acc_ref):
    f = pl.program_id(1)

    @pl.when(f == 0)
    def _():
        acc_ref[...] = jnp.zeros_like(acc_ref)

    h = h_ref[...]
    a = _silu(_dot(h, wg_ref[...])) * _dot(h, wu_ref[...])
    acc_ref[...] += _dot(a.astype(BF16), wd_ref[...])

    @pl.when(f == pl.num_programs(1) - 1)
    def _():
        d = D_MODEL
        o_ref[...] = x_ref[...] + mod_ref[:, 5 * d:6 * d] * _rms(acc_ref[...], g_ref[...])


def _dense_ffn(h, wg, wu, wd, x2, g_post, mod_l, seq):
    t, d = h.shape
    ff = wg.shape[1]
    tm = _tile(seq, 1024)
    tf = _tile(ff, FF_TILE)
    tpb = seq // tm
    return pl.pallas_call(
        _ffn_kernel,
        grid=(t // tm, ff // tf),
        in_specs=[pl.BlockSpec((tm, d), lambda i, f: (i, 0)),
                  pl.BlockSpec((d, tf), lambda i, f: (0, f)),
                  pl.BlockSpec((d, tf), lambda i, f: (0, f)),
                  pl.BlockSpec((tf, d), lambda i, f: (f, 0)),
                  pl.BlockSpec((tm, d), lambda i, f: (i, 0)),
                  pl.BlockSpec((1, d), lambda i, f: (0, 0)),
                  pl.BlockSpec((None, 1, 6 * d), lambda i, f: (i // tpb, 0, 0))],
        out_specs=pl.BlockSpec((tm, d), lambda i, f: (i, 0)),
        out_shape=jax.ShapeDtypeStruct((t, d), F32),
        scratch_shapes=[pltpu.VMEM((tm, d), F32)],
        compiler_params=_cparams(("arbitrary", "arbitrary")),
        name="dense_swiglu",
    )(h, wg.astype(BF16), wu.astype(BF16), wd.astype(BF16), x2, g_post.reshape(1, d), mod_l)


def _route_kernel(logit_ref, tri_ref, info_ref, count_ref, carry_ref):
    @pl.when(pl.program_id(0) == 0)
    def _():
        carry_ref[...] = jnp.zeros_like(carry_ref)

    lg = logit_ref[...]
    lane = lax.broadcasted_iota(jnp.int32, lg.shape, 1)
    lanef = lane.astype(F32)
    valid = lane < N_EXPERTS
    big = float(LANES)
    lg = jnp.where(valid, lg, -jnp.inf)
    m1 = jnp.max(lg, axis=-1, keepdims=True)
    e1 = jnp.min(jnp.where(lg == m1, lanef, big), axis=-1, keepdims=True)
    lg2 = jnp.where(lanef == e1, -jnp.inf, lg)
    m2 = jnp.max(lg2, axis=-1, keepdims=True)
    e2 = jnp.min(jnp.where(lg2 == m2, lanef, big), axis=-1, keepdims=True)
    z = jnp.exp(m2 - m1)
    w1 = 1.0 / (1.0 + z)
    w2 = z / (1.0 + z)
    oh1 = (lanef == e1).astype(F32)
    oh2 = (lanef == e2).astype(F32)
    both = oh1 + oh2
    before = _dot(tri_ref[...], both.astype(BF16)) + carry_ref[0:1, :]
    r1 = jnp.sum(before * oh1, axis=-1, keepdims=True)
    r2 = jnp.sum(before * oh2, axis=-1, keepdims=True)
    carry_ref[0:1, :] = carry_ref[0:1, :] + jnp.sum(both, axis=0, keepdims=True)
    count_ref[...] = carry_ref[...]
    cols = (e1, e2, r1, r2, w1, w2)
    info = jnp.zeros(lg.shape, F32)
    for idx, col in enumerate(cols):
        info = jnp.where(lane == idx, col, info)
    info_ref[...] = info


def _route(logits):
    t = logits.shape[0]
    tm = _tile(t, 512)
    tri = jnp.asarray(np.tril(np.ones((tm, tm), np.float32), -1), BF16)
    info, counts = pl.pallas_call(
        _route_kernel,
        grid=(t // tm,),
        in_specs=[pl.BlockSpec((tm, LANES), lambda i: (i, 0)), pl.BlockSpec((tm, tm), lambda i: (0, 0))],
        out_specs=[pl.BlockSpec((tm, LANES), lambda i: (i, 0)), pl.BlockSpec((8, LANES), lambda i: (0, 0))],
        out_shape=[jax.ShapeDtypeStruct((t, LANES), F32), jax.ShapeDtypeStruct((8, LANES), F32)],
        scratch_shapes=[pltpu.VMEM((8, LANES), F32)],
        compiler_params=_cparams(("arbitrary",)),
        name="moe_route",
    )(logits, tri)
    return info, counts[0, :N_EXPERTS]


def _dispatch_kernel(d1_ref, d2_ref, h_ref, xin_ref, xbuf_ref, sem, *, tm):
    del xin_ref

    def row_copy(r, dref):
        return pltpu.make_async_copy(h_ref.at[pl.ds(r, 1), :], xbuf_ref.at[pl.ds(dref[0, 0, r], 1), :], sem)

    def start(g, carry):
        for j in range(DMA_UNROLL):
            r = g * DMA_UNROLL + j
            row_copy(r, d1_ref).start(priority=0)
            row_copy(r, d2_ref).start(priority=1)
        return carry

    lax.fori_loop(0, tm // DMA_UNROLL, start, 0)
    for _ in range(TOP_K):
        pltpu.make_async_copy(h_ref, xbuf_ref.at[pl.ds(0, tm), :], sem).wait()


def _dispatch(h, dest1, dest2, n_rows):
    t, d = h.shape
    tm = _tile(t, 512)
    nt = t // tm
    smem = lambda: pl.BlockSpec((1, 1, tm), lambda i: (i, 0, 0), memory_space=pltpu.SMEM)
    return pl.pallas_call(
        functools.partial(_dispatch_kernel, tm=tm),
        grid=(nt,),
        in_specs=[smem(), smem(), pl.BlockSpec((tm, d), lambda i: (i, 0)), pl.BlockSpec(memory_space=pl.ANY)],
        out_specs=pl.BlockSpec(memory_space=pl.ANY),
        out_shape=jax.ShapeDtypeStruct((n_rows, d), h.dtype),
        scratch_shapes=[pltpu.SemaphoreType.DMA(())],
        input_output_aliases={3: 0},
        compiler_params=_cparams(("arbitrary",)),
        name="moe_dispatch",
    )(dest1.reshape(nt, 1, tm), dest2.reshape(nt, 1, tm), h, jnp.zeros((n_rows, d), h.dtype))


def _moe_kernel(be_ref, nb_ref, x_ref, wg_ref, wu_ref, wd_ref, o_ref, xb_ref, acc_ref):
    i = pl.program_id(0)
    f = pl.program_id(1)
    active = i < nb_ref[0]

    @pl.when(jnp.logical_and(active, f == 0))
    def _():
        xb_ref[...] = x_ref[...].astype(BF16)
        acc_ref[...] = jnp.zeros_like(acc_ref)

    @pl.when(active)
    def _():
        xb = xb_ref[...]
        a = _silu(_dot(xb, wg_ref[...])) * _dot(xb, wu_ref[...])
        acc_ref[...] += _dot(a.astype(BF16), wd_ref[...])

    @pl.when(f == pl.num_programs(1) - 1)
    def _():
        o_ref[...] = jnp.where(active, acc_ref[...], 0.0)


def _moe_ffn(x_buf, block_expert, n_active, wg, wu, wd):
    n_rows, d = x_buf.shape
    ff = wg.shape[2]
    tm = MOE_TILE
    tf = _tile(ff, FF_TILE)
    nf = ff // tf

    def wmap(i, f, be, nb):
        return be[i], jnp.where(i < nb[0], f, nf - 1)

    grid_spec = pltpu.PrefetchScalarGridSpec(
        num_scalar_prefetch=2,
        grid=(n_rows // tm, nf),
        in_specs=[pl.BlockSpec((tm, d), lambda i, f, be, nb: (jnp.minimum(i, nb[0] - 1), 0)),
                  pl.BlockSpec((None, d, tf), lambda i, f, be, nb: (wmap(i, f, be, nb)[0], 0, wmap(i, f, be, nb)[1])),
                  pl.BlockSpec((None, d, tf), lambda i, f, be, nb: (wmap(i, f, be, nb)[0], 0, wmap(i, f, be, nb)[1])),
                  pl.BlockSpec((None, tf, d), lambda i, f, be, nb: (wmap(i, f, be, nb)[0], wmap(i, f, be, nb)[1], 0))],
        out_specs=pl.BlockSpec((tm, d), lambda i, f, be, nb: (i, 0)),
        scratch_shapes=[pltpu.VMEM((tm, d), BF16), pltpu.VMEM((tm, d), F32)],
    )
    return pl.pallas_call(
        _moe_kernel,
        grid_spec=grid_spec,
        out_shape=jax.ShapeDtypeStruct((n_rows, d), F32),
        compiler_params=_cparams(("arbitrary", "arbitrary")),
        name="moe_grouped_swiglu",
    )(block_expert, n_active, x_buf, wg.astype(BF16), wu.astype(BF16), wd.astype(BF16))


def _combine_kernel(d1_ref, d2_ref, ybuf_ref, info_ref, x_ref, g_ref, mod_ref, o_ref, y1_ref, y2_ref, sem, *, tm):
    def row_copy(r, dref, dst):
        return pltpu.make_async_copy(ybuf_ref.at[pl.ds(dref[0, 0, r], 1), :], dst.at[pl.ds(r, 1), :], sem)

    def start(g, carry):
        for j in range(DMA_UNROLL):
            r = g * DMA_UNROLL + j
            row_copy(r, d1_ref, y1_ref).start(priority=0)
            row_copy(r, d2_ref, y2_ref).start(priority=1)
        return carry

    lax.fori_loop(0, tm // DMA_UNROLL, start, 0)
    for dst in (y1_ref, y2_ref):
        pltpu.make_async_copy(ybuf_ref.at[pl.ds(0, tm), :], dst, sem).wait()
    d = D_MODEL
    info = info_ref[...]
    y = y1_ref[...] * info[:, 4:5] + y2_ref[...] * info[:, 5:6]
    o_ref[...] = x_ref[...] + mod_ref[:, 5 * d:6 * d] * _rms(y, g_ref[...])


def _combine(y_buf, dest1, dest2, info, x2, g_post, mod_l, seq):
    t, d = x2.shape
    tm = _tile(seq, 512)
    nt = t // tm
    tpb = seq // tm
    smem = lambda: pl.BlockSpec((1, 1, tm), lambda i: (i, 0, 0), memory_space=pltpu.SMEM)
    return pl.pallas_call(
        functools.partial(_combine_kernel, tm=tm),
        grid=(nt,),
        in_specs=[smem(), smem(), pl.BlockSpec(memory_space=pl.ANY),
                  pl.BlockSpec((tm, LANES), lambda i: (i, 0)),
                  pl.BlockSpec((tm, d), lambda i: (i, 0)),
                  pl.BlockSpec((1, d), lambda i: (0, 0)),
                  pl.BlockSpec((None, 1, 6 * d), lambda i: (i // tpb, 0, 0))],
        out_specs=pl.BlockSpec((tm, d), lambda i: (i, 0)),
        out_shape=jax.ShapeDtypeStruct((t, d), F32),
        scratch_shapes=[pltpu.VMEM((tm, d), F32), pltpu.VMEM((tm, d), F32), pltpu.SemaphoreType.DMA(())],
        compiler_params=_cparams(("arbitrary",)),
        name="moe_combine",
    )(dest1.reshape(nt, 1, tm), dest2.reshape(nt, 1, tm), y_buf, info, x2, g_post.reshape(1, d), mod_l)


def _routed_ffn(h, logits, wg, wu, wd, x2, g_post, mod_l, seq):
    t, d = h.shape
    tm = MOE_TILE
    info, counts = _route(logits)
    counts = counts.astype(jnp.int32)
    padded = ((counts + tm - 1) // tm) * tm
    pad_end = jnp.cumsum(padded)
    pad_start = pad_end - padded
    e1 = info[:, 0].astype(jnp.int32)
    e2 = info[:, 1].astype(jnp.int32)
    dest1 = pad_start[e1] + info[:, 2].astype(jnp.int32)
    dest2 = pad_start[e2] + info[:, 3].astype(jnp.int32)
    n_rows = t * TOP_K + N_EXPERTS * tm
    n_blocks = n_rows // tm
    block_start = jnp.arange(n_blocks, dtype=jnp.int32) * tm
    block_expert = jnp.minimum(jnp.sum(pad_end[None, :] <= block_start[:, None], axis=1), N_EXPERTS - 1).astype(jnp.int32)
    n_active = (pad_end[-1] // tm).astype(jnp.int32).reshape(1)
    x_buf = _dispatch(h, dest1, dest2, n_rows)
    y_buf = _moe_ffn(x_buf, block_expert, n_active, wg, wu, wd)
    return _combine(y_buf, dest1, dest2, info, x2, g_post, mod_l, seq)


def kernel(x, c, positions, ada_w, ada_b, norm_pre_mix, norm_post_mix, norm_pre_ffn, norm_post_ffn, w_in, ret_norm, ssm_a_re, ssm_a_im, ssm_b_re, ssm_b_im, ssm_c_re, ssm_c_im, ssm_d, ssm_log_dt, ssm_glu_w, ssm_glu_b, ssm_norm, mla_q_norm, mla_w_uq, mla_kv_norm, mla_w_ukv, mla_norm, w_out, ffn_w_gate, ffn_w_up, ffn_w_down, moe_router, moe_w_gate, moe_w_up, moe_w_down):
    bsz, seq, d = x.shape
    depth = ada_w.shape[0]
    assert d == D_MODEL and seq % SSM_CHUNK == 0 and seq % RET_CHUNK == 0
    t = bsz * seq
    x2 = x.reshape(t, d)
    mod = _modulation(c, ada_w, ada_b)
    cos128, sin128 = _rope_tables(positions)
    ssm_params = _ssm_params(ssm_a_re, ssm_a_im, ssm_b_re, ssm_b_im, ssm_c_re, ssm_c_im, ssm_log_dt)
    for layer in range(depth):
        mod_l = mod[layer]
        r, u, cq, ckv, kr = _inproj(x2, norm_pre_mix[layer], mod_l, w_in[layer], seq)
        ret = _retention(r, cos128, sin128, ret_norm[layer], bsz, seq)
        ssm = _ssm_post(_ssm_scan(u, ssm_params, layer, seq), u, ssm_d[layer], ssm_glu_w[layer], ssm_glu_b[layer],
                        ssm_norm[layer])
        q3, k3, v3 = _mla_prep(cq, ckv, kr, cos128, sin128, mla_q_norm[layer], mla_w_uq[layer],
                               mla_kv_norm[layer], mla_w_ukv[layer])
        att = _attention(q3, k3, v3, bsz, seq)
        j = layer // 2
        if layer % 2 == 0:
            x2, h = _outproj(x2, ret, ssm, att, mla_norm[layer], w_out[layer], norm_post_mix[layer], mod_l,
                             norm_pre_ffn[layer], seq)
            x2 = _dense_ffn(h, ffn_w_gate[j], ffn_w_up[j], ffn_w_down[j], x2, norm_post_ffn[layer], mod_l, seq)
        else:
            x2, h, logits = _outproj(x2, ret, ssm, att, mla_norm[layer], w_out[layer], norm_post_mix[layer], mod_l,
                                     norm_pre_ffn[layer], seq, router=moe_router[j])
            x2 = _routed_ffn(h, logits, moe_w_gate[j], moe_w_up[j], moe_w_down[j], x2, norm_post_ffn[layer],
                             mod_l, seq)
    return x2.reshape(bsz, seq, d)
```

```python
import functools
import math

import numpy as np
import jax
import jax.numpy as jnp
from jax import lax
from jax.experimental import pallas as pl
from jax.experimental.pallas import tpu as pltpu

D_MODEL = 1024
RET_HEADS = 4
RET_HEAD_DIM = 64
RET_WIDTH = RET_HEADS * RET_HEAD_DIM
RET_CHUNK = 128
SSM_GROUP_CH = 16
SSM_GROUPS = 16
SSM_WIDTH = SSM_GROUPS * SSM_GROUP_CH
SSM_STATE = 64
MLA_HEADS = 4
MLA_NOPE = 128
MLA_ROPE = 64
MLA_V = 128
MLA_QK = MLA_NOPE + MLA_ROPE
MLA_WIDTH = MLA_HEADS * MLA_V
MLA_Q_RANK = 256
MLA_KV_RANK = 128
ROPE_DIM = 64
ROPE_BASE = 10000.0
D_FF = 3584
N_EXPERTS = 8
TOP_K = 2
EPS = 1e-6
NEG_INF = -1e30
IN_COLS = 4 * RET_WIDTH + SSM_WIDTH + MLA_Q_RANK + MLA_KV_RANK + MLA_ROPE

LANES = 128
VMEM_LIMIT_BYTES = 56 * 1024 * 1024

SSM_CHUNK = 64
MOE_TILE = 1024
FF_TILE = 512
DMA_UNROLL = 8

BF16 = jnp.bfloat16
F32 = jnp.float32


def _cparams(sem):
    return pltpu.CompilerParams(dimension_semantics=sem, vmem_limit_bytes=VMEM_LIMIT_BYTES)


def _tile(n, pref):
    t = min(n, pref)
    while n % t:
        t //= 2
    return t


def _dot(a, b):
    return jnp.dot(a, b, preferred_element_type=F32)


def _dot3(a, b):
    a_hi = a.astype(BF16)
    a_lo = (a - a_hi.astype(F32)).astype(BF16)
    b_hi = b.astype(BF16)
    b_lo = (b - b_hi.astype(F32)).astype(BF16)
    return _dot(a_hi, b_hi) + (_dot(a_hi, b_lo) + _dot(a_lo, b_hi))


def _rms(x, g):
    return x * lax.rsqrt(jnp.mean(x * x, axis=-1, keepdims=True) + EPS) * g


def _silu(x):
    return x * (1.0 / (1.0 + jnp.exp(-x)))


def _rope128(x, cos, sin_signed):
    lane = lax.broadcasted_iota(jnp.int32, x.shape, 1)
    first_half = (lane % ROPE_DIM) < (ROPE_DIM // 2)
    partner = jnp.where(first_half, pltpu.roll(x, LANES - ROPE_DIM // 2, 1), pltpu.roll(x, ROPE_DIM // 2, 1))
    return x * cos + partner * sin_signed


def _mod_kernel(c_ref, w_ref, b_ref, o_ref):
    cond = _silu(c_ref[...])
    o_ref[...] = _dot3(cond, w_ref[...]) + b_ref[...]


def _modulation(c, ada_w, ada_b):
    depth, d, n = ada_w.shape
    bsz = c.shape[0]
    rows = 8
    c_pad = jnp.zeros((rows, d), F32).at[:bsz].set(c)
    tn = _tile(n, 1536)
    out = pl.pallas_call(
        _mod_kernel,
        grid=(depth, n // tn),
        in_specs=[pl.BlockSpec((rows, d), lambda l, j: (0, 0)),
                  pl.BlockSpec((None, d, tn), lambda l, j: (l, 0, j)),
                  pl.BlockSpec((None, 1, tn), lambda l, j: (l, 0, j))],
        out_specs=pl.BlockSpec((None, rows, tn), lambda l, j: (l, 0, j)),
        out_shape=jax.ShapeDtypeStruct((depth, rows, n), F32),
        compiler_params=_cparams(("arbitrary", "arbitrary")),
        name="adaln_mod",
    )(c_pad, ada_w, ada_b.reshape(depth, 1, n))
    return out[:, :bsz].reshape(depth, bsz, 1, n)


def _rope_kernel(pos_ref, inv_ref, cos_ref, sin_ref):
    pos = pos_ref[...].astype(F32)
    lane = lax.broadcasted_iota(jnp.int32, cos_ref.shape, 1)
    half = ROPE_DIM // 2
    p = jnp.where(lane < half, pos[:, 0:1],
                  jnp.where(lane < 2 * half, pos[:, 1:2],
                            jnp.where(lane < 3 * half, pos[:, 2:3], pos[:, 3:4])))
    ang = p * inv_ref[...]
    cos_ref[...] = jnp.cos(ang)
    sin_ref[...] = jnp.sin(ang)


def _rope_tables(positions):
    t = positions.size
    half = ROPE_DIM // 2
    per_row = LANES // half
    inv = ROPE_BASE ** (-jnp.arange(0, ROPE_DIM, 2, dtype=F32) / ROPE_DIM)
    inv_row = jnp.tile(inv, per_row).reshape(1, LANES)
    rows = t // per_row
    tr = _tile(rows, 1024)
    cos, sin = pl.pallas_call(
        _rope_kernel,
        grid=(rows // tr,),
        in_specs=[pl.BlockSpec((tr, per_row), lambda i: (i, 0)),
                  pl.BlockSpec((1, LANES), lambda i: (0, 0))],
        out_specs=[pl.BlockSpec((tr, LANES), lambda i: (i, 0))] * 2,
        out_shape=[jax.ShapeDtypeStruct((rows, LANES), F32)] * 2,
        compiler_params=_cparams(("arbitrary",)),
        name="rope_tables",
    )(positions.reshape(rows, per_row), inv_row)
    cos = cos.reshape(t, half)
    sin = sin.reshape(t, half)
    return jnp.tile(cos, (1, per_row)), jnp.concatenate([-sin, sin, -sin, sin], axis=1)


def _inproj_kernel(x_ref, g_ref, mod_ref, w_ref, ret_ref, u_ref, cq_ref, ckv_ref, kr_ref):
    d = D_MODEL
    x = x_ref[...]
    h = _rms(x, g_ref[...]) * (1.0 + mod_ref[:, d:2 * d]) + mod_ref[:, 0:d]
    p = _dot(h.astype(BF16), w_ref[...])
    o = 4 * RET_WIDTH
    ret_ref[...] = p[:, :o].astype(BF16)
    u_ref[...] = p[:, o:o + SSM_WIDTH].astype(BF16)
    o += SSM_WIDTH
    cq_ref[...] = p[:, o:o + MLA_Q_RANK].astype(BF16)
    o += MLA_Q_RANK
    ckv_ref[...] = p[:, o:o + MLA_KV_RANK].astype(BF16)
    o += MLA_KV_RANK
    kr_ref[...] = p[:, o:o + MLA_ROPE].astype(BF16)


def _inproj(x2, g, mod_l, w_in, seq):
    t, d = x2.shape
    tm = _tile(seq, 512)
    tpb = seq // tm
    widths = (4 * RET_WIDTH, SSM_WIDTH, MLA_Q_RANK, MLA_KV_RANK, MLA_ROPE)
    return pl.pallas_call(
        _inproj_kernel,
        grid=(t // tm,),
        in_specs=[pl.BlockSpec((tm, d), lambda i: (i, 0)),
                  pl.BlockSpec((1, d), lambda i: (0, 0)),
                  pl.BlockSpec((None, 1, 6 * d), lambda i: (i // tpb, 0, 0)),
                  pl.BlockSpec((d, IN_COLS), lambda i: (0, 0))],
        out_specs=[pl.BlockSpec((tm, w), lambda i: (i, 0)) for w in widths],
        out_shape=[jax.ShapeDtypeStruct((t, w), BF16) for w in widths],
        compiler_params=_cparams(("arbitrary",)),
        name="mixer_inproj",
    )(x2, g.reshape(1, d), mod_l, w_in.astype(BF16))


def _retention_kernel(r_ref, cos_ref, sin_ref, dmask_ref, qdec_ref, kdec_ref, cdec_ref, bd_ref, gn_ref,
                      o_ref, state_ref, *, chunks):
    w = RET_WIDTH

    @pl.when(pl.program_id(1) == 0)
    def _():
        state_ref[...] = jnp.zeros_like(state_ref)

    lane = lax.broadcasted_iota(jnp.int32, (RET_CHUNK, w), 1)
    bd = bd_ref[...]
    for c in range(chunks):
        rows = pl.ds(c * RET_CHUNK, RET_CHUNK)
        cos = cos_ref[rows, :]
        sin = sin_ref[rows, :]

        def rope(v):
            return jnp.concatenate([_rope128(v[:, :LANES], cos, sin), _rope128(v[:, LANES:], cos, sin)], axis=1)

        q = rope(r_ref[rows, 0:w].astype(F32))
        k = rope(r_ref[rows, w:2 * w].astype(F32)) * (RET_HEAD_DIM ** -0.5)
        v = r_ref[rows, 2 * w:3 * w]
        gate = r_ref[rows, 3 * w:4 * w].astype(F32)
        kb = k.astype(BF16)
        state = state_ref[...]
        y = _dot((q * qdec_ref[...]).astype(BF16), state.astype(BF16))
        for h in range(RET_HEADS):
            in_head = (lane // RET_HEAD_DIM) == h
            qh = jnp.where(in_head, q, 0.0).astype(BF16)
            s = lax.dot_general(qh, kb, (((1,), (1,)), ((), ())), preferred_element_type=F32)
            s = s * dmask_ref[h]
            y = y + jnp.where(in_head, _dot(s.astype(BF16), v), 0.0)
        kv = lax.dot_general((k * kdec_ref[...]).astype(BF16), v, (((0,), (0,)), ((), ())),
                             preferred_element_type=F32)
        state_ref[...] = cdec_ref[...] * state + bd * kv
        y2 = y * y
        y2_hi = y2.astype(BF16)
        y2_lo = (y2 - y2_hi.astype(F32)).astype(BF16)
        bdb = bd.astype(BF16)
        ms = (_dot(y2_hi, bdb) + _dot(y2_lo, bdb)) * (1.0 / RET_HEAD_DIM)
        yn = y * lax.rsqrt(ms + EPS) * gn_ref[...]
        o_ref[rows, :] = (_silu(gate) * yn).astype(BF16)


def _retention_consts():
    c = RET_CHUNK
    log_g = np.log1p(-(2.0 ** (-5.0 - np.arange(RET_HEADS, dtype=np.float32)))).astype(np.float32)
    i = np.arange(c, dtype=np.float32)
    diff = i[:, None] - i[None, :]
    dmask = np.where(diff >= 0, np.exp(log_g[:, None, None] * np.maximum(diff, 0.0)), 0.0).astype(np.float32)
    k_dec = np.exp(log_g[None, :] * (c - 1 - i)[:, None]).astype(np.float32)
    q_dec = np.exp(log_g[None, :] * (i + 1.0)[:, None]).astype(np.float32)
    c_dec = np.exp(log_g * c).astype(np.float32)
    rep = lambda a: np.repeat(a, RET_HEAD_DIM, axis=-1)
    head = np.arange(RET_WIDTH) // RET_HEAD_DIM
    bd = (head[:, None] == head[None, :]).astype(np.float32)
    return dmask, rep(q_dec), rep(k_dec), rep(c_dec[None, :]), bd


def _retention(r, cos128, sin128, ret_norm, bsz, seq):
    t = r.shape[0]
    w = RET_WIDTH
    tr = _tile(seq, 512)
    chunks = tr // RET_CHUNK
    nt = seq // tr
    dmask, qdec, kdec, cdec, bd = (jnp.asarray(a) for a in _retention_consts())
    const = lambda shape: pl.BlockSpec(shape, lambda b, i: (0,) * len(shape))
    return pl.pallas_call(
        functools.partial(_retention_kernel, chunks=chunks),
        grid=(bsz, nt),
        in_specs=[pl.BlockSpec((tr, 4 * w), lambda b, i: (b * nt + i, 0)),
                  pl.BlockSpec((tr, LANES), lambda b, i: (b * nt + i, 0)),
                  pl.BlockSpec((tr, LANES), lambda b, i: (b * nt + i, 0)),
                  const((RET_HEADS, RET_CHUNK, RET_CHUNK)),
                  const((RET_CHUNK, w)), const((RET_CHUNK, w)), const((1, w)), const((w, w)), const((1, w))],
        out_specs=pl.BlockSpec((tr, w), lambda b, i: (b * nt + i, 0)),
        out_shape=jax.ShapeDtypeStruct((t, w), BF16),
        scratch_shapes=[pltpu.VMEM((w, w), F32)],
        compiler_params=_cparams(("arbitrary", "arbitrary")),
        name="retention",
    )(r, cos128, sin128, dmask, qdec, kdec, cdec, bd, ret_norm.reshape(1, w))


def _expand(x, sel):
    hi = x.astype(BF16)
    lo = (x - hi.astype(F32)).astype(BF16)
    return _dot(hi, sel) + _dot(lo, sel)


def _ssm_kernel(u_ref, acol_ref, arow_ref, ldt_ref, b_ref, bt_ref, ct_ref, esel_ref, ech_ref, y_ref, toep_ref,
                *, chunks_per_seq, steps):
    tc, hg, p = SSM_CHUNK, SSM_GROUP_CH, SSM_STATE
    k = tc * hg
    dt = jnp.exp(ldt_ref[...])
    ar, ai = acol_ref[:, 0:1], acol_ref[:, 1:2]
    n = lax.broadcasted_iota(jnp.int32, (p, LANES), 1).astype(F32)
    mag = jnp.exp(ar * dt * n)
    pw_re, pw_im = mag * jnp.cos(ai * dt * n), mag * jnp.sin(ai * dt * n)
    def zoh(a_r, a_i):
        m1 = jnp.exp(a_r * dt)
        l_r, l_i = m1 * jnp.cos(a_i * dt), m1 * jnp.sin(a_i * dt)
        den = a_r * a_r + a_i * a_i
        return ((l_r - 1.0) * a_r + l_i * a_i) / den, (l_i * a_r - (l_r - 1.0) * a_i) / den

    f_re, f_im = zoh(ar, ai)
    bb_re = f_re * b_ref[0] - f_im * b_ref[1]
    bb_im = f_re * b_ref[1] + f_im * b_ref[0]
    fr_re, fr_im = zoh(arow_ref[0:1, 0:p], arow_ref[1:2, 0:p])
    bbt_re = fr_re * bt_ref[0] - fr_im * bt_ref[1]
    bbt_im = fr_re * bt_ref[1] + fr_im * bt_ref[0]
    ech = ech_ref[...]
    pt_re, pt_im = _expand(pw_re, esel_ref[0]), _expand(pw_im, esel_ref[0])
    p1_re, p1_im = _expand(pw_re, esel_ref[1]), _expand(pw_im, esel_ref[1])
    pr_re, pr_im = _expand(pw_re, esel_ref[2]), _expand(pw_im, esel_ref[2])
    bbr, bbi = _expand(bb_re, ech), _expand(bb_im, ech)
    cr, ci = _expand(ct_ref[0], ech), _expand(ct_ref[1], ech)
    z_re = cr * pt_re - ci * pt_im
    z_im = cr * pt_im + ci * pt_re
    strip = _dot3(bbt_re, z_re) - _dot3(bbt_im, z_im)
    w_in_t = jnp.concatenate([pr_re * bbr - pr_im * bbi, pr_re * bbi + pr_im * bbr], axis=0).astype(BF16)
    v_out = jnp.concatenate([cr * p1_re - ci * p1_im, -(cr * p1_im + ci * p1_re)], axis=0).astype(BF16)
    padded = jnp.concatenate([jnp.zeros((hg, k), F32), strip], axis=1)
    for s in range(tc):
        toep_ref[s * hg:(s + 1) * hg, :] = padded[:, k - s * hg:2 * k - s * hg].astype(BF16)
    dist = (tc * (1 << lax.broadcasted_iota(jnp.int32, (steps, 2 * p), 0))).astype(F32)
    lane = lax.broadcasted_iota(jnp.int32, (steps, 2 * p), 1)
    amag = jnp.exp(arow_ref[0:1, :] * dt * dist)
    a_cos = amag * jnp.cos(arow_ref[1:2, :] * dt * dist)
    a_sin = jnp.where(lane < p, -1.0, 1.0) * amag * jnp.sin(arow_ref[1:2, :] * dt * dist)

    u = u_ref[...]
    x = lax.dot_general(u, w_in_t, (((1,), (1,)), ((), ())), preferred_element_type=F32)
    row = lax.broadcasted_iota(jnp.int32, x.shape, 0) % chunks_per_seq

    def shift_rows(a, d):
        return jnp.where(row >= d, pltpu.roll(a, d, 0), 0.0)

    for j in range(steps):
        prev = shift_rows(x, 1 << j)
        x = x + prev * a_cos[j:j + 1, :] + pltpu.roll(prev, p, 1) * a_sin[j:j + 1, :]
    x_in = shift_rows(x, 1)
    y_ref[...] = _dot(u, toep_ref[...]) + _dot(x_in.astype(BF16), v_out)


def _ssm_params(a_re, a_im, b_re, b_im, c_re, c_im, log_dt):
    a_col = jnp.stack([a_re, a_im], axis=-1)
    a_row = jnp.stack([jnp.concatenate([a_re, a_re], -1), jnp.concatenate([a_im, a_im], -1)], axis=2)
    b = jnp.stack([b_re, b_im], axis=2)
    bt = jnp.swapaxes(b, -1, -2)
    ct = jnp.swapaxes(jnp.stack([c_re, c_im], axis=2), -1, -2)
    return a_col, a_row, log_dt[..., None, None], b, bt, ct


def _ssm_selectors():
    tc, hg = SSM_CHUNK, SSM_GROUP_CH
    t_of_lane = np.arange(tc * hg) // hg
    n = np.arange(LANES)[:, None]
    esel = np.stack([n == t_of_lane, n == t_of_lane + 1, n == tc - 1 - t_of_lane]).astype(np.float32)
    ech = (np.arange(hg)[:, None] == (np.arange(tc * hg) % hg)[None, :]).astype(np.float32)
    return jnp.asarray(esel, BF16), jnp.asarray(ech, BF16)


def _ssm_scan(u, params, layer, seq):
    t = u.shape[0]
    g, hg, tc, p = SSM_GROUPS, SSM_GROUP_CH, SSM_CHUNK, SSM_STATE
    nc = t // tc
    cps = seq // tc
    steps = max(1, int(math.ceil(math.log2(cps))))
    k = tc * hg
    esel, ech = _ssm_selectors()
    ug = u.reshape(nc, tc, g, hg).transpose(2, 0, 1, 3).reshape(g, nc, k)
    per_group = lambda *blk: pl.BlockSpec((None, None) + blk, lambda i: (layer, i) + (0,) * len(blk))
    y = pl.pallas_call(
        functools.partial(_ssm_kernel, chunks_per_seq=cps, steps=steps),
        grid=(g,),
        in_specs=[pl.BlockSpec((None, nc, k), lambda i: (i, 0, 0)),
                  per_group(p, 2), per_group(2, 2 * p), per_group(1, 1),
                  per_group(2, p, hg), per_group(2, hg, p), per_group(2, p, hg),
                  pl.BlockSpec((3, LANES, k), lambda i: (0, 0, 0)),
                  pl.BlockSpec((hg, k), lambda i: (0, 0))],
        out_specs=pl.BlockSpec((None, nc, k), lambda i: (i, 0, 0)),
        out_shape=jax.ShapeDtypeStruct((g, nc, k), F32),
        scratch_shapes=[pltpu.VMEM((k, k), BF16)],
        compiler_params=_cparams(("arbitrary",)),
        name="s5_chunk_scan",
    )(ug, *params, esel, ech)
    return y.reshape(g, nc, tc, hg).transpose(1, 2, 0, 3).reshape(t, g * hg)


def _ssm_post_kernel(y_ref, u_ref, d_ref, gw_ref, gb_ref, gn_ref, o_ref):
    y = y_ref[...] + d_ref[...] * u_ref[...].astype(F32)
    y = 0.5 * y * (1.0 + jnp.tanh(math.sqrt(2.0 / math.pi) * (y + 0.044715 * (y * y * y))))
    z = _dot(y.astype(BF16), gw_ref[...]) + gb_ref[...]
    y = y * (1.0 / (1.0 + jnp.exp(-z)))
    o_ref[...] = _rms(y, gn_ref[...]).astype(BF16)


def _ssm_post(y, u, d, glu_w, glu_b, ssm_norm):
    t, w = y.shape
    tm = _tile(t, 1024)
    row = lambda: pl.BlockSpec((1, w), lambda i: (0, 0))
    return pl.pallas_call(
        _ssm_post_kernel,
        grid=(t // tm,),
        in_specs=[pl.BlockSpec((tm, w), lambda i: (i, 0)), pl.BlockSpec((tm, w), lambda i: (i, 0)),
                  row(), pl.BlockSpec((w, w), lambda i: (0, 0)), row(), row()],
        out_specs=pl.BlockSpec((tm, w), lambda i: (i, 0)),
        out_shape=jax.ShapeDtypeStruct((t, w), BF16),
        compiler_params=_cparams(("arbitrary",)),
        name="s5_gelu_glu_norm",
    )(y, u, d.reshape(1, w), glu_w.astype(BF16), glu_b.reshape(1, w), ssm_norm.reshape(1, w))


def _mla_prep_kernel(cq_ref, ckv_ref, kr_ref, cos_ref, sin_ref, qn_ref, kvn_ref, wq_ref, wkv_ref,
                     q_ref, k_ref, v_ref):
    cos = cos_ref[...]
    sin = sin_ref[...]
    nope_w = MLA_HEADS * MLA_NOPE
    q = _dot(_rms(cq_ref[...].astype(F32), qn_ref[...]).astype(BF16), wq_ref[...])
    q = q * (MLA_QK ** -0.5 * math.log2(math.e))
    q_rope = jnp.concatenate([_rope128(q[:, nope_w:nope_w + LANES], cos, sin),
                              _rope128(q[:, nope_w + LANES:], cos, sin)], axis=1)
    kv = _dot(_rms(ckv_ref[...].astype(F32), kvn_ref[...]).astype(BF16), wkv_ref[...])
    kr = kr_ref[...].astype(F32)
    k_pe = _rope128(jnp.concatenate([kr, kr], axis=1), cos, sin)[:, :MLA_ROPE]
    for h in range(MLA_HEADS):
        q_ref[h] = jnp.concatenate([q[:, h * MLA_NOPE:(h + 1) * MLA_NOPE],
                                    q_rope[:, h * MLA_ROPE:(h + 1) * MLA_ROPE]], axis=1).astype(BF16)
        k_ref[h] = jnp.concatenate([kv[:, h * MLA_NOPE:(h + 1) * MLA_NOPE], k_pe], axis=1).astype(BF16)
        v_ref[h, :, 0:MLA_V] = kv[:, nope_w + h * MLA_V:nope_w + (h + 1) * MLA_V].astype(BF16)
        v_ref[h, :, MLA_V:] = jnp.ones((kv.shape[0], LANES), BF16)


def _mla_prep(cq, ckv, kr, cos128, sin128, q_norm, w_uq, kv_norm, w_ukv):
    t = cq.shape[0]
    tm = _tile(t, 512)
    hh = MLA_HEADS
    wq = w_uq.reshape(MLA_Q_RANK, hh, MLA_QK)
    wq = jnp.concatenate([wq[:, :, :MLA_NOPE].reshape(MLA_Q_RANK, -1), wq[:, :, MLA_NOPE:].reshape(MLA_Q_RANK, -1)], 1)
    wkv = w_ukv.reshape(MLA_KV_RANK, hh, MLA_NOPE + MLA_V)
    wkv = jnp.concatenate([wkv[:, :, :MLA_NOPE].reshape(MLA_KV_RANK, -1), wkv[:, :, MLA_NOPE:].reshape(MLA_KV_RANK, -1)], 1)
    tok = lambda w: pl.BlockSpec((tm, w), lambda i: (i, 0))
    full = lambda a, b: pl.BlockSpec((a, b), lambda i: (0, 0))
    return pl.pallas_call(
        _mla_prep_kernel,
        grid=(t // tm,),
        in_specs=[tok(MLA_Q_RANK), tok(MLA_KV_RANK), tok(MLA_ROPE), tok(LANES), tok(LANES),
                  full(1, MLA_Q_RANK), full(1, MLA_KV_RANK),
                  full(MLA_Q_RANK, hh * MLA_QK), full(MLA_KV_RANK, hh * (MLA_NOPE + MLA_V))],
        out_specs=[pl.BlockSpec((hh, tm, MLA_QK), lambda i: (0, i, 0)),
                   pl.BlockSpec((hh, tm, MLA_QK), lambda i: (0, i, 0)),
                   pl.BlockSpec((hh, tm, MLA_V + LANES), lambda i: (0, i, 0))],
        out_shape=[jax.ShapeDtypeStruct((hh, t, MLA_QK), BF16), jax.ShapeDtypeStruct((hh, t, MLA_QK), BF16),
                   jax.ShapeDtypeStruct((hh, t, MLA_V + LANES), BF16)],
        compiler_params=_cparams(("arbitrary",)),
        name="mla_prep",
    )(cq, ckv, kr, cos128, sin128, q_norm.reshape(1, -1), kv_norm.reshape(1, -1), wq.astype(BF16), wkv.astype(BF16))


def _attn_kernel(q_ref, k_ref, v_ref, o_ref, m_ref, l_ref, acc_ref, sa_ref, sb_ref, *, tq, th, tk):
    qi = pl.program_id(2)
    m_ref[...] = jnp.full_like(m_ref, NEG_INF)
    l_ref[...] = jnp.zeros_like(l_ref)
    acc_ref[...] = jnp.zeros_like(acc_ref)

    halves = (0, 1)
    assert tq == 2 * th and tk == th

    def scores(s_ref, j, which=halves):
        k = k_ref[pl.ds(pl.multiple_of(j * tk, tk), tk), :]
        for half in which:
            rows = pl.ds(half * th, th)
            s_ref[rows, :] = lax.dot_general(q_ref[rows, :], k, (((1,), (1,)), ((), ())),
                                             preferred_element_type=F32)

    def consume(s_ref, j, diag_half=None):
        v = v_ref[pl.ds(pl.multiple_of(j * tk, tk), tk), :]
        for half in halves:
            if diag_half is not None and half < diag_half:
                continue
            rows = pl.ds(half * th, th)
            s = s_ref[rows, :]
            if half == diag_half:
                r = lax.broadcasted_iota(jnp.int32, s.shape, 0)
                c = lax.broadcasted_iota(jnp.int32, s.shape, 1)
                s = jnp.where(c <= r, s, NEG_INF)
            m_prev = m_ref[rows, :]
            m_new = jnp.maximum(m_prev, jnp.max(s, axis=-1, keepdims=True))
            alpha = jnp.exp2(m_prev - m_new)
            p = jnp.exp2(s - jnp.tile(m_new, (1, tk // LANES)))
            pv = _dot(p.astype(BF16), v)
            acc_ref[rows, :] = alpha * acc_ref[rows, :] + pv[:, :MLA_V]
            l_ref[rows, :] = alpha * l_ref[rows, :] + pv[:, MLA_V:]
            m_ref[rows, :] = m_new

    scores(sa_ref, 0)

    def pair(i, carry):
        scores(sb_ref, 2 * i + 1)
        consume(sa_ref, 2 * i)
        scores(sa_ref, 2 * i + 2)
        consume(sb_ref, 2 * i + 1)
        return carry

    lax.fori_loop(0, qi, pair, 0)
    scores(sb_ref, 2 * qi + 1, which=(1,))
    consume(sa_ref, 2 * qi, diag_half=0)
    consume(sb_ref, 2 * qi + 1, diag_half=1)
    o_ref[...] = (acc_ref[...] / l_ref[...]).astype(o_ref.dtype)


def _attention(q3, k3, v3, bsz, seq):
    hh, t, _ = q3.shape
    tq = _tile(seq, 1024)
    th = tq // 2
    tk = th
    nq = seq // tq
    vw = v3.shape[-1]
    return pl.pallas_call(
        functools.partial(_attn_kernel, tq=tq, th=th, tk=tk),
        grid=(bsz, hh, nq),
        in_specs=[pl.BlockSpec((None, tq, MLA_QK), lambda b, h, i: (h, b * nq + i, 0)),
                  pl.BlockSpec((None, seq, MLA_QK), lambda b, h, i: (h, b, 0)),
                  pl.BlockSpec((None, seq, vw), lambda b, h, i: (h, b, 0))],
        out_specs=pl.BlockSpec((tq, MLA_V), lambda b, h, i: (b * nq + i, h)),
        out_shape=jax.ShapeDtypeStruct((t, hh * MLA_V), BF16),
        scratch_shapes=[pltpu.VMEM((tq, LANES), F32), pltpu.VMEM((tq, LANES), F32), pltpu.VMEM((tq, MLA_V), F32),
                        pltpu.VMEM((tq, tk), F32), pltpu.VMEM((tq, tk), F32)],
        compiler_params=_cparams(("arbitrary", "arbitrary", "arbitrary")),
        name="mla_flash_attention",
    )(q3, k3, v3)


def _outproj_kernel(*refs, routed):
    if routed:
        (x_ref, ret_ref, ssm_ref, att_ref, an_ref, w_ref, gpost_ref, mod_ref, gpre_ref, router_ref,
         xo_ref, h_ref, logit_ref) = refs
    else:
        x_ref, ret_ref, ssm_ref, att_ref, an_ref, w_ref, gpost_ref, mod_ref, gpre_ref, xo_ref, h_ref = refs
    d = D_MODEL
    att = _rms(att_ref[...].astype(F32), an_ref[...]).astype(BF16)
    o1 = RET_WIDTH
    o2 = RET_WIDTH + SSM_WIDTH
    y = _dot(ret_ref[...], w_ref[0:o1, :]) + _dot(ssm_ref[...], w_ref[o1:o2, :]) + _dot(att, w_ref[o2:, :])
    x = x_ref[...] + mod_ref[:, 2 * d:3 * d] * _rms(y, gpost_ref[...])
    xo_ref[...] = x
    h = _rms(x, gpre_ref[...]) * (1.0 + mod_ref[:, 4 * d:5 * d]) + mod_ref[:, 3 * d:4 * d]
    h_ref[...] = h.astype(h_ref.dtype)
    if routed:
        logit_ref[...] = _dot3(h, router_ref[...])


def _outproj(x2, ret, ssm, att, mla_norm, w_out, g_post, mod_l, g_pre, seq, router=None):
    t, d = x2.shape
    tm = _tile(seq, 512)
    tpb = seq // tm
    routed = router is not None
    tok = lambda w: pl.BlockSpec((tm, w), lambda i: (i, 0))
    full = lambda a, b: pl.BlockSpec((a, b), lambda i: (0, 0))
    in_specs = [tok(d), tok(RET_WIDTH), tok(SSM_WIDTH), tok(MLA_WIDTH), full(1, MLA_WIDTH), full(d, d), full(1, d),
                pl.BlockSpec((None, 1, 6 * d), lambda i: (i // tpb, 0, 0)), full(1, d)]
    args = [x2, ret, ssm, att, mla_norm.reshape(1, -1), w_out.astype(BF16), g_post.reshape(1, d), mod_l,
            g_pre.reshape(1, d)]
    out_specs = [tok(d), tok(d)]
    out_shape = [jax.ShapeDtypeStruct((t, d), F32), jax.ShapeDtypeStruct((t, d), F32 if routed else BF16)]
    if routed:
        in_specs.append(full(d, LANES))
        args.append(jnp.zeros((d, LANES), F32).at[:, :N_EXPERTS].set(router))
        out_specs.append(tok(LANES))
        out_shape.append(jax.ShapeDtypeStruct((t, LANES), F32))
    return pl.pallas_call(
        functools.partial(_outproj_kernel, routed=routed),
        grid=(t // tm,),
        in_specs=in_specs, out_specs=out_specs, out_shape=out_shape,
        compiler_params=_cparams(("arbitrary",)),
        name="mixer_outproj_routed" if routed else "mixer_outproj",
    )(*args)


def _ffn_kernel(h_ref, wg_ref, wu_ref, wd_ref, x_ref, g_ref, mod_ref, o_ref, acc_ref):
    f = pl.program_id(1)

    @pl.when(f == 0)
    def _():
        acc_ref[...] = jnp.zeros_like(acc_ref)

    h = h_ref[...]
    a = _silu(_dot(h, wg_ref[...].astype(BF16))) * _dot(h, wu_ref[...].astype(BF16))
    acc_ref[...] += _dot(a.astype(BF16), wd_ref[...].astype(BF16))

    @pl.when(f == pl.num_programs(1) - 1)
    def _():
        d = D_MODEL
        o_ref[...] = x_ref[...] + mod_ref[:, 5 * d:6 * d] * _rms(acc_ref[...], g_ref[...])


def _dense_ffn(h, wg, wu, wd, x2, g_post, mod_l, seq):
    t, d = h.shape
    ff = wg.shape[1]
    tm = _tile(seq, 1024)
    tf = _tile(ff, FF_TILE)
    tpb = seq // tm
    return pl.pallas_call(
        _ffn_kernel,
        grid=(t // tm, ff // tf),
        in_specs=[pl.BlockSpec((tm, d), lambda i, f: (i, 0)),
                  pl.BlockSpec((d, tf), lambda i, f: (0, f)),
                  pl.BlockSpec((d, tf), lambda i, f: (0, f)),
                  pl.BlockSpec((tf, d), lambda i, f: (f, 0)),
                  pl.BlockSpec((tm, d), lambda i, f: (i, 0)),
                  pl.BlockSpec((1, d), lambda i, f: (0, 0)),
                  pl.BlockSpec((None, 1, 6 * d), lambda i, f: (i // tpb, 0, 0))],
        out_specs=pl.BlockSpec((tm, d), lambda i, f: (i, 0)),
        out_shape=jax.ShapeDtypeStruct((t, d), F32),
        scratch_shapes=[pltpu.VMEM((tm, d), F32)],
        compiler_params=_cparams(("arbitrary", "arbitrary")),
        name="dense_swiglu",
    )(h, wg, wu, wd, x2, g_post.reshape(1, d), mod_l)


def _route_kernel(logit_ref, tri_ref, info_ref, count_ref, carry_ref):
    @pl.when(pl.program_id(0) == 0)
    def _():
        carry_ref[...] = jnp.zeros_like(carry_ref)

    lg = logit_ref[...]
    lane = lax.broadcasted_iota(jnp.int32, lg.shape, 1)
    lanef = lane.astype(F32)
    valid = lane < N_EXPERTS
    big = float(LANES)
    lg = jnp.where(valid, lg, -jnp.inf)
    m1 = jnp.max(lg, axis=-1, keepdims=True)
    e1 = jnp.min(jnp.where(lg == m1, lanef, big), axis=-1, keepdims=True)
    lg2 = jnp.where(lanef == e1, -jnp.inf, lg)
    m2 = jnp.max(lg2, axis=-1, keepdims=True)
    e2 = jnp.min(jnp.where(lg2 == m2, lanef, big), axis=-1, keepdims=True)
    z = jnp.exp(m2 - m1)
    w1 = 1.0 / (1.0 + z)
    w2 = z / (1.0 + z)
    oh1 = (lanef == e1).astype(F32)
    oh2 = (lanef == e2).astype(F32)
    both = oh1 + oh2
    before = _dot(tri_ref[...], both.astype(BF16)) + carry_ref[0:1, :]
    r1 = jnp.sum(before * oh1, axis=-1, keepdims=True)
    r2 = jnp.sum(before * oh2, axis=-1, keepdims=True)
    carry_ref[0:1, :] = carry_ref[0:1, :] + jnp.sum(both, axis=0, keepdims=True)
    count_ref[...] = carry_ref[...]
    cols = (e1, e2, r1, r2, w1, w2)
    info = jnp.zeros(lg.shape, F32)
    for idx, col in enumerate(cols):
        info = jnp.where(lane == idx, col, info)
    info_ref[...] = info


def _route(logits):
    t = logits.shape[0]
    tm = _tile(t, 512)
    tri = jnp.asarray(np.tril(np.ones((tm, tm), np.float32), -1), BF16)
    info, counts = pl.pallas_call(
        _route_kernel,
        grid=(t // tm,),
        in_specs=[pl.BlockSpec((tm, LANES), lambda i: (i, 0)), pl.BlockSpec((tm, tm), lambda i: (0, 0))],
        out_specs=[pl.BlockSpec((tm, LANES), lambda i: (i, 0)), pl.BlockSpec((8, LANES), lambda i: (0, 0))],
        out_shape=[jax.ShapeDtypeStruct((t, LANES), F32), jax.ShapeDtypeStruct((8, LANES), F32)],
        scratch_shapes=[pltpu.VMEM((8, LANES), F32)],
        compiler_params=_cparams(("arbitrary",)),
        name="moe_route",
    )(logits, tri)
    return info, counts[0, :N_EXPERTS]


def _dispatch_kernel(d1_ref, d2_ref, h_ref, xin_ref, xbuf_ref, sem, *, tm):
    del xin_ref

    def row_copy(r, dref):
        return pltpu.make_async_copy(h_ref.at[pl.ds(r, 1), :], xbuf_ref.at[pl.ds(dref[0, 0, r], 1), :], sem)

    def start(g, carry):
        for j in range(DMA_UNROLL):
            r = g * DMA_UNROLL + j
            row_copy(r, d1_ref).start(priority=0)
            row_copy(r, d2_ref).start(priority=1)
        return carry

    lax.fori_loop(0, tm // DMA_UNROLL, start, 0)
    for _ in range(TOP_K):
        pltpu.make_async_copy(h_ref, xbuf_ref.at[pl.ds(0, tm), :], sem).wait()


def _dispatch(h, dest1, dest2, n_rows):
    t, d = h.shape
    tm = _tile(t, 512)
    nt = t // tm
    smem = lambda: pl.BlockSpec((1, 1, tm), lambda i: (i, 0, 0), memory_space=pltpu.SMEM)
    return pl.pallas_call(
        functools.partial(_dispatch_kernel, tm=tm),
        grid=(nt,),
        in_specs=[smem(), smem(), pl.BlockSpec((tm, d), lambda i: (i, 0)), pl.BlockSpec(memory_space=pl.ANY)],
        out_specs=pl.BlockSpec(memory_space=pl.ANY),
        out_shape=jax.ShapeDtypeStruct((n_rows, d), h.dtype),
        scratch_shapes=[pltpu.SemaphoreType.DMA(())],
        input_output_aliases={3: 0},
        compiler_params=_cparams(("arbitrary",)),
        name="moe_dispatch",
    )(dest1.reshape(nt, 1, tm), dest2.reshape(nt, 1, tm), h, jnp.zeros((n_rows, d), h.dtype))


def _moe_kernel(be_ref, nb_ref, x_ref, wg_ref, wu_ref, wd_ref, o_ref, xb_ref, acc_ref):
    i = pl.program_id(0)
    f = pl.program_id(1)
    active = i < nb_ref[0]

    @pl.when(jnp.logical_and(active, f == 0))
    def _():
        xb_ref[...] = x_ref[...].astype(BF16)
        acc_ref[...] = jnp.zeros_like(acc_ref)

    @pl.when(active)
    def _():
        xb = xb_ref[...]
        a = _silu(_dot(xb, wg_ref[...].astype(BF16))) * _dot(xb, wu_ref[...].astype(BF16))
        acc_ref[...] += _dot(a.astype(BF16), wd_ref[...].astype(BF16))

    @pl.when(f == pl.num_programs(1) - 1)
    def _():
        o_ref[...] = jnp.where(active, acc_ref[...], 0.0)


def _moe_ffn(x_buf, block_expert, n_active, wg, wu, wd):
    n_rows, d = x_buf.shape
    ff = wg.shape[2]
    tm = MOE_TILE
    tf = _tile(ff, FF_TILE)
    nf = ff // tf

    def wmap(i, f, be, nb):
        return be[i], jnp.where(i < nb[0], f, nf - 1)

    grid_spec = pltpu.PrefetchScalarGridSpec(
        num_scalar_prefetch=2,
        grid=(n_rows // tm, nf),
        in_specs=[pl.BlockSpec((tm, d), lambda i, f, be, nb: (jnp.minimum(i, nb[0] - 1), 0)),
                  pl.BlockSpec((None, d, tf), lambda i, f, be, nb: (wmap(i, f, be, nb)[0], 0, wmap(i, f, be, nb)[1])),
                  pl.BlockSpec((None, d, tf), lambda i, f, be, nb: (wmap(i, f, be, nb)[0], 0, wmap(i, f, be, nb)[1])),
                  pl.BlockSpec((None, tf, d), lambda i, f, be, nb: (wmap(i, f, be, nb)[0], wmap(i, f, be, nb)[1], 0))],
        out_specs=pl.BlockSpec((tm, d), lambda i, f, be, nb: (i, 0)),
        scratch_shapes=[pltpu.VMEM((tm, d), BF16), pltpu.VMEM((tm, d), F32)],
    )
    return pl.pallas_call(
        _moe_kernel,
        grid_spec=grid_spec,
        out_shape=jax.ShapeDtypeStruct((n_rows, d), F32),
        compiler_params=_cparams(("arbitrary", "arbitrary")),
        name="moe_grouped_swiglu",
    )(block_expert, n_active, x_buf, wg, wu, wd)


def _combine_kernel(d1_ref, d2_ref, ybuf_ref, info_ref, x_ref, g_ref, mod_ref, o_ref, y1_ref, y2_ref, sem, *, tm):
    def row_copy(r, dref, dst):
        return pltpu.make_async_copy(ybuf_ref.at[pl.ds(dref[0, 0, r], 1), :], dst.at[pl.ds(r, 1), :], sem)

    def start(g, carry):
        for j in range(DMA_UNROLL):
            r = g * DMA_UNROLL + j
            row_copy(r, d1_ref, y1_ref).start(priority=0)
            row_copy(r, d2_ref, y2_ref).start(priority=1)
        return carry

    lax.fori_loop(0, tm // DMA_UNROLL, start, 0)
    for dst in (y1_ref, y2_ref):
        pltpu.make_async_copy(ybuf_ref.at[pl.ds(0, tm), :], dst, sem).wait()
    d = D_MODEL
    info = info_ref[...]
    y = y1_ref[...] * info[:, 4:5] + y2_ref[...] * info[:, 5:6]
    o_ref[...] = x_ref[...] + mod_ref[:, 5 * d:6 * d] * _rms(y, g_ref[...])


def _combine(y_buf, dest1, dest2, info, x2, g_post, mod_l, seq):
    t, d = x2.shape
    tm = _tile(seq, 512)
    nt = t // tm
    tpb = seq // tm
    smem = lambda: pl.BlockSpec((1, 1, tm), lambda i: (i, 0, 0), memory_space=pltpu.SMEM)
    return pl.pallas_call(
        functools.partial(_combine_kernel, tm=tm),
        grid=(nt,),
        in_specs=[smem(), smem(), pl.BlockSpec(memory_space=pl.ANY),
                  pl.BlockSpec((tm, LANES), lambda i: (i, 0)),
                  pl.BlockSpec((tm, d), lambda i: (i, 0)),
                  pl.BlockSpec((1, d), lambda i: (0, 0)),
                  pl.BlockSpec((None, 1, 6 * d), lambda i: (i // tpb, 0, 0))],
        out_specs=pl.BlockSpec((tm, d), lambda i: (i, 0)),
        out_shape=jax.ShapeDtypeStruct((t, d), F32),
        scratch_shapes=[pltpu.VMEM((tm, d), F32), pltpu.VMEM((tm, d), F32), pltpu.SemaphoreType.DMA(())],
        compiler_params=_cparams(("arbitrary",)),
        name="moe_combine",
    )(dest1.reshape(nt, 1, tm), dest2.reshape(nt, 1, tm), y_buf, info, x2, g_post.reshape(1, d), mod_l)


def _routed_ffn(h, logits, wg, wu, wd, x2, g_post, mod_l, seq):
    t, d = h.shape
    tm = MOE_TILE
    info, counts = _route(logits)
    counts = counts.astype(jnp.int32)
    padded = ((counts + tm - 1) // tm) * tm
    pad_end = jnp.cumsum(padded)
    pad_start = pad_end - padded
    e1 = info[:, 0].astype(jnp.int32)
    e2 = info[:, 1].astype(jnp.int32)
    dest1 = pad_start[e1] + info[:, 2].astype(jnp.int32)
    dest2 = pad_start[e2] + info[:, 3].astype(jnp.int32)
    n_rows = t * TOP_K + N_EXPERTS * tm
    n_blocks = n_rows // tm
    block_start = jnp.arange(n_blocks, dtype=jnp.int32) * tm
    block_expert = jnp.minimum(jnp.sum(pad_end[None, :] <= block_start[:, None], axis=1), N_EXPERTS - 1).astype(jnp.int32)
    n_active = (pad_end[-1] // tm).astype(jnp.int32).reshape(1)
    x_buf = _dispatch(h, dest1, dest2, n_rows)
    y_buf = _moe_ffn(x_buf, block_expert, n_active, wg, wu, wd)
    return _combine(y_buf, dest1, dest2, info, x2, g_post, mod_l, seq)


def kernel(x, c, positions, ada_w, ada_b, norm_pre_mix, norm_post_mix, norm_pre_ffn, norm_post_ffn, w_in, ret_norm, ssm_a_re, ssm_a_im, ssm_b_re, ssm_b_im, ssm_c_re, ssm_c_im, ssm_d, ssm_log_dt, ssm_glu_w, ssm_glu_b, ssm_norm, mla_q_norm, mla_w_uq, mla_kv_norm, mla_w_ukv, mla_norm, w_out, ffn_w_gate, ffn_w_up, ffn_w_down, moe_router, moe_w_gate, moe_w_up, moe_w_down):
    bsz, seq, d = x.shape
    depth = ada_w.shape[0]
    assert d == D_MODEL and seq % SSM_CHUNK == 0 and seq % RET_CHUNK == 0
    t = bsz * seq
    x2 = x.reshape(t, d)
    mod = _modulation(c, ada_w, ada_b)
    cos128, sin128 = _rope_tables(positions)
    ssm_params = _ssm_params(ssm_a_re, ssm_a_im, ssm_b_re, ssm_b_im, ssm_c_re, ssm_c_im, ssm_log_dt)
    for layer in range(depth):
        mod_l = mod[layer]
        r, u, cq, ckv, kr = _inproj(x2, norm_pre_mix[layer], mod_l, w_in[layer], seq)
        ret = _retention(r, cos128, sin128, ret_norm[layer], bsz, seq)
        ssm = _ssm_post(_ssm_scan(u, ssm_params, layer, seq), u, ssm_d[layer], ssm_glu_w[layer], ssm_glu_b[layer],
                        ssm_norm[layer])
        q3, k3, v3 = _mla_prep(cq, ckv, kr, cos128, sin128, mla_q_norm[layer], mla_w_uq[layer],
                               mla_kv_norm[layer], mla_w_ukv[layer])
        att = _attention(q3, k3, v3, bsz, seq)
        j = layer // 2
        if layer % 2 == 0:
            x2, h = _outproj(x2, ret, ssm, att, mla_norm[layer], w_out[layer], norm_post_mix[layer], mod_l,
                             norm_pre_ffn[layer], seq)
            x2 = _dense_ffn(h, ffn_w_gate[j], ffn_w_up[j], ffn_w_down[j], x2, norm_post_ffn[layer], mod_l, seq)
        else:
            x2, h, logits = _outproj(x2, ret, ssm, att, mla_norm[layer], w_out[layer], norm_post_mix[layer], mod_l,
                                     norm_pre_ffn[layer], seq, router=moe_router[j])
            x2 = _routed_ffn(h, logits, moe_w_gate[j], moe_w_up[j], moe_w_down[j], x2, norm_post_ffn[layer],
                             mod_l, seq)
    return x2.reshape(bsz, seq, d)
```

```python
import functools
import math

import numpy as np
import jax
import jax.numpy as jnp
from jax import lax
from jax.experimental import pallas as pl
from jax.experimental.pallas import tpu as pltpu

D_MODEL = 1024
RET_HEADS = 4
RET_HEAD_DIM = 64
RET_WIDTH = RET_HEADS * RET_HEAD_DIM
RET_CHUNK = 128
SSM_GROUP_CH = 16
SSM_GROUPS = 16
SSM_WIDTH = SSM_GROUPS * SSM_GROUP_CH
SSM_STATE = 64
MLA_HEADS = 4
MLA_NOPE = 128
MLA_ROPE = 64
MLA_V = 128
MLA_QK = MLA_NOPE + MLA_ROPE
MLA_WIDTH = MLA_HEADS * MLA_V
MLA_Q_RANK = 256
MLA_KV_RANK = 128
ROPE_DIM = 64
ROPE_BASE = 10000.0
D_FF = 3584
N_EXPERTS = 8
TOP_K = 2
EPS = 1e-6
NEG_INF = -1e30
IN_COLS = 4 * RET_WIDTH + SSM_WIDTH + MLA_Q_RANK + MLA_KV_RANK + MLA_ROPE

LANES = 128
VMEM_LIMIT_BYTES = 56 * 1024 * 1024

SSM_CHUNK = 64
MOE_TILE = 1024
FF_TILE = 512
DMA_UNROLL = 8

BF16 = jnp.bfloat16
F32 = jnp.float32


def _cparams(sem):
    return pltpu.CompilerParams(dimension_semantics=sem, vmem_limit_bytes=VMEM_LIMIT_BYTES)


def _tile(n, pref):
    t = min(n, pref)
    while n % t:
        t //= 2
    return t


def _dot(a, b):
    return jnp.dot(a, b, preferred_element_type=F32)


def _dot3(a, b):
    a_hi = a.astype(BF16)
    a_lo = (a - a_hi.astype(F32)).astype(BF16)
    b_hi = b.astype(BF16)
    b_lo = (b - b_hi.astype(F32)).astype(BF16)
    return _dot(a_hi, b_hi) + (_dot(a_hi, b_lo) + _dot(a_lo, b_hi))


def _rms(x, g):
    return x * lax.rsqrt(jnp.mean(x * x, axis=-1, keepdims=True) + EPS) * g


def _silu(x):
    return x * (1.0 / (1.0 + jnp.exp(-x)))


def _rope128(x, cos, sin_signed):
    lane = lax.broadcasted_iota(jnp.int32, x.shape, 1)
    first_half = (lane % ROPE_DIM) < (ROPE_DIM // 2)
    partner = jnp.where(first_half, pltpu.roll(x, LANES - ROPE_DIM // 2, 1), pltpu.roll(x, ROPE_DIM // 2, 1))
    return x * cos + partner * sin_signed


def _mod_kernel(c_ref, w_ref, b_ref, o_ref):
    cond = _silu(c_ref[...])
    o_ref[...] = _dot3(cond, w_ref[...]) + b_ref[...]


def _modulation(c, ada_w, ada_b):
    depth, d, n = ada_w.shape
    bsz = c.shape[0]
    rows = 8
    c_pad = jnp.zeros((rows, d), F32).at[:bsz].set(c)
    tn = _tile(n, 1536)
    out = pl.pallas_call(
        _mod_kernel,
        grid=(depth, n // tn),
        in_specs=[pl.BlockSpec((rows, d), lambda l, j: (0, 0)),
                  pl.BlockSpec((None, d, tn), lambda l, j: (l, 0, j)),
                  pl.BlockSpec((None, 1, tn), lambda l, j: (l, 0, j))],
        out_specs=pl.BlockSpec((None, rows, tn), lambda l, j: (l, 0, j)),
        out_shape=jax.ShapeDtypeStruct((depth, rows, n), F32),
        compiler_params=_cparams(("arbitrary", "arbitrary")),
        name="adaln_mod",
    )(c_pad, ada_w, ada_b.reshape(depth, 1, n))
    return out[:, :bsz].reshape(depth, bsz, 1, n)


def _rope_kernel(pos_ref, inv_ref, cos_ref, sin_ref):
    pos = pos_ref[...].astype(F32)
    lane = lax.broadcasted_iota(jnp.int32, cos_ref.shape, 1)
    half = ROPE_DIM // 2
    p = jnp.where(lane < half, pos[:, 0:1],
                  jnp.where(lane < 2 * half, pos[:, 1:2],
                            jnp.where(lane < 3 * half, pos[:, 2:3], pos[:, 3:4])))
    ang = p * inv_ref[...]
    cos_ref[...] = jnp.cos(ang)
    sin_ref[...] = jnp.sin(ang)


def _rope_tables(positions):
    t = positions.size
    half = ROPE_DIM // 2
    per_row = LANES // half
    inv = ROPE_BASE ** (-jnp.arange(0, ROPE_DIM, 2, dtype=F32) / ROPE_DIM)
    inv_row = jnp.tile(inv, per_row).reshape(1, LANES)
    rows = t // per_row
    tr = _tile(rows, 1024)
    cos, sin = pl.pallas_call(
        _rope_kernel,
        grid=(rows // tr,),
        in_specs=[pl.BlockSpec((tr, per_row), lambda i: (i, 0)),
                  pl.BlockSpec((1, LANES), lambda i: (0, 0))],
        out_specs=[pl.BlockSpec((tr, LANES), lambda i: (i, 0))] * 2,
        out_shape=[jax.ShapeDtypeStruct((rows, LANES), F32)] * 2,
        compiler_params=_cparams(("arbitrary",)),
        name="rope_tables",
    )(positions.reshape(rows, per_row), inv_row)
    cos = cos.reshape(t, half)
    sin = sin.reshape(t, half)
    return jnp.tile(cos, (1, per_row)), jnp.concatenate([-sin, sin, -sin, sin], axis=1)


def _inproj_kernel(x_ref, g_ref, mod_ref, w_ref, ret_ref, u_ref, cq_ref, ckv_ref, kr_ref):
    d = D_MODEL
    x = x_ref[...]
    h = _rms(x, g_ref[...]) * (1.0 + mod_ref[:, d:2 * d]) + mod_ref[:, 0:d]
    p = _dot(h.astype(BF16), w_ref[...])
    o = 4 * RET_WIDTH
    ret_ref[...] = p[:, :o].astype(BF16)
    u_ref[...] = p[:, o:o + SSM_WIDTH].astype(BF16)
    o += SSM_WIDTH
    cq_ref[...] = p[:, o:o + MLA_Q_RANK].astype(BF16)
    o += MLA_Q_RANK
    ckv_ref[...] = p[:, o:o + MLA_KV_RANK].astype(BF16)
    o += MLA_KV_RANK
    kr_ref[...] = p[:, o:o + MLA_ROPE].astype(BF16)


def _inproj(x2, g, mod_l, w_in, seq):
    t, d = x2.shape
    tm = _tile(seq, 512)
    tpb = seq // tm
    widths = (4 * RET_WIDTH, SSM_WIDTH, MLA_Q_RANK, MLA_KV_RANK, MLA_ROPE)
    return pl.pallas_call(
        _inproj_kernel,
        grid=(t // tm,),
        in_specs=[pl.BlockSpec((tm, d), lambda i: (i, 0)),
                  pl.BlockSpec((1, d), lambda i: (0, 0)),
                  pl.BlockSpec((None, 1, 6 * d), lambda i: (i // tpb, 0, 0)),
                  pl.BlockSpec((d, IN_COLS), lambda i: (0, 0))],
        out_specs=[pl.BlockSpec((tm, w), lambda i: (i, 0)) for w in widths],
        out_shape=[jax.ShapeDtypeStruct((t, w), BF16) for w in widths],
        compiler_params=_cparams(("arbitrary",)),
        name="mixer_inproj",
    )(x2, g.reshape(1, d), mod_l, w_in.astype(BF16))


def _retention_kernel(r_ref, cos_ref, sin_ref, dmask_ref, qdec_ref, kdec_ref, cdec_ref, bd_ref, gn_ref,
                      o_ref, state_ref, *, chunks):
    w = RET_WIDTH

    @pl.when(pl.program_id(1) == 0)
    def _():
        state_ref[...] = jnp.zeros_like(state_ref)

    lane = lax.broadcasted_iota(jnp.int32, (RET_CHUNK, w), 1)
    bd = bd_ref[...]
    for c in range(chunks):
        rows = pl.ds(c * RET_CHUNK, RET_CHUNK)
        cos = cos_ref[rows, :]
        sin = sin_ref[rows, :]

        def rope(v):
            return jnp.concatenate([_rope128(v[:, :LANES], cos, sin), _rope128(v[:, LANES:], cos, sin)], axis=1)

        q = rope(r_ref[rows, 0:w].astype(F32))
        k = rope(r_ref[rows, w:2 * w].astype(F32)) * (RET_HEAD_DIM ** -0.5)
        v = r_ref[rows, 2 * w:3 * w]
        gate = r_ref[rows, 3 * w:4 * w].astype(F32)
        kb = k.astype(BF16)
        state = state_ref[...]
        y = _dot((q * qdec_ref[...]).astype(BF16), state.astype(BF16))
        for h in range(RET_HEADS):
            in_head = (lane // RET_HEAD_DIM) == h
            qh = jnp.where(in_head, q, 0.0).astype(BF16)
            s = lax.dot_general(qh, kb, (((1,), (1,)), ((), ())), preferred_element_type=F32)
            s = s * dmask_ref[h]
            y = y + jnp.where(in_head, _dot(s.astype(BF16), v), 0.0)
        kv = lax.dot_general((k * kdec_ref[...]).astype(BF16), v, (((0,), (0,)), ((), ())),
                             preferred_element_type=F32)
        state_ref[...] = cdec_ref[...] * state + bd * kv
        y2 = y * y
        y2_hi = y2.astype(BF16)
        y2_lo = (y2 - y2_hi.astype(F32)).astype(BF16)
        bdb = bd.astype(BF16)
        ms = (_dot(y2_hi, bdb) + _dot(y2_lo, bdb)) * (1.0 / RET_HEAD_DIM)
        yn = y * lax.rsqrt(ms + EPS) * gn_ref[...]
        o_ref[rows, :] = (_silu(gate) * yn).astype(BF16)


def _retention_consts():
    c = RET_CHUNK
    log_g = np.log1p(-(2.0 ** (-5.0 - np.arange(RET_HEADS, dtype=np.float32)))).astype(np.float32)
    i = np.arange(c, dtype=np.float32)
    diff = i[:, None] - i[None, :]
    dmask = np.where(diff >= 0, np.exp(log_g[:, None, None] * np.maximum(diff, 0.0)), 0.0).astype(np.float32)
    k_dec = np.exp(log_g[None, :] * (c - 1 - i)[:, None]).astype(np.float32)
    q_dec = np.exp(log_g[None, :] * (i + 1.0)[:, None]).astype(np.float32)
    c_dec = np.exp(log_g * c).astype(np.float32)
    rep = lambda a: np.repeat(a, RET_HEAD_DIM, axis=-1)
    head = np.arange(RET_WIDTH) // RET_HEAD_DIM
    bd = (head[:, None] == head[None, :]).astype(np.float32)
    return dmask, rep(q_dec), rep(k_dec), rep(c_dec[None, :]), bd


def _retention(r, cos128, sin128, ret_norm, bsz, seq):
    t = r.shape[0]
    w = RET_WIDTH
    tr = _tile(seq, 512)
    chunks = tr // RET_CHUNK
    nt = seq // tr
    dmask, qdec, kdec, cdec, bd = (jnp.asarray(a) for a in _retention_consts())
    const = lambda shape: pl.BlockSpec(shape, lambda b, i: (0,) * len(shape))
    return pl.pallas_call(
        functools.partial(_retention_kernel, chunks=chunks),
        grid=(bsz, nt),
        in_specs=[pl.BlockSpec((tr, 4 * w), lambda b, i: (b * nt + i, 0)),
                  pl.BlockSpec((tr, LANES), lambda b, i: (b * nt + i, 0)),
                  pl.BlockSpec((tr, LANES), lambda b, i: (b * nt + i, 0)),
                  const((RET_HEADS, RET_CHUNK, RET_CHUNK)),
                  const((RET_CHUNK, w)), const((RET_CHUNK, w)), const((1, w)), const((w, w)), const((1, w))],
        out_specs=pl.BlockSpec((tr, w), lambda b, i: (b * nt + i, 0)),
        out_shape=jax.ShapeDtypeStruct((t, w), BF16),
        scratch_shapes=[pltpu.VMEM((w, w), F32)],
        compiler_params=_cparams(("arbitrary", "arbitrary")),
        name="retention",
    )(r, cos128, sin128, dmask, qdec, kdec, cdec, bd, ret_norm.reshape(1, w))


def _expand(x, sel):
    hi = x.astype(BF16)
    lo = (x - hi.astype(F32)).astype(BF16)
    return _dot(hi, sel) + _dot(lo, sel)


def _ssm_kernel(u_ref, acol_ref, arow_ref, ldt_ref, b_ref, bt_ref, ct_ref, esel_ref, ech_ref, y_ref, toep_ref,
                *, chunks_per_seq, steps):
    tc, hg, p = SSM_CHUNK, SSM_GROUP_CH, SSM_STATE
    k = tc * hg
    dt = jnp.exp(ldt_ref[...])
    ar, ai = acol_ref[:, 0:1], acol_ref[:, 1:2]
    n = lax.broadcasted_iota(jnp.int32, (p, LANES), 1).astype(F32)
    mag = jnp.exp(ar * dt * n)
    pw_re, pw_im = mag * jnp.cos(ai * dt * n), mag * jnp.sin(ai * dt * n)
    def zoh(a_r, a_i):
        m1 = jnp.exp(a_r * dt)
        l_r, l_i = m1 * jnp.cos(a_i * dt), m1 * jnp.sin(a_i * dt)
        den = a_r * a_r + a_i * a_i
        return ((l_r - 1.0) * a_r + l_i * a_i) / den, (l_i * a_r - (l_r - 1.0) * a_i) / den

    f_re, f_im = zoh(ar, ai)
    bb_re = f_re * b_ref[0] - f_im * b_ref[1]
    bb_im = f_re * b_ref[1] + f_im * b_ref[0]
    fr_re, fr_im = zoh(arow_ref[0:1, 0:p], arow_ref[1:2, 0:p])
    bbt_re = fr_re * bt_ref[0] - fr_im * bt_ref[1]
    bbt_im = fr_re * bt_ref[1] + fr_im * bt_ref[0]
    ech = ech_ref[...]
    pt_re, pt_im = _expand(pw_re, esel_ref[0]), _expand(pw_im, esel_ref[0])
    p1_re, p1_im = _expand(pw_re, esel_ref[1]), _expand(pw_im, esel_ref[1])
    pr_re, pr_im = _expand(pw_re, esel_ref[2]), _expand(pw_im, esel_ref[2])
    bbr, bbi = _expand(bb_re, ech), _expand(bb_im, ech)
    cr, ci = _expand(ct_ref[0], ech), _expand(ct_ref[1], ech)
    z_re = cr * pt_re - ci * pt_im
    z_im = cr * pt_im + ci * pt_re
    strip = _dot3(bbt_re, z_re) - _dot3(bbt_im, z_im)
    w_in_t = jnp.concatenate([pr_re * bbr - pr_im * bbi, pr_re * bbi + pr_im * bbr], axis=0).astype(BF16)
    v_out = jnp.concatenate([cr * p1_re - ci * p1_im, -(cr * p1_im + ci * p1_re)], axis=0).astype(BF16)
    padded = jnp.concatenate([jnp.zeros((hg, k), F32), strip], axis=1)
    for s in range(tc):
        toep_ref[s * hg:(s + 1) * hg, :] = padded[:, k - s * hg:2 * k - s * hg].astype(BF16)
    dist = (tc * (1 << lax.broadcasted_iota(jnp.int32, (steps, 2 * p), 0))).astype(F32)
    lane = lax.broadcasted_iota(jnp.int32, (steps, 2 * p), 1)
    amag = jnp.exp(arow_ref[0:1, :] * dt * dist)
    a_cos = amag * jnp.cos(arow_ref[1:2, :] * dt * dist)
    a_sin = jnp.where(lane < p, -1.0, 1.0) * amag * jnp.sin(arow_ref[1:2, :] * dt * dist)

    u = u_ref[...]
    x = lax.dot_general(u, w_in_t, (((1,), (1,)), ((), ())), preferred_element_type=F32)
    row = lax.broadcasted_iota(jnp.int32, x.shape, 0) % chunks_per_seq

    def shift_rows(a, d):
        return jnp.where(row >= d, pltpu.roll(a, d, 0), 0.0)

    for j in range(steps):
        prev = shift_rows(x, 1 << j)
        x = x + prev * a_cos[j:j + 1, :] + pltpu.roll(prev, p, 1) * a_sin[j:j + 1, :]
    x_in = shift_rows(x, 1)
    y_ref[...] = _dot(u, toep_ref[...]) + _dot(x_in.astype(BF16), v_out)


def _ssm_params(a_re, a_im, b_re, b_im, c_re, c_im, log_dt):
    a_col = jnp.stack([a_re, a_im], axis=-1)
    a_row = jnp.stack([jnp.concatenate([a_re, a_re], -1), jnp.concatenate([a_im, a_im], -1)], axis=2)
    b = jnp.stack([b_re, b_im], axis=2)
    bt = jnp.swapaxes(b, -1, -2)
    ct = jnp.swapaxes(jnp.stack([c_re, c_im], axis=2), -1, -2)
    return a_col, a_row, log_dt[..., None, None], b, bt, ct


def _ssm_selectors():
    tc, hg = SSM_CHUNK, SSM_GROUP_CH
    t_of_lane = np.arange(tc * hg) // hg
    n = np.arange(LANES)[:, None]
    esel = np.stack([n == t_of_lane, n == t_of_lane + 1, n == tc - 1 - t_of_lane]).astype(np.float32)
    ech = (np.arange(hg)[:, None] == (np.arange(tc * hg) % hg)[None, :]).astype(np.float32)
    return jnp.asarray(esel, BF16), jnp.asarray(ech, BF16)


def _ssm_scan(u, params, layer, seq):
    t = u.shape[0]
    g, hg, tc, p = SSM_GROUPS, SSM_GROUP_CH, SSM_CHUNK, SSM_STATE
    nc = t // tc
    cps = seq // tc
    steps = max(1, int(math.ceil(math.log2(cps))))
    k = tc * hg
    esel, ech = _ssm_selectors()
    ug = u.reshape(nc, tc, g, hg).transpose(2, 0, 1, 3).reshape(g, nc, k)
    per_group = lambda *blk: pl.BlockSpec((None, None) + blk, lambda i: (layer, i) + (0,) * len(blk))
    y = pl.pallas_call(
        functools.partial(_ssm_kernel, chunks_per_seq=cps, steps=steps),
        grid=(g,),
        in_specs=[pl.BlockSpec((None, nc, k), lambda i: (i, 0, 0)),
                  per_group(p, 2), per_group(2, 2 * p), per_group(1, 1),
                  per_group(2, p, hg), per_group(2, hg, p), per_group(2, p, hg),
                  pl.BlockSpec((3, LANES, k), lambda i: (0, 0, 0)),
                  pl.BlockSpec((hg, k), lambda i: (0, 0))],
        out_specs=pl.BlockSpec((None, nc, k), lambda i: (i, 0, 0)),
        out_shape=jax.ShapeDtypeStruct((g, nc, k), F32),
        scratch_shapes=[pltpu.VMEM((k, k), BF16)],
        compiler_params=_cparams(("arbitrary",)),
        name="s5_chunk_scan",
    )(ug, *params, esel, ech)
    return y.reshape(g, nc, tc, hg).transpose(1, 2, 0, 3).reshape(t, g * hg)


def _ssm_post_kernel(y_ref, u_ref, d_ref, gw_ref, gb_ref, gn_ref, o_ref):
    y = y_ref[...] + d_ref[...] * u_ref[...].astype(F32)
    y = 0.5 * y * (1.0 + jnp.tanh(math.sqrt(2.0 / math.pi) * (y + 0.044715 * (y * y * y))))
    z = _dot(y.astype(BF16), gw_ref[...]) + gb_ref[...]
    y = y * (1.0 / (1.0 + jnp.exp(-z)))
    o_ref[...] = _rms(y, gn_ref[...]).astype(BF16)


def _ssm_post(y, u, d, glu_w, glu_b, ssm_norm):
    t, w = y.shape
    tm = _tile(t, 1024)
    row = lambda: pl.BlockSpec((1, w), lambda i: (0, 0))
    return pl.pallas_call(
        _ssm_post_kernel,
        grid=(t // tm,),
        in_specs=[pl.BlockSpec((tm, w), lambda i: (i, 0)), pl.BlockSpec((tm, w), lambda i: (i, 0)),
                  row(), pl.BlockSpec((w, w), lambda i: (0, 0)), row(), row()],
        out_specs=pl.BlockSpec((tm, w), lambda i: (i, 0)),
        out_shape=jax.ShapeDtypeStruct((t, w), BF16),
        compiler_params=_cparams(("arbitrary",)),
        name="s5_gelu_glu_norm",
    )(y, u, d.reshape(1, w), glu_w.astype(BF16), glu_b.reshape(1, w), ssm_norm.reshape(1, w))


def _mla_prep_kernel(cq_ref, ckv_ref, kr_ref, cos_ref, sin_ref, qn_ref, kvn_ref, wq_ref, wkv_ref,
                     q_ref, k_ref, v_ref):
    cos = cos_ref[...]
    sin = sin_ref[...]
    nope_w = MLA_HEADS * MLA_NOPE
    q = _dot(_rms(cq_ref[...].astype(F32), qn_ref[...]).astype(BF16), wq_ref[...])
    q = q * (MLA_QK ** -0.5 * math.log2(math.e))
    q_rope = jnp.concatenate([_rope128(q[:, nope_w:nope_w + LANES], cos, sin),
                              _rope128(q[:, nope_w + LANES:], cos, sin)], axis=1)
    kv = _dot(_rms(ckv_ref[...].astype(F32), kvn_ref[...]).astype(BF16), wkv_ref[...])
    kr = kr_ref[...].astype(F32)
    k_pe = _rope128(jnp.concatenate([kr, kr], axis=1), cos, sin)[:, :MLA_ROPE]
    for h in range(MLA_HEADS):
        q_ref[h] = jnp.concatenate([q[:, h * MLA_NOPE:(h + 1) * MLA_NOPE],
                                    q_rope[:, h * MLA_ROPE:(h + 1) * MLA_ROPE]], axis=1).astype(BF16)
        k_ref[h] = jnp.concatenate([kv[:, h * MLA_NOPE:(h + 1) * MLA_NOPE], k_pe], axis=1).astype(BF16)
        v_ref[h, :, 0:MLA_V] = kv[:, nope_w + h * MLA_V:nope_w + (h + 1) * MLA_V].astype(BF16)
        v_ref[h, :, MLA_V:] = jnp.ones((kv.shape[0], LANES), BF16)


def _mla_prep(cq, ckv, kr, cos128, sin128, q_norm, w_uq, kv_norm, w_ukv):
    t = cq.shape[0]
    tm = _tile(t, 512)
    hh = MLA_HEADS
    wq = w_uq.reshape(MLA_Q_RANK, hh, MLA_QK)
    wq = jnp.concatenate([wq[:, :, :MLA_NOPE].reshape(MLA_Q_RANK, -1), wq[:, :, MLA_NOPE:].reshape(MLA_Q_RANK, -1)], 1)
    wkv = w_ukv.reshape(MLA_KV_RANK, hh, MLA_NOPE + MLA_V)
    wkv = jnp.concatenate([wkv[:, :, :MLA_NOPE].reshape(MLA_KV_RANK, -1), wkv[:, :, MLA_NOPE:].reshape(MLA_KV_RANK, -1)], 1)
    tok = lambda w: pl.BlockSpec((tm, w), lambda i: (i, 0))
    full = lambda a, b: pl.BlockSpec((a, b), lambda i: (0, 0))
    return pl.pallas_call(
        _mla_prep_kernel,
        grid=(t // tm,),
        in_specs=[tok(MLA_Q_RANK), tok(MLA_KV_RANK), tok(MLA_ROPE), tok(LANES), tok(LANES),
                  full(1, MLA_Q_RANK), full(1, MLA_KV_RANK),
                  full(MLA_Q_RANK, hh * MLA_QK), full(MLA_KV_RANK, hh * (MLA_NOPE + MLA_V))],
        out_specs=[pl.BlockSpec((hh, tm, MLA_QK), lambda i: (0, i, 0)),
                   pl.BlockSpec((hh, tm, MLA_QK), lambda i: (0, i, 0)),
                   pl.BlockSpec((hh, tm, MLA_V + LANES), lambda i: (0, i, 0))],
        out_shape=[jax.ShapeDtypeStruct((hh, t, MLA_QK), BF16), jax.ShapeDtypeStruct((hh, t, MLA_QK), BF16),
                   jax.ShapeDtypeStruct((hh, t, MLA_V + LANES), BF16)],
        compiler_params=_cparams(("arbitrary",)),
        name="mla_prep",
    )(cq, ckv, kr, cos128, sin128, q_norm.reshape(1, -1), kv_norm.reshape(1, -1), wq.astype(BF16), wkv.astype(BF16))


def _attn_kernel(q_ref, k_ref, v_ref, o_ref, m_ref, l_ref, acc_ref, sa_ref, sb_ref, *, tq, th, tk):
    qi = pl.program_id(2)
    m_ref[...] = jnp.full_like(m_ref, NEG_INF)
    l_ref[...] = jnp.zeros_like(l_ref)
    acc_ref[...] = jnp.zeros_like(acc_ref)

    halves = (0, 1)
    assert tq == 2 * th and tk == th

    def scores(s_ref, j, which=halves):
        k = k_ref[pl.ds(pl.multiple_of(j * tk, tk), tk), :]
        for half in which:
            rows = pl.ds(half * th, th)
            s_ref[rows, :] = lax.dot_general(q_ref[rows, :], k, (((1,), (1,)), ((), ())),
                                             preferred_element_type=F32)

    def consume(s_ref, j, diag_half=None):
        v = v_ref[pl.ds(pl.multiple_of(j * tk, tk), tk), :]
        for half in halves:
            if diag_half is not None and half < diag_half:
                continue
            rows = pl.ds(half * th, th)
            s = s_ref[rows, :]
            if half == diag_half:
                r = lax.broadcasted_iota(jnp.int32, s.shape, 0)
                c = lax.broadcasted_iota(jnp.int32, s.shape, 1)
                s = jnp.where(c <= r, s, NEG_INF)
            m_prev = m_ref[rows, :]
            m_new = jnp.maximum(m_prev, jnp.max(s, axis=-1, keepdims=True))
            alpha = jnp.exp2(m_prev - m_new)
            p = jnp.exp2(s - jnp.tile(m_new, (1, tk // LANES)))
            pv = _dot(p.astype(BF16), v)
            acc_ref[rows, :] = alpha * acc_ref[rows, :] + pv[:, :MLA_V]
            l_ref[rows, :] = alpha * l_ref[rows, :] + pv[:, MLA_V:]
            m_ref[rows, :] = m_new

    scores(sa_ref, 0)

    def pair(i, carry):
        scores(sb_ref, 2 * i + 1)
        consume(sa_ref, 2 * i)
        scores(sa_ref, 2 * i + 2)
        consume(sb_ref, 2 * i + 1)
        return carry

    def two_pairs(i2, carry):
        pair(2 * i2, carry)
        return pair(2 * i2 + 1, carry)

    lax.fori_loop(0, qi // 2, two_pairs, 0)

    @pl.when(qi % 2 == 1)
    def _():
        pair(qi - 1, 0)

    scores(sb_ref, 2 * qi + 1, which=(1,))
    consume(sa_ref, 2 * qi, diag_half=0)
    consume(sb_ref, 2 * qi + 1, diag_half=1)
    o_ref[...] = (acc_ref[...] / l_ref[...]).astype(o_ref.dtype)


def _attention(q3, k3, v3, bsz, seq):
    hh, t, _ = q3.shape
    tq = _tile(seq, 1024)
    th = tq // 2
    tk = th
    nq = seq // tq
    vw = v3.shape[-1]
    return pl.pallas_call(
        functools.partial(_attn_kernel, tq=tq, th=th, tk=tk),
        grid=(bsz, hh, nq),
        in_specs=[pl.BlockSpec((None, tq, MLA_QK), lambda b, h, i: (h, b * nq + i, 0)),
                  pl.BlockSpec((None, seq, MLA_QK), lambda b, h, i: (h, b, 0)),
                  pl.BlockSpec((None, seq, vw), lambda b, h, i: (h, b, 0))],
        out_specs=pl.BlockSpec((tq, MLA_V), lambda b, h, i: (b * nq + i, h)),
        out_shape=jax.ShapeDtypeStruct((t, hh * MLA_V), BF16),
        scratch_shapes=[pltpu.VMEM((tq, LANES), F32), pltpu.VMEM((tq, LANES), F32), pltpu.VMEM((tq, MLA_V), F32),
                        pltpu.VMEM((tq, tk), F32), pltpu.VMEM((tq, tk), F32)],
        compiler_params=_cparams(("arbitrary", "arbitrary", "arbitrary")),
        name="mla_flash_attention",
    )(q3, k3, v3)


def _outproj_kernel(*refs, routed):
    if routed:
        (x_ref, ret_ref, ssm_ref, att_ref, an_ref, w_ref, gpost_ref, mod_ref, gpre_ref, router_ref,
         xo_ref, h_ref, logit_ref) = refs
    else:
        x_ref, ret_ref, ssm_ref, att_ref, an_ref, w_ref, gpost_ref, mod_ref, gpre_ref, xo_ref, h_ref = refs
    d = D_MODEL
    att = _rms(att_ref[...].astype(F32), an_ref[...]).astype(BF16)
    o1 = RET_WIDTH
    o2 = RET_WIDTH + SSM_WIDTH
    y = _dot(ret_ref[...], w_ref[0:o1, :]) + _dot(ssm_ref[...], w_ref[o1:o2, :]) + _dot(att, w_ref[o2:, :])
    x = x_ref[...] + mod_ref[:, 2 * d:3 * d] * _rms(y, gpost_ref[...])
    xo_ref[...] = x
    h = _rms(x, gpre_ref[...]) * (1.0 + mod_ref[:, 4 * d:5 * d]) + mod_ref[:, 3 * d:4 * d]
    h_ref[...] = h.astype(h_ref.dtype)
    if routed:
        logit_ref[...] = _dot3(h, router_ref[...])


def _outproj(x2, ret, ssm, att, mla_norm, w_out, g_post, mod_l, g_pre, seq, router=None):
    t, d = x2.shape
    tm = _tile(seq, 512)
    tpb = seq // tm
    routed = router is not None
    tok = lambda w: pl.BlockSpec((tm, w), lambda i: (i, 0))
    full = lambda a, b: pl.BlockSpec((a, b), lambda i: (0, 0))
    in_specs = [tok(d), tok(RET_WIDTH), tok(SSM_WIDTH), tok(MLA_WIDTH), full(1, MLA_WIDTH), full(d, d), full(1, d),
                pl.BlockSpec((None, 1, 6 * d), lambda i: (i // tpb, 0, 0)), full(1, d)]
    args = [x2, ret, ssm, att, mla_norm.reshape(1, -1), w_out.astype(BF16), g_post.reshape(1, d), mod_l,
            g_pre.reshape(1, d)]
    out_specs = [tok(d), tok(d)]
    out_shape = [jax.ShapeDtypeStruct((t, d), F32), jax.ShapeDtypeStruct((t, d), F32 if routed else BF16)]
    if routed:
        in_specs.append(full(d, LANES))
        args.append(jnp.zeros((d, LANES), F32).at[:, :N_EXPERTS].set(router))
        out_specs.append(tok(LANES))
        out_shape.append(jax.ShapeDtypeStruct((t, LANES), F32))
    return pl.pallas_call(
        functools.partial(_outproj_kernel, routed=routed),
        grid=(t // tm,),
        in_specs=in_specs, out_specs=out_specs, out_shape=out_shape,
        compiler_params=_cparams(("arbitrary",)),
        name="mixer_outproj_routed" if routed else "mixer_outproj",
    )(*args)


def _ffn_kernel(h_ref, wg_ref, wu_ref, wd_ref, x_ref, g_ref, mod_ref, o_ref, acc_ref):
    f = pl.program_id(1)

    @pl.when(f == 0)
    def _():
        acc_ref[...] = jnp.zeros_like(acc_ref)

    h = h_ref[...]
    a = _silu(_dot(h, wg_ref[...].astype(BF16))) * _dot(h, wu_ref[...].astype(BF16))
    acc_ref[...] += _dot(a.astype(BF16), wd_ref[...].astype(BF16))

    @pl.when(f == pl.num_programs(1) - 1)
    def _():
        d = D_MODEL
        o_ref[...] = x_ref[...] + mod_ref[:, 5 * d:6 * d] * _rms(acc_ref[...], g_ref[...])


def _dense_ffn(h, wg, wu, wd, x2, g_post, mod_l, seq):
    t, d = h.shape
    ff = wg.shape[1]
    tm = _tile(seq, 1024)
    tf = _tile(ff, FF_TILE)
    tpb = seq // tm
    return pl.pallas_call(
        _ffn_kernel,
        grid=(t // tm, ff // tf),
        in_specs=[pl.BlockSpec((tm, d), lambda i, f: (i, 0)),
                  pl.BlockSpec((d, tf), lambda i, f: (0, f)),
                  pl.BlockSpec((d, tf), lambda i, f: (0, f)),
                  pl.BlockSpec((tf, d), lambda i, f: (f, 0)),
                  pl.BlockSpec((tm, d), lambda i, f: (i, 0)),
                  pl.BlockSpec((1, d), lambda i, f: (0, 0)),
                  pl.BlockSpec((None, 1, 6 * d), lambda i, f: (i // tpb, 0, 0))],
        out_specs=pl.BlockSpec((tm, d), lambda i, f: (i, 0)),
        out_shape=jax.ShapeDtypeStruct((t, d), F32),
        scratch_shapes=[pltpu.VMEM((tm, d), F32)],
        compiler_params=_cparams(("arbitrary", "arbitrary")),
        name="dense_swiglu",
    )(h, wg, wu, wd, x2, g_post.reshape(1, d), mod_l)


def _route_kernel(logit_ref, tri_ref, info_ref, count_ref, carry_ref):
    @pl.when(pl.program_id(0) == 0)
    def _():
        carry_ref[...] = jnp.zeros_like(carry_ref)

    lg = logit_ref[...]
    lane = lax.broadcasted_iota(jnp.int32, lg.shape, 1)
    lanef = lane.astype(F32)
    valid = lane < N_EXPERTS
    big = float(LANES)
    lg = jnp.where(valid, lg, -jnp.inf)
    m1 = jnp.max(lg, axis=-1, keepdims=True)
    e1 = jnp.min(jnp.where(lg == m1, lanef, big), axis=-1, keepdims=True)
    lg2 = jnp.where(lanef == e1, -jnp.inf, lg)
    m2 = jnp.max(lg2, axis=-1, keepdims=True)
    e2 = jnp.min(jnp.where(lg2 == m2, lanef, big), axis=-1, keepdims=True)
    z = jnp.exp(m2 - m1)
    w1 = 1.0 / (1.0 + z)
    w2 = z / (1.0 + z)
    oh1 = (lanef == e1).astype(F32)
    oh2 = (lanef == e2).astype(F32)
    both = oh1 + oh2
    before = _dot(tri_ref[...], both.astype(BF16)) + carry_ref[0:1, :]
    r1 = jnp.sum(before * oh1, axis=-1, keepdims=True)
    r2 = jnp.sum(before * oh2, axis=-1, keepdims=True)
    carry_ref[0:1, :] = carry_ref[0:1, :] + jnp.sum(both, axis=0, keepdims=True)
    count_ref[...] = carry_ref[...]
    cols = (e1, e2, r1, r2, w1, w2)
    info = jnp.zeros(lg.shape, F32)
    for idx, col in enumerate(cols):
        info = jnp.where(lane == idx, col, info)
    info_ref[...] = info


def _route(logits):
    t = logits.shape[0]
    tm = _tile(t, 512)
    tri = jnp.asarray(np.tril(np.ones((tm, tm), np.float32), -1), BF16)
    info, counts = pl.pallas_call(
        _route_kernel,
        grid=(t // tm,),
        in_specs=[pl.BlockSpec((tm, LANES), lambda i: (i, 0)), pl.BlockSpec((tm, tm), lambda i: (0, 0))],
        out_specs=[pl.BlockSpec((tm, LANES), lambda i: (i, 0)), pl.BlockSpec((8, LANES), lambda i: (0, 0))],
        out_shape=[jax.ShapeDtypeStruct((t, LANES), F32), jax.ShapeDtypeStruct((8, LANES), F32)],
        scratch_shapes=[pltpu.VMEM((8, LANES), F32)],
        compiler_params=_cparams(("arbitrary",)),
        name="moe_route",
    )(logits, tri)
    return info, counts[0, :N_EXPERTS]


def _dispatch_kernel(d1_ref, d2_ref, h_ref, xin_ref, xbuf_ref, sem, *, tm):
    del xin_ref

    def row_copy(r, dref):
        return pltpu.make_async_copy(h_ref.at[pl.ds(r, 1), :], xbuf_ref.at[pl.ds(dref[0, 0, r], 1), :], sem)

    def start(g, carry):
        for j in range(DMA_UNROLL):
            r = g * DMA_UNROLL + j
            row_copy(r, d1_ref).start(priority=0)
            row_copy(r, d2_ref).start(priority=1)
        return carry

    lax.fori_loop(0, tm // DMA_UNROLL, start, 0)
    for _ in range(TOP_K):
        pltpu.make_async_copy(h_ref, xbuf_ref.at[pl.ds(0, tm), :], sem).wait()


def _dispatch(h, dest1, dest2, n_rows):
    t, d = h.shape
    tm = _tile(t, 512)
    nt = t // tm
    smem = lambda: pl.BlockSpec((1, 1, tm), lambda i: (i, 0, 0), memory_space=pltpu.SMEM)
    return pl.pallas_call(
        functools.partial(_dispatch_kernel, tm=tm),
        grid=(nt,),
        in_specs=[smem(), smem(), pl.BlockSpec((tm, d), lambda i: (i, 0)), pl.BlockSpec(memory_space=pl.ANY)],
        out_specs=pl.BlockSpec(memory_space=pl.ANY),
        out_shape=jax.ShapeDtypeStruct((n_rows, d), h.dtype),
        scratch_shapes=[pltpu.SemaphoreType.DMA(())],
        input_output_aliases={3: 0},
        compiler_params=_cparams(("arbitrary",)),
        name="moe_dispatch",
    )(dest1.reshape(nt, 1, tm), dest2.reshape(nt, 1, tm), h, jnp.zeros((n_rows, d), h.dtype))


def _moe_kernel(be_ref, nb_ref, x_ref, wg_ref, wu_ref, wd_ref, o_ref, xb_ref, acc_ref):
    i = pl.program_id(0)
    f = pl.program_id(1)
    active = i < nb_ref[0]

    @pl.when(jnp.logical_and(active, f == 0))
    def _():
        xb_ref[...] = x_ref[...].astype(BF16)
        acc_ref[...] = jnp.zeros_like(acc_ref)

    @pl.when(active)
    def _():
        xb = xb_ref[...]
        a = _silu(_dot(xb, wg_ref[...].astype(BF16))) * _dot(xb, wu_ref[...].astype(BF16))
        acc_ref[...] += _dot(a.astype(BF16), wd_ref[...].astype(BF16))

    @pl.when(f == pl.num_programs(1) - 1)
    def _():
        o_ref[...] = jnp.where(active, acc_ref[...], 0.0)


def _moe_ffn(x_buf, block_expert, n_active, wg, wu, wd):
    n_rows, d = x_buf.shape
    ff = wg.shape[2]
    tm = MOE_TILE
    tf = _tile(ff, FF_TILE)
    nf = ff // tf

    def wmap(i, f, be, nb):
        return be[i], jnp.where(i < nb[0], f, nf - 1)

    grid_spec = pltpu.PrefetchScalarGridSpec(
        num_scalar_prefetch=2,
        grid=(n_rows // tm, nf),
        in_specs=[pl.BlockSpec((tm, d), lambda i, f, be, nb: (jnp.minimum(i, nb[0] - 1), 0)),
                  pl.BlockSpec((None, d, tf), lambda i, f, be, nb: (wmap(i, f, be, nb)[0], 0, wmap(i, f, be, nb)[1])),
                  pl.BlockSpec((None, d, tf), lambda i, f, be, nb: (wmap(i, f, be, nb)[0], 0, wmap(i, f, be, nb)[1])),
                  pl.BlockSpec((None, tf, d), lambda i, f, be, nb: (wmap(i, f, be, nb)[0], wmap(i, f, be, nb)[1], 0))],
        out_specs=pl.BlockSpec((tm, d), lambda i, f, be, nb: (i, 0)),
        scratch_shapes=[pltpu.VMEM((tm, d), BF16), pltpu.VMEM((tm, d), F32)],
    )
    return pl.pallas_call(
        _moe_kernel,
        grid_spec=grid_spec,
        out_shape=jax.ShapeDtypeStruct((n_rows, d), F32),
        compiler_params=_cparams(("arbitrary", "arbitrary")),
        name="moe_grouped_swiglu",
    )(block_expert, n_active, x_buf, wg, wu, wd)


def _combine_kernel(d1_ref, d2_ref, d1n_ref, d2n_ref, ybuf_ref, info_ref, x_ref, g_ref, mod_ref, o_ref,
                    y1_ref, y2_ref, sem, *, tm):
    i = pl.program_id(0)
    slot = i % 2

    def gather(da_ref, db_ref, s):
        def start(g, carry):
            for j in range(DMA_UNROLL):
                r = g * DMA_UNROLL + j
                pltpu.make_async_copy(ybuf_ref.at[pl.ds(da_ref[0, 0, r], 1), :], y1_ref.at[s, pl.ds(r, 1), :],
                                      sem.at[s]).start(priority=0)
                pltpu.make_async_copy(ybuf_ref.at[pl.ds(db_ref[0, 0, r], 1), :], y2_ref.at[s, pl.ds(r, 1), :],
                                      sem.at[s]).start(priority=1)
            return carry

        lax.fori_loop(0, tm // DMA_UNROLL, start, 0)

    @pl.when(i == 0)
    def _():
        gather(d1_ref, d2_ref, 0)

    @pl.when(i + 1 < pl.num_programs(0))
    def _():
        gather(d1n_ref, d2n_ref, 1 - slot)

    for dst in (y1_ref, y2_ref):
        pltpu.make_async_copy(ybuf_ref.at[pl.ds(0, tm), :], dst.at[slot], sem.at[slot]).wait()
    d = D_MODEL
    info = info_ref[...]
    y = y1_ref[slot] * info[:, 4:5] + y2_ref[slot] * info[:, 5:6]
    o_ref[...] = x_ref[...] + mod_ref[:, 5 * d:6 * d] * _rms(y, g_ref[...])


def _combine(y_buf, dest1, dest2, info, x2, g_post, mod_l, seq):
    t, d = x2.shape
    tm = _tile(seq, 512)
    nt = t // tm
    tpb = seq // tm
    smem = lambda: pl.BlockSpec((1, 1, tm), lambda i: (i, 0, 0), memory_space=pltpu.SMEM)
    smem_next = lambda: pl.BlockSpec((1, 1, tm), lambda i: (jnp.minimum(i + 1, nt - 1), 0, 0),
                                     memory_space=pltpu.SMEM)
    dest1, dest2 = dest1.reshape(nt, 1, tm), dest2.reshape(nt, 1, tm)
    return pl.pallas_call(
        functools.partial(_combine_kernel, tm=tm),
        grid=(nt,),
        in_specs=[smem(), smem(), smem_next(), smem_next(), pl.BlockSpec(memory_space=pl.ANY),
                  pl.BlockSpec((tm, LANES), lambda i: (i, 0)),
                  pl.BlockSpec((tm, d), lambda i: (i, 0)),
                  pl.BlockSpec((1, d), lambda i: (0, 0)),
                  pl.BlockSpec((None, 1, 6 * d), lambda i: (i // tpb, 0, 0))],
        out_specs=pl.BlockSpec((tm, d), lambda i: (i, 0)),
        out_shape=jax.ShapeDtypeStruct((t, d), F32),
        scratch_shapes=[pltpu.VMEM((2, tm, d), F32), pltpu.VMEM((2, tm, d), F32), pltpu.SemaphoreType.DMA((2,))],
        compiler_params=_cparams(("arbitrary",)),
        name="moe_combine",
    )(dest1, dest2, dest1, dest2, y_buf, info, x2, g_post.reshape(1, d), mod_l)


def _routed_ffn(h, logits, wg, wu, wd, x2, g_post, mod_l, seq):
    t, d = h.shape
    tm = MOE_TILE
    info, counts = _route(logits)
    counts = counts.astype(jnp.int32)
    padded = ((counts + tm - 1) // tm) * tm
    pad_end = jnp.cumsum(padded)
    pad_start = pad_end - padded
    e1 = info[:, 0].astype(jnp.int32)
    e2 = info[:, 1].astype(jnp.int32)
    dest1 = pad_start[e1] + info[:, 2].astype(jnp.int32)
    dest2 = pad_start[e2] + info[:, 3].astype(jnp.int32)
    n_rows = t * TOP_K + N_EXPERTS * tm
    n_blocks = n_rows // tm
    block_start = jnp.arange(n_blocks, dtype=jnp.int32) * tm
    block_expert = jnp.minimum(jnp.sum(pad_end[None, :] <= block_start[:, None], axis=1), N_EXPERTS - 1).astype(jnp.int32)
    n_active = (pad_end[-1] // tm).astype(jnp.int32).reshape(1)
    x_buf = _dispatch(h, dest1, dest2, n_rows)
    y_buf = _moe_ffn(x_buf, block_expert, n_active, wg, wu, wd)
    return _combine(y_buf, dest1, dest2, info, x2, g_post, mod_l, seq)


def kernel(x, c, positions, ada_w, ada_b, norm_pre_mix, norm_post_mix, norm_pre_ffn, norm_post_ffn, w_in, ret_norm, ssm_a_re, ssm_a_im, ssm_b_re, ssm_b_im, ssm_c_re, ssm_c_im, ssm_d, ssm_log_dt, ssm_glu_w, ssm_glu_b, ssm_norm, mla_q_norm, mla_w_uq, mla_kv_norm, mla_w_ukv, mla_norm, w_out, ffn_w_gate, ffn_w_up, ffn_w_down, moe_router, moe_w_gate, moe_w_up, moe_w_down):
    bsz, seq, d = x.shape
    depth = ada_w.shape[0]
    assert d == D_MODEL and seq % SSM_CHUNK == 0 and seq % RET_CHUNK == 0
    t = bsz * seq
    x2 = x.reshape(t, d)
    mod = _modulation(c, ada_w, ada_b)
    cos128, sin128 = _rope_tables(positions)
    ssm_params = _ssm_params(ssm_a_re, ssm_a_im, ssm_b_re, ssm_b_im, ssm_c_re, ssm_c_im, ssm_log_dt)
    for layer in range(depth):
        mod_l = mod[layer]
        r, u, cq, ckv, kr = _inproj(x2, norm_pre_mix[layer], mod_l, w_in[layer], seq)
        ret = _retention(r, cos128, sin128, ret_norm[layer], bsz, seq)
        ssm = _ssm_post(_ssm_scan(u, ssm_params, layer, seq), u, ssm_d[layer], ssm_glu_w[layer], ssm_glu_b[layer],
                        ssm_norm[layer])
        q3, k3, v3 = _mla_prep(cq, ckv, kr, cos128, sin128, mla_q_norm[layer], mla_w_uq[layer],
                               mla_kv_norm[layer], mla_w_ukv[layer])
        att = _attention(q3, k3, v3, bsz, seq)
        j = layer // 2
        if layer % 2 == 0:
            x2, h = _outproj(x2, ret, ssm, att, mla_norm[layer], w_out[layer], norm_post_mix[layer], mod_l,
                             norm_pre_ffn[layer], seq)
            x2 = _dense_ffn(h, ffn_w_gate[j], ffn_w_up[j], ffn_w_down[j], x2, norm_post_ffn[layer], mod_l, seq)
        else:
            x2, h, logits = _outproj(x2, ret, ssm, att, mla_norm[layer], w_out[layer], norm_post_mix[layer], mod_l,
                                     norm_pre_ffn[layer], seq, router=moe_router[j])
            x2 = _routed_ffn(h, logits, moe_w_gate[j], moe_w_up[j], moe_w_down[j], x2, norm_post_ffn[layer],
                             mod_l, seq)
    return x2.reshape(bsz, seq, d)
```

```python
import functools
import math

import numpy as np
import jax
import jax.numpy as jnp
from jax import lax
from jax.experimental import pallas as pl
from jax.experimental.pallas import tpu as pltpu

D_MODEL = 1024
RET_HEADS = 4
RET_HEAD_DIM = 64
RET_WIDTH = RET_HEADS * RET_HEAD_DIM
RET_CHUNK = 128
SSM_GROUP_CH = 16
SSM_GROUPS = 16
SSM_WIDTH = SSM_GROUPS * SSM_GROUP_CH
SSM_STATE = 64
MLA_HEADS = 4
MLA_NOPE = 128
MLA_ROPE = 64
MLA_V = 128
MLA_QK = MLA_NOPE + MLA_ROPE
MLA_WIDTH = MLA_HEADS * MLA_V
MLA_Q_RANK = 256
MLA_KV_RANK = 128
ROPE_DIM = 64
ROPE_BASE = 10000.0
D_FF = 3584
N_EXPERTS = 8
TOP_K = 2
EPS = 1e-6
NEG_INF = -1e30
IN_COLS = 4 * RET_WIDTH + SSM_WIDTH + MLA_Q_RANK + MLA_KV_RANK + MLA_ROPE

LANES = 128
VMEM_LIMIT_BYTES = 56 * 1024 * 1024

SSM_CHUNK = 64
MOE_TILE = 1024
FF_TILE = 512
DMA_UNROLL = 8

BF16 = jnp.bfloat16
F32 = jnp.float32


def _cparams(sem):
    return pltpu.CompilerParams(dimension_semantics=sem, vmem_limit_bytes=VMEM_LIMIT_BYTES)


def _tile(n, pref):
    t = min(n, pref)
    while n % t:
        t //= 2
    return t


def _dot(a, b):
    return jnp.dot(a, b, preferred_element_type=F32)


def _dot3(a, b):
    a_hi = a.astype(BF16)
    a_lo = (a - a_hi.astype(F32)).astype(BF16)
    b_hi = b.astype(BF16)
    b_lo = (b - b_hi.astype(F32)).astype(BF16)
    return _dot(a_hi, b_hi) + (_dot(a_hi, b_lo) + _dot(a_lo, b_hi))


def _rms(x, g):
    return x * lax.rsqrt(jnp.mean(x * x, axis=-1, keepdims=True) + EPS) * g


def _silu(x):
    return x * (1.0 / (1.0 + jnp.exp(-x)))


def _rope128(x, cos, sin_signed):
    lane = lax.broadcasted_iota(jnp.int32, x.shape, 1)
    first_half = (lane % ROPE_DIM) < (ROPE_DIM // 2)
    partner = jnp.where(first_half, pltpu.roll(x, LANES - ROPE_DIM // 2, 1), pltpu.roll(x, ROPE_DIM // 2, 1))
    return x * cos + partner * sin_signed


def _mod_kernel(c_ref, w_ref, b_ref, o_ref):
    cond = _silu(c_ref[...])
    o_ref[...] = _dot3(cond, w_ref[...]) + b_ref[...]


def _modulation(c, ada_w, ada_b):
    depth, d, n = ada_w.shape
    bsz = c.shape[0]
    rows = 8
    c_pad = jnp.zeros((rows, d), F32).at[:bsz].set(c)
    tn = _tile(n, 1536)
    out = pl.pallas_call(
        _mod_kernel,
        grid=(depth, n // tn),
        in_specs=[pl.BlockSpec((rows, d), lambda l, j: (0, 0)),
                  pl.BlockSpec((None, d, tn), lambda l, j: (l, 0, j)),
                  pl.BlockSpec((None, 1, tn), lambda l, j: (l, 0, j))],
        out_specs=pl.BlockSpec((None, rows, tn), lambda l, j: (l, 0, j)),
        out_shape=jax.ShapeDtypeStruct((depth, rows, n), F32),
        compiler_params=_cparams(("arbitrary", "arbitrary")),
        name="adaln_mod",
    )(c_pad, ada_w, ada_b.reshape(depth, 1, n))
    return out[:, :bsz].reshape(depth, bsz, 1, n)


def _rope_kernel(pos_ref, inv_ref, cos_ref, sin_ref):
    pos = pos_ref[...].astype(F32)
    lane = lax.broadcasted_iota(jnp.int32, cos_ref.shape, 1)
    half = ROPE_DIM // 2
    p = jnp.where(lane < half, pos[:, 0:1],
                  jnp.where(lane < 2 * half, pos[:, 1:2],
                            jnp.where(lane < 3 * half, pos[:, 2:3], pos[:, 3:4])))
    ang = p * inv_ref[...]
    cos_ref[...] = jnp.cos(ang)
    sin_ref[...] = jnp.sin(ang)


def _rope_tables(positions):
    t = positions.size
    half = ROPE_DIM // 2
    per_row = LANES // half
    inv = ROPE_BASE ** (-jnp.arange(0, ROPE_DIM, 2, dtype=F32) / ROPE_DIM)
    inv_row = jnp.tile(inv, per_row).reshape(1, LANES)
    rows = t // per_row
    tr = _tile(rows, 1024)
    cos, sin = pl.pallas_call(
        _rope_kernel,
        grid=(rows // tr,),
        in_specs=[pl.BlockSpec((tr, per_row), lambda i: (i, 0)),
                  pl.BlockSpec((1, LANES), lambda i: (0, 0))],
        out_specs=[pl.BlockSpec((tr, LANES), lambda i: (i, 0))] * 2,
        out_shape=[jax.ShapeDtypeStruct((rows, LANES), F32)] * 2,
        compiler_params=_cparams(("arbitrary",)),
        name="rope_tables",
    )(positions.reshape(rows, per_row), inv_row)
    cos = cos.reshape(t, half)
    sin = sin.reshape(t, half)
    return jnp.tile(cos, (1, per_row)), jnp.concatenate([-sin, sin, -sin, sin], axis=1)


def _inproj_kernel(x_ref, g_ref, mod_ref, w_ref, ret_ref, u_ref, cq_ref, ckv_ref, kr_ref):
    d = D_MODEL
    x = x_ref[...]
    h = _rms(x, g_ref[...]) * (1.0 + mod_ref[:, d:2 * d]) + mod_ref[:, 0:d]
    p = _dot(h.astype(BF16), w_ref[...])
    o = 4 * RET_WIDTH
    ret_ref[...] = p[:, :o].astype(BF16)
    u_ref[...] = p[:, o:o + SSM_WIDTH].astype(BF16)
    o += SSM_WIDTH
    cq_ref[...] = p[:, o:o + MLA_Q_RANK].astype(BF16)
    o += MLA_Q_RANK
    ckv_ref[...] = p[:, o:o + MLA_KV_RANK].astype(BF16)
    o += MLA_KV_RANK
    kr_ref[...] = p[:, o:o + MLA_ROPE].astype(BF16)


def _inproj(x2, g, mod_l, w_in, seq):
    t, d = x2.shape
    tm = _tile(seq, 512)
    tpb = seq // tm
    widths = (4 * RET_WIDTH, SSM_WIDTH, MLA_Q_RANK, MLA_KV_RANK, MLA_ROPE)
    return pl.pallas_call(
        _inproj_kernel,
        grid=(t // tm,),
        in_specs=[pl.BlockSpec((tm, d), lambda i: (i, 0)),
                  pl.BlockSpec((1, d), lambda i: (0, 0)),
                  pl.BlockSpec((None, 1, 6 * d), lambda i: (i // tpb, 0, 0)),
                  pl.BlockSpec((d, IN_COLS), lambda i: (0, 0))],
        out_specs=[pl.BlockSpec((tm, w), lambda i: (i, 0)) for w in widths],
        out_shape=[jax.ShapeDtypeStruct((t, w), BF16) for w in widths],
        compiler_params=_cparams(("arbitrary",)),
        name="mixer_inproj",
    )(x2, g.reshape(1, d), mod_l, w_in.astype(BF16))


def _retention_kernel(r_ref, cos_ref, sin_ref, dmask_ref, qdec_ref, kdec_ref, cdec_ref, bd_ref, gn_ref,
                      o_ref, state_ref, *, chunks):
    w = RET_WIDTH

    @pl.when(pl.program_id(1) == 0)
    def _():
        state_ref[...] = jnp.zeros_like(state_ref)

    lane = lax.broadcasted_iota(jnp.int32, (RET_CHUNK, w), 1)
    bd = bd_ref[...]
    for c in range(chunks):
        rows = pl.ds(c * RET_CHUNK, RET_CHUNK)
        cos = cos_ref[rows, :]
        sin = sin_ref[rows, :]

        def rope(v):
            return jnp.concatenate([_rope128(v[:, :LANES], cos, sin), _rope128(v[:, LANES:], cos, sin)], axis=1)

        q = rope(r_ref[rows, 0:w].astype(F32))
        k = rope(r_ref[rows, w:2 * w].astype(F32)) * (RET_HEAD_DIM ** -0.5)
        v = r_ref[rows, 2 * w:3 * w]
        gate = r_ref[rows, 3 * w:4 * w].astype(F32)
        kb = k.astype(BF16)
        state = state_ref[...]
        y = _dot((q * qdec_ref[...]).astype(BF16), state.astype(BF16))
        for h in range(RET_HEADS):
            in_head = (lane // RET_HEAD_DIM) == h
            qh = jnp.where(in_head, q, 0.0).astype(BF16)
            s = lax.dot_general(qh, kb, (((1,), (1,)), ((), ())), preferred_element_type=F32)
            s = s * dmask_ref[h]
            y = y + jnp.where(in_head, _dot(s.astype(BF16), v), 0.0)
        kv = lax.dot_general((k * kdec_ref[...]).astype(BF16), v, (((0,), (0,)), ((), ())),
                             preferred_element_type=F32)
        state_ref[...] = cdec_ref[...] * state + bd * kv
        y2 = y * y
        y2_hi = y2.astype(BF16)
        y2_lo = (y2 - y2_hi.astype(F32)).astype(BF16)
        bdb = bd.astype(BF16)
        ms = (_dot(y2_hi, bdb) + _dot(y2_lo, bdb)) * (1.0 / RET_HEAD_DIM)
        yn = y * lax.rsqrt(ms + EPS) * gn_ref[...]
        o_ref[rows, :] = (_silu(gate) * yn).astype(BF16)


def _retention_consts():
    c = RET_CHUNK
    log_g = np.log1p(-(2.0 ** (-5.0 - np.arange(RET_HEADS, dtype=np.float32)))).astype(np.float32)
    i = np.arange(c, dtype=np.float32)
    diff = i[:, None] - i[None, :]
    dmask = np.where(diff >= 0, np.exp(log_g[:, None, None] * np.maximum(diff, 0.0)), 0.0).astype(np.float32)
    k_dec = np.exp(log_g[None, :] * (c - 1 - i)[:, None]).astype(np.float32)
    q_dec = np.exp(log_g[None, :] * (i + 1.0)[:, None]).astype(np.float32)
    c_dec = np.exp(log_g * c).astype(np.float32)
    rep = lambda a: np.repeat(a, RET_HEAD_DIM, axis=-1)
    head = np.arange(RET_WIDTH) // RET_HEAD_DIM
    bd = (head[:, None] == head[None, :]).astype(np.float32)
    return dmask, rep(q_dec), rep(k_dec), rep(c_dec[None, :]), bd


def _retention(r, cos128, sin128, ret_norm, bsz, seq):
    t = r.shape[0]
    w = RET_WIDTH
    tr = _tile(seq, 512)
    chunks = tr // RET_CHUNK
    nt = seq // tr
    dmask, qdec, kdec, cdec, bd = (jnp.asarray(a) for a in _retention_consts())
    const = lambda shape: pl.BlockSpec(shape, lambda b, i: (0,) * len(shape))
    return pl.pallas_call(
        functools.partial(_retention_kernel, chunks=chunks),
        grid=(bsz, nt),
        in_specs=[pl.BlockSpec((tr, 4 * w), lambda b, i: (b * nt + i, 0)),
                  pl.BlockSpec((tr, LANES), lambda b, i: (b * nt + i, 0)),
                  pl.BlockSpec((tr, LANES), lambda b, i: (b * nt + i, 0)),
                  const((RET_HEADS, RET_CHUNK, RET_CHUNK)),
                  const((RET_CHUNK, w)), const((RET_CHUNK, w)), const((1, w)), const((w, w)), const((1, w))],
        out_specs=pl.BlockSpec((tr, w), lambda b, i: (b * nt + i, 0)),
        out_shape=jax.ShapeDtypeStruct((t, w), BF16),
        scratch_shapes=[pltpu.VMEM((w, w), F32)],
        compiler_params=_cparams(("arbitrary", "arbitrary")),
        name="retention",
    )(r, cos128, sin128, dmask, qdec, kdec, cdec, bd, ret_norm.reshape(1, w))


def _expand(x, sel):
    hi = x.astype(BF16)
    lo = (x - hi.astype(F32)).astype(BF16)
    return _dot(hi, sel) + _dot(lo, sel)


def _ssm_kernel(u_ref, acol_ref, arow_ref, ldt_ref, b_ref, bt_ref, ct_ref, esel_ref, ech_ref, y_ref, toep_ref,
                *, chunks_per_seq, steps):
    tc, hg, p = SSM_CHUNK, SSM_GROUP_CH, SSM_STATE
    k = tc * hg
    dt = jnp.exp(ldt_ref[...])
    ar, ai = acol_ref[:, 0:1], acol_ref[:, 1:2]
    n = lax.broadcasted_iota(jnp.int32, (p, LANES), 1).astype(F32)
    mag = jnp.exp(ar * dt * n)
    pw_re, pw_im = mag * jnp.cos(ai * dt * n), mag * jnp.sin(ai * dt * n)
    def zoh(a_r, a_i):
        m1 = jnp.exp(a_r * dt)
        l_r, l_i = m1 * jnp.cos(a_i * dt), m1 * jnp.sin(a_i * dt)
        den = a_r * a_r + a_i * a_i
        return ((l_r - 1.0) * a_r + l_i * a_i) / den, (l_i * a_r - (l_r - 1.0) * a_i) / den

    f_re, f_im = zoh(ar, ai)
    bb_re = f_re * b_ref[0] - f_im * b_ref[1]
    bb_im = f_re * b_ref[1] + f_im * b_ref[0]
    fr_re, fr_im = zoh(arow_ref[0:1, 0:p], arow_ref[1:2, 0:p])
    bbt_re = fr_re * bt_ref[0] - fr_im * bt_ref[1]
    bbt_im = fr_re * bt_ref[1] + fr_im * bt_ref[0]
    ech = ech_ref[...]
    pt_re, pt_im = _expand(pw_re, esel_ref[0]), _expand(pw_im, esel_ref[0])
    p1_re, p1_im = _expand(pw_re, esel_ref[1]), _expand(pw_im, esel_ref[1])
    pr_re, pr_im = _expand(pw_re, esel_ref[2]), _expand(pw_im, esel_ref[2])
    bbr, bbi = _expand(bb_re, ech), _expand(bb_im, ech)
    cr, ci = _expand(ct_ref[0], ech), _expand(ct_ref[1], ech)
    z_re = cr * pt_re - ci * pt_im
    z_im = cr * pt_im + ci * pt_re
    strip = _dot3(bbt_re, z_re) - _dot3(bbt_im, z_im)
    w_in_t = jnp.concatenate([pr_re * bbr - pr_im * bbi, pr_re * bbi + pr_im * bbr], axis=0).astype(BF16)
    v_out = jnp.concatenate([cr * p1_re - ci * p1_im, -(cr * p1_im + ci * p1_re)], axis=0).astype(BF16)
    padded = jnp.concatenate([jnp.zeros((hg, k), F32), strip], axis=1)
    for s in range(tc):
        toep_ref[s * hg:(s + 1) * hg, :] = padded[:, k - s * hg:2 * k - s * hg].astype(BF16)
    dist = (tc * (1 << lax.broadcasted_iota(jnp.int32, (steps, 2 * p), 0))).astype(F32)
    lane = lax.broadcasted_iota(jnp.int32, (steps, 2 * p), 1)
    amag = jnp.exp(arow_ref[0:1, :] * dt * dist)
    a_cos = amag * jnp.cos(arow_ref[1:2, :] * dt * dist)
    a_sin = jnp.where(lane < p, -1.0, 1.0) * amag * jnp.sin(arow_ref[1:2, :] * dt * dist)

    u = u_ref[...]
    x = lax.dot_general(u, w_in_t, (((1,), (1,)), ((), ())), preferred_element_type=F32)
    row = lax.broadcasted_iota(jnp.int32, x.shape, 0) % chunks_per_seq

    def shift_rows(a, d):
        return jnp.where(row >= d, pltpu.roll(a, d, 0), 0.0)

    for j in range(steps):
        prev = shift_rows(x, 1 << j)
        x = x + prev * a_cos[j:j + 1, :] + pltpu.roll(prev, p, 1) * a_sin[j:j + 1, :]
    x_in = shift_rows(x, 1)
    y_ref[...] = (_dot(u, toep_ref[...]) + _dot(x_in.astype(BF16), v_out)).astype(y_ref.dtype)


def _ssm_params(a_re, a_im, b_re, b_im, c_re, c_im, log_dt):
    a_col = jnp.stack([a_re, a_im], axis=-1)
    a_row = jnp.stack([jnp.concatenate([a_re, a_re], -1), jnp.concatenate([a_im, a_im], -1)], axis=2)
    b = jnp.stack([b_re, b_im], axis=2)
    bt = jnp.swapaxes(b, -1, -2)
    ct = jnp.swapaxes(jnp.stack([c_re, c_im], axis=2), -1, -2)
    return a_col, a_row, log_dt[..., None, None], b, bt, ct


def _ssm_selectors():
    tc, hg = SSM_CHUNK, SSM_GROUP_CH
    t_of_lane = np.arange(tc * hg) // hg
    n = np.arange(LANES)[:, None]
    esel = np.stack([n == t_of_lane, n == t_of_lane + 1, n == tc - 1 - t_of_lane]).astype(np.float32)
    ech = (np.arange(hg)[:, None] == (np.arange(tc * hg) % hg)[None, :]).astype(np.float32)
    return jnp.asarray(esel, BF16), jnp.asarray(ech, BF16)


def _ssm_scan(u, params, layer, seq):
    t = u.shape[0]
    g, hg, tc, p = SSM_GROUPS, SSM_GROUP_CH, SSM_CHUNK, SSM_STATE
    nc = t // tc
    cps = seq // tc
    steps = max(1, int(math.ceil(math.log2(cps))))
    k = tc * hg
    esel, ech = _ssm_selectors()
    ug = u.reshape(nc, tc, g, hg).transpose(2, 0, 1, 3).reshape(g, nc, k)
    per_group = lambda *blk: pl.BlockSpec((None, None) + blk, lambda i: (layer, i) + (0,) * len(blk))
    y = pl.pallas_call(
        functools.partial(_ssm_kernel, chunks_per_seq=cps, steps=steps),
        grid=(g,),
        in_specs=[pl.BlockSpec((None, nc, k), lambda i: (i, 0, 0)),
                  per_group(p, 2), per_group(2, 2 * p), per_group(1, 1),
                  per_group(2, p, hg), per_group(2, hg, p), per_group(2, p, hg),
                  pl.BlockSpec((3, LANES, k), lambda i: (0, 0, 0)),
                  pl.BlockSpec((hg, k), lambda i: (0, 0))],
        out_specs=pl.BlockSpec((None, nc, k), lambda i: (i, 0, 0)),
        out_shape=jax.ShapeDtypeStruct((g, nc, k), BF16),
        scratch_shapes=[pltpu.VMEM((k, k), BF16)],
        compiler_params=_cparams(("arbitrary",)),
        name="s5_chunk_scan",
    )(ug, *params, esel, ech)
    return y.reshape(g, nc, tc, hg).transpose(1, 2, 0, 3).reshape(t, g * hg)


def _ssm_post_kernel(y_ref, u_ref, d_ref, gw_ref, gb_ref, gn_ref, o_ref):
    y = y_ref[...].astype(F32) + d_ref[...] * u_ref[...].astype(F32)
    y = 0.5 * y * (1.0 + jnp.tanh(math.sqrt(2.0 / math.pi) * (y + 0.044715 * (y * y * y))))
    z = _dot(y.astype(BF16), gw_ref[...]) + gb_ref[...]
    y = y * (1.0 / (1.0 + jnp.exp(-z)))
    o_ref[...] = _rms(y, gn_ref[...]).astype(BF16)


def _ssm_post(y, u, d, glu_w, glu_b, ssm_norm):
    t, w = y.shape
    tm = _tile(t, 1024)
    row = lambda: pl.BlockSpec((1, w), lambda i: (0, 0))
    return pl.pallas_call(
        _ssm_post_kernel,
        grid=(t // tm,),
        in_specs=[pl.BlockSpec((tm, w), lambda i: (i, 0)), pl.BlockSpec((tm, w), lambda i: (i, 0)),
                  row(), pl.BlockSpec((w, w), lambda i: (0, 0)), row(), row()],
        out_specs=pl.BlockSpec((tm, w), lambda i: (i, 0)),
        out_shape=jax.ShapeDtypeStruct((t, w), BF16),
        compiler_params=_cparams(("arbitrary",)),
        name="s5_gelu_glu_norm",
    )(y, u, d.reshape(1, w), glu_w.astype(BF16), glu_b.reshape(1, w), ssm_norm.reshape(1, w))


def _mla_prep_kernel(cq_ref, ckv_ref, kr_ref, cos_ref, sin_ref, qn_ref, kvn_ref, wq_ref, wkv_ref,
                     q_ref, k_ref, v_ref):
    cos = cos_ref[...]
    sin = sin_ref[...]
    nope_w = MLA_HEADS * MLA_NOPE
    q = _dot(_rms(cq_ref[...].astype(F32), qn_ref[...]).astype(BF16), wq_ref[...])
    q = q * (MLA_QK ** -0.5 * math.log2(math.e))
    q_rope = jnp.concatenate([_rope128(q[:, nope_w:nope_w + LANES], cos, sin),
                              _rope128(q[:, nope_w + LANES:], cos, sin)], axis=1)
    kv = _dot(_rms(ckv_ref[...].astype(F32), kvn_ref[...]).astype(BF16), wkv_ref[...])
    kr = kr_ref[...].astype(F32)
    k_pe = _rope128(jnp.concatenate([kr, kr], axis=1), cos, sin)[:, :MLA_ROPE]
    for h in range(MLA_HEADS):
        q_ref[h] = jnp.concatenate([q[:, h * MLA_NOPE:(h + 1) * MLA_NOPE],
                                    q_rope[:, h * MLA_ROPE:(h + 1) * MLA_ROPE]], axis=1).astype(BF16)
        k_ref[h] = jnp.concatenate([kv[:, h * MLA_NOPE:(h + 1) * MLA_NOPE], k_pe], axis=1).astype(BF16)
        v_ref[h, :, 0:MLA_V] = kv[:, nope_w + h * MLA_V:nope_w + (h + 1) * MLA_V].astype(BF16)
        v_ref[h, :, MLA_V:] = jnp.ones((kv.shape[0], LANES), BF16)


def _mla_prep(cq, ckv, kr, cos128, sin128, q_norm, w_uq, kv_norm, w_ukv):
    t = cq.shape[0]
    tm = _tile(t, 512)
    hh = MLA_HEADS
    wq = w_uq.reshape(MLA_Q_RANK, hh, MLA_QK)
    wq = jnp.concatenate([wq[:, :, :MLA_NOPE].reshape(MLA_Q_RANK, -1), wq[:, :, MLA_NOPE:].reshape(MLA_Q_RANK, -1)], 1)
    wkv = w_ukv.reshape(MLA_KV_RANK, hh, MLA_NOPE + MLA_V)
    wkv = jnp.concatenate([wkv[:, :, :MLA_NOPE].reshape(MLA_KV_RANK, -1), wkv[:, :, MLA_NOPE:].reshape(MLA_KV_RANK, -1)], 1)
    tok = lambda w: pl.BlockSpec((tm, w), lambda i: (i, 0))
    full = lambda a, b: pl.BlockSpec((a, b), lambda i: (0, 0))
    return pl.pallas_call(
        _mla_prep_kernel,
        grid=(t // tm,),
        in_specs=[tok(MLA_Q_RANK), tok(MLA_KV_RANK), tok(MLA_ROPE), tok(LANES), tok(LANES),
                  full(1, MLA_Q_RANK), full(1, MLA_KV_RANK),
                  full(MLA_Q_RANK, hh * MLA_QK), full(MLA_KV_RANK, hh * (MLA_NOPE + MLA_V))],
        out_specs=[pl.BlockSpec((hh, tm, MLA_QK), lambda i: (0, i, 0)),
                   pl.BlockSpec((hh, tm, MLA_QK), lambda i: (0, i, 0)),
                   pl.BlockSpec((hh, tm, MLA_V + LANES), lambda i: (0, i, 0))],
        out_shape=[jax.ShapeDtypeStruct((hh, t, MLA_QK), BF16), jax.ShapeDtypeStruct((hh, t, MLA_QK), BF16),
                   jax.ShapeDtypeStruct((hh, t, MLA_V + LANES), BF16)],
        compiler_params=_cparams(("arbitrary",)),
        name="mla_prep",
    )(cq, ckv, kr, cos128, sin128, q_norm.reshape(1, -1), kv_norm.reshape(1, -1), wq.astype(BF16), wkv.astype(BF16))


def _attn_kernel(q_ref, k_ref, v_ref, o_ref, m_ref, l_ref, acc_ref, sa_ref, sb_ref, *, tq, th, tk):
    qi = pl.program_id(2)
    m_ref[...] = jnp.full_like(m_ref, NEG_INF)
    l_ref[...] = jnp.zeros_like(l_ref)
    acc_ref[...] = jnp.zeros_like(acc_ref)

    halves = (0, 1)
    assert tq == 2 * th and tk == th

    def scores(s_ref, j, which=halves):
        k = k_ref[pl.ds(pl.multiple_of(j * tk, tk), tk), :]
        for half in which:
            rows = pl.ds(half * th, th)
            s_ref[rows, :] = lax.dot_general(q_ref[rows, :], k, (((1,), (1,)), ((), ())),
                                             preferred_element_type=F32)

    def consume(s_ref, j, diag_half=None):
        v = v_ref[pl.ds(pl.multiple_of(j * tk, tk), tk), :]
        for half in halves:
            if diag_half is not None and half < diag_half:
                continue
            rows = pl.ds(half * th, th)
            s = s_ref[rows, :]
            if half == diag_half:
                r = lax.broadcasted_iota(jnp.int32, s.shape, 0)
                c = lax.broadcasted_iota(jnp.int32, s.shape, 1)
                s = jnp.where(c <= r, s, NEG_INF)
            m_prev = m_ref[rows, :]
            m_new = jnp.maximum(m_prev, jnp.max(s, axis=-1, keepdims=True))
            alpha = jnp.exp2(m_prev - m_new)
            p = jnp.exp2(s - jnp.tile(m_new, (1, tk // LANES)))
            pv = _dot(p.astype(BF16), v)
            acc_ref[rows, :] = alpha * acc_ref[rows, :] + pv[:, :MLA_V]
            l_ref[rows, :] = alpha * l_ref[rows, :] + pv[:, MLA_V:]
            m_ref[rows, :] = m_new

    scores(sa_ref, 0)

    def pair(i, carry):
        scores(sb_ref, 2 * i + 1)
        consume(sa_ref, 2 * i)
        scores(sa_ref, 2 * i + 2)
        consume(sb_ref, 2 * i + 1)
        return carry

    def two_pairs(i2, carry):
        pair(2 * i2, carry)
        return pair(2 * i2 + 1, carry)

    lax.fori_loop(0, qi // 2, two_pairs, 0)

    @pl.when(qi % 2 == 1)
    def _():
        pair(qi - 1, 0)

    scores(sb_ref, 2 * qi + 1, which=(1,))
    consume(sa_ref, 2 * qi, diag_half=0)
    consume(sb_ref, 2 * qi + 1, diag_half=1)
    o_ref[...] = (acc_ref[...] / l_ref[...]).astype(o_ref.dtype)


def _attention(q3, k3, v3, bsz, seq):
    hh, t, _ = q3.shape
    tq = _tile(seq, 1024)
    th = tq // 2
    tk = th
    nq = seq // tq
    vw = v3.shape[-1]
    return pl.pallas_call(
        functools.partial(_attn_kernel, tq=tq, th=th, tk=tk),
        grid=(bsz, hh, nq),
        in_specs=[pl.BlockSpec((None, tq, MLA_QK), lambda b, h, i: (h, b * nq + i, 0)),
                  pl.BlockSpec((None, seq, MLA_QK), lambda b, h, i: (h, b, 0)),
                  pl.BlockSpec((None, seq, vw), lambda b, h, i: (h, b, 0))],
        out_specs=pl.BlockSpec((tq, MLA_V), lambda b, h, i: (b * nq + i, h)),
        out_shape=jax.ShapeDtypeStruct((t, hh * MLA_V), BF16),
        scratch_shapes=[pltpu.VMEM((tq, LANES), F32), pltpu.VMEM((tq, LANES), F32), pltpu.VMEM((tq, MLA_V), F32),
                        pltpu.VMEM((tq, tk), F32), pltpu.VMEM((tq, tk), F32)],
        compiler_params=_cparams(("arbitrary", "arbitrary", "arbitrary")),
        name="mla_flash_attention",
    )(q3, k3, v3)


def _outproj_kernel(*refs, routed):
    if routed:
        (x_ref, ret_ref, ssm_ref, att_ref, an_ref, w_ref, gpost_ref, mod_ref, gpre_ref, router_ref,
         xo_ref, h_ref, logit_ref) = refs
    else:
        x_ref, ret_ref, ssm_ref, att_ref, an_ref, w_ref, gpost_ref, mod_ref, gpre_ref, xo_ref, h_ref = refs
    d = D_MODEL
    att = _rms(att_ref[...].astype(F32), an_ref[...]).astype(BF16)
    o1 = RET_WIDTH
    o2 = RET_WIDTH + SSM_WIDTH
    y = _dot(ret_ref[...], w_ref[0:o1, :]) + _dot(ssm_ref[...], w_ref[o1:o2, :]) + _dot(att, w_ref[o2:, :])
    x = x_ref[...] + mod_ref[:, 2 * d:3 * d] * _rms(y, gpost_ref[...])
    xo_ref[...] = x
    h = _rms(x, gpre_ref[...]) * (1.0 + mod_ref[:, 4 * d:5 * d]) + mod_ref[:, 3 * d:4 * d]
    h_ref[...] = h.astype(h_ref.dtype)
    if routed:
        logit_ref[...] = _dot3(h, router_ref[...])


def _outproj(x2, ret, ssm, att, mla_norm, w_out, g_post, mod_l, g_pre, seq, router=None):
    t, d = x2.shape
    tm = _tile(seq, 512)
    tpb = seq // tm
    routed = router is not None
    tok = lambda w: pl.BlockSpec((tm, w), lambda i: (i, 0))
    full = lambda a, b: pl.BlockSpec((a, b), lambda i: (0, 0))
    in_specs = [tok(d), tok(RET_WIDTH), tok(SSM_WIDTH), tok(MLA_WIDTH), full(1, MLA_WIDTH), full(d, d), full(1, d),
                pl.BlockSpec((None, 1, 6 * d), lambda i: (i // tpb, 0, 0)), full(1, d)]
    args = [x2, ret, ssm, att, mla_norm.reshape(1, -1), w_out.astype(BF16), g_post.reshape(1, d), mod_l,
            g_pre.reshape(1, d)]
    out_specs = [tok(d), tok(d)]
    out_shape = [jax.ShapeDtypeStruct((t, d), F32), jax.ShapeDtypeStruct((t, d), F32 if routed else BF16)]
    if routed:
        in_specs.append(full(d, LANES))
        args.append(jnp.zeros((d, LANES), F32).at[:, :N_EXPERTS].set(router))
        out_specs.append(tok(LANES))
        out_shape.append(jax.ShapeDtypeStruct((t, LANES), F32))
    return pl.pallas_call(
        functools.partial(_outproj_kernel, routed=routed),
        grid=(t // tm,),
        in_specs=in_specs, out_specs=out_specs, out_shape=out_shape,
        compiler_params=_cparams(("arbitrary",)),
        name="mixer_outproj_routed" if routed else "mixer_outproj",
    )(*args)


def _ffn_kernel(h_ref, wg_ref, wu_ref, wd_ref, x_ref, g_ref, mod_ref, o_ref, acc_ref):
    f = pl.program_id(1)

    @pl.when(f == 0)
    def _():
        acc_ref[...] = jnp.zeros_like(acc_ref)

    h = h_ref[...]
    a = _silu(_dot(h, wg_ref[...].astype(BF16))) * _dot(h, wu_ref[...].astype(BF16))
    acc_ref[...] += _dot(a.astype(BF16), wd_ref[...].astype(BF16))

    @pl.when(f == pl.num_programs(1) - 1)
    def _():
        d = D_MODEL
        o_ref[...] = x_ref[...] + mod_ref[:, 5 * d:6 * d] * _rms(acc_ref[...], g_ref[...])


def _dense_ffn(h, wg, wu, wd, x2, g_post, mod_l, seq):
    t, d = h.shape
    ff = wg.shape[1]
    tm = _tile(seq, 1024)
    tf = _tile(ff, FF_TILE)
    tpb = seq // tm
    return pl.pallas_call(
        _ffn_kernel,
        grid=(t // tm, ff // tf),
        in_specs=[pl.BlockSpec((tm, d), lambda i, f: (i, 0)),
                  pl.BlockSpec((d, tf), lambda i, f: (0, f)),
                  pl.BlockSpec((d, tf), lambda i, f: (0, f)),
                  pl.BlockSpec((tf, d), lambda i, f: (f, 0)),
                  pl.BlockSpec((tm, d), lambda i, f: (i, 0)),
                  pl.BlockSpec((1, d), lambda i, f: (0, 0)),
                  pl.BlockSpec((None, 1, 6 * d), lambda i, f: (i // tpb, 0, 0))],
        out_specs=pl.BlockSpec((tm, d), lambda i, f: (i, 0)),
        out_shape=jax.ShapeDtypeStruct((t, d), F32),
        scratch_shapes=[pltpu.VMEM((tm, d), F32)],
        compiler_params=_cparams(("arbitrary", "arbitrary")),
        name="dense_swiglu",
    )(h, wg, wu, wd, x2, g_post.reshape(1, d), mod_l)


def _route_kernel(logit_ref, tri_ref, info_ref, count_ref, carry_ref):
    @pl.when(pl.program_id(0) == 0)
    def _():
        carry_ref[...] = jnp.zeros_like(carry_ref)

    lg = logit_ref[...]
    lane = lax.broadcasted_iota(jnp.int32, lg.shape, 1)
    lanef = lane.astype(F32)
    valid = lane < N_EXPERTS
    big = float(LANES)
    lg = jnp.where(valid, lg, -jnp.inf)
    m1 = jnp.max(lg, axis=-1, keepdims=True)
    e1 = jnp.min(jnp.where(lg == m1, lanef, big), axis=-1, keepdims=True)
    lg2 = jnp.where(lanef == e1, -jnp.inf, lg)
    m2 = jnp.max(lg2, axis=-1, keepdims=True)
    e2 = jnp.min(jnp.where(lg2 == m2, lanef, big), axis=-1, keepdims=True)
    z = jnp.exp(m2 - m1)
    w1 = 1.0 / (1.0 + z)
    w2 = z / (1.0 + z)
    oh1 = (lanef == e1).astype(F32)
    oh2 = (lanef == e2).astype(F32)
    both = oh1 + oh2
    before = _dot(tri_ref[...], both.astype(BF16)) + carry_ref[0:1, :]
    r1 = jnp.sum(before * oh1, axis=-1, keepdims=True)
    r2 = jnp.sum(before * oh2, axis=-1, keepdims=True)
    carry_ref[0:1, :] = carry_ref[0:1, :] + jnp.sum(both, axis=0, keepdims=True)
    count_ref[...] = carry_ref[...]
    cols = (e1, e2, r1, r2, w1, w2)
    info = jnp.zeros(lg.shape, F32)
    for idx, col in enumerate(cols):
        info = jnp.where(lane == idx, col, info)
    info_ref[...] = info


def _route(logits):
    t = logits.shape[0]
    tm = _tile(t, 512)
    tri = jnp.asarray(np.tril(np.ones((tm, tm), np.float32), -1), BF16)
    info, counts = pl.pallas_call(
        _route_kernel,
        grid=(t // tm,),
        in_specs=[pl.BlockSpec((tm, LANES), lambda i: (i, 0)), pl.BlockSpec((tm, tm), lambda i: (0, 0))],
        out_specs=[pl.BlockSpec((tm, LANES), lambda i: (i, 0)), pl.BlockSpec((8, LANES), lambda i: (0, 0))],
        out_shape=[jax.ShapeDtypeStruct((t, LANES), F32), jax.ShapeDtypeStruct((8, LANES), F32)],
        scratch_shapes=[pltpu.VMEM((8, LANES), F32)],
        compiler_params=_cparams(("arbitrary",)),
        name="moe_route",
    )(logits, tri)
    return info, counts[0, :N_EXPERTS]


def _dispatch_kernel(d1_ref, d2_ref, h_ref, xin_ref, xbuf_ref, sem, *, tm):
    del xin_ref

    def row_copy(r, dref):
        return pltpu.make_async_copy(h_ref.at[pl.ds(r, 1), :], xbuf_ref.at[pl.ds(dref[0, 0, r], 1), :], sem)

    def start(g, carry):
        for j in range(DMA_UNROLL):
            r = g * DMA_UNROLL + j
            row_copy(r, d1_ref).start(priority=0)
            row_copy(r, d2_ref).start(priority=1)
        return carry

    lax.fori_loop(0, tm // DMA_UNROLL, start, 0)
    for _ in range(TOP_K):
        pltpu.make_async_copy(h_ref, xbuf_ref.at[pl.ds(0, tm), :], sem).wait()


def _dispatch(h, dest1, dest2, n_rows):
    t, d = h.shape
    tm = _tile(t, 512)
    nt = t // tm
    smem = lambda: pl.BlockSpec((1, 1, tm), lambda i: (i, 0, 0), memory_space=pltpu.SMEM)
    return pl.pallas_call(
        functools.partial(_dispatch_kernel, tm=tm),
        grid=(nt,),
        in_specs=[smem(), smem(), pl.BlockSpec((tm, d), lambda i: (i, 0)), pl.BlockSpec(memory_space=pl.ANY)],
        out_specs=pl.BlockSpec(memory_space=pl.ANY),
        out_shape=jax.ShapeDtypeStruct((n_rows, d), h.dtype),
        scratch_shapes=[pltpu.SemaphoreType.DMA(())],
        input_output_aliases={3: 0},
        compiler_params=_cparams(("arbitrary",)),
        name="moe_dispatch",
    )(dest1.reshape(nt, 1, tm), dest2.reshape(nt, 1, tm), h, jnp.zeros((n_rows, d), h.dtype))


def _moe_kernel(be_ref, nb_ref, top_ref, x_ref, wg_ref, wu_ref, wd_ref, o_ref, xb_ref, acc_ref):
    i = pl.program_id(0)
    f = pl.program_id(1)
    active = i < nb_ref[0]
    top_half_only = top_ref[i] == 1

    @pl.when(jnp.logical_and(active, f == 0))
    def _():
        xb_ref[...] = x_ref[...].astype(BF16)
        acc_ref[...] = jnp.zeros_like(acc_ref)

    def ffn(rows):
        xb = xb_ref[rows, :]
        a = _silu(_dot(xb, wg_ref[...].astype(BF16))) * _dot(xb, wu_ref[...].astype(BF16))
        acc_ref[rows, :] += _dot(a.astype(BF16), wd_ref[...].astype(BF16))

    @pl.when(jnp.logical_and(active, jnp.logical_not(top_half_only)))
    def _():
        ffn(pl.ds(0, xb_ref.shape[0]))

    @pl.when(jnp.logical_and(active, top_half_only))
    def _():
        ffn(pl.ds(0, xb_ref.shape[0] // 2))

    @pl.when(f == pl.num_programs(1) - 1)
    def _():
        o_ref[...] = jnp.where(active, acc_ref[...], 0.0)


def _moe_ffn(x_buf, block_expert, n_active, top_half_only, wg, wu, wd):
    n_rows, d = x_buf.shape
    ff = wg.shape[2]
    tm = MOE_TILE
    tf = _tile(ff, FF_TILE)
    nf = ff // tf

    def wmap(i, f, be, nb):
        return be[i], jnp.where(i < nb[0], f, nf - 1)

    grid_spec = pltpu.PrefetchScalarGridSpec(
        num_scalar_prefetch=3,
        grid=(n_rows // tm, nf),
        in_specs=[pl.BlockSpec((tm, d), lambda i, f, be, nb, top: (jnp.minimum(i, nb[0] - 1), 0)),
                  pl.BlockSpec((None, d, tf),
                               lambda i, f, be, nb, top: (wmap(i, f, be, nb)[0], 0, wmap(i, f, be, nb)[1])),
                  pl.BlockSpec((None, d, tf),
                               lambda i, f, be, nb, top: (wmap(i, f, be, nb)[0], 0, wmap(i, f, be, nb)[1])),
                  pl.BlockSpec((None, tf, d),
                               lambda i, f, be, nb, top: (wmap(i, f, be, nb)[0], wmap(i, f, be, nb)[1], 0))],
        out_specs=pl.BlockSpec((tm, d), lambda i, f, be, nb, top: (i, 0)),
        scratch_shapes=[pltpu.VMEM((tm, d), BF16), pltpu.VMEM((tm, d), F32)],
    )
    return pl.pallas_call(
        _moe_kernel,
        grid_spec=grid_spec,
        out_shape=jax.ShapeDtypeStruct((n_rows, d), F32),
        compiler_params=_cparams(("arbitrary", "arbitrary")),
        name="moe_grouped_swiglu",
    )(block_expert, n_active, top_half_only, x_buf, wg, wu, wd)


def _combine_kernel(d1_ref, d2_ref, d1n_ref, d2n_ref, ybuf_ref, info_ref, x_ref, g_ref, mod_ref, o_ref,
                    y1_ref, y2_ref, sem, *, tm):
    i = pl.program_id(0)
    slot = i % 2

    def gather(da_ref, db_ref, s):
        def start(g, carry):
            for j in range(DMA_UNROLL):
                r = g * DMA_UNROLL + j
                pltpu.make_async_copy(ybuf_ref.at[pl.ds(da_ref[0, 0, r], 1), :], y1_ref.at[s, pl.ds(r, 1), :],
                                      sem.at[s]).start(priority=0)
                pltpu.make_async_copy(ybuf_ref.at[pl.ds(db_ref[0, 0, r], 1), :], y2_ref.at[s, pl.ds(r, 1), :],
                                      sem.at[s]).start(priority=1)
            return carry

        lax.fori_loop(0, tm // DMA_UNROLL, start, 0)

    @pl.when(i == 0)
    def _():
        gather(d1_ref, d2_ref, 0)

    @pl.when(i + 1 < pl.num_programs(0))
    def _():
        gather(d1n_ref, d2n_ref, 1 - slot)

    for dst in (y1_ref, y2_ref):
        pltpu.make_async_copy(ybuf_ref.at[pl.ds(0, tm), :], dst.at[slot], sem.at[slot]).wait()
    d = D_MODEL
    info = info_ref[...]
    y = y1_ref[slot] * info[:, 4:5] + y2_ref[slot] * info[:, 5:6]
    o_ref[...] = x_ref[...] + mod_ref[:, 5 * d:6 * d] * _rms(y, g_ref[...])


def _combine(y_buf, dest1, dest2, info, x2, g_post, mod_l, seq):
    t, d = x2.shape
    tm = _tile(seq, 512)
    nt = t // tm
    tpb = seq // tm
    smem = lambda: pl.BlockSpec((1, 1, tm), lambda i: (i, 0, 0), memory_space=pltpu.SMEM)
    smem_next = lambda: pl.BlockSpec((1, 1, tm), lambda i: (jnp.minimum(i + 1, nt - 1), 0, 0),
                                     memory_space=pltpu.SMEM)
    dest1, dest2 = dest1.reshape(nt, 1, tm), dest2.reshape(nt, 1, tm)
    return pl.pallas_call(
        functools.partial(_combine_kernel, tm=tm),
        grid=(nt,),
        in_specs=[smem(), smem(), smem_next(), smem_next(), pl.BlockSpec(memory_space=pl.ANY),
                  pl.BlockSpec((tm, LANES), lambda i: (i, 0)),
                  pl.BlockSpec((tm, d), lambda i: (i, 0)),
                  pl.BlockSpec((1, d), lambda i: (0, 0)),
                  pl.BlockSpec((None, 1, 6 * d), lambda i: (i // tpb, 0, 0))],
        out_specs=pl.BlockSpec((tm, d), lambda i: (i, 0)),
        out_shape=jax.ShapeDtypeStruct((t, d), F32),
        scratch_shapes=[pltpu.VMEM((2, tm, d), F32), pltpu.VMEM((2, tm, d), F32), pltpu.SemaphoreType.DMA((2,))],
        compiler_params=_cparams(("arbitrary",)),
        name="moe_combine",
    )(dest1, dest2, dest1, dest2, y_buf, info, x2, g_post.reshape(1, d), mod_l)


def _routed_ffn(h, logits, wg, wu, wd, x2, g_post, mod_l, seq):
    t, d = h.shape
    tm = MOE_TILE
    info, counts = _route(logits)
    counts = counts.astype(jnp.int32)
    padded = ((counts + tm - 1) // tm) * tm
    pad_end = jnp.cumsum(padded)
    pad_start = pad_end - padded
    e1 = info[:, 0].astype(jnp.int32)
    e2 = info[:, 1].astype(jnp.int32)
    dest1 = pad_start[e1] + info[:, 2].astype(jnp.int32)
    dest2 = pad_start[e2] + info[:, 3].astype(jnp.int32)
    n_rows = t * TOP_K + N_EXPERTS * tm
    n_blocks = n_rows // tm
    block_start = jnp.arange(n_blocks, dtype=jnp.int32) * tm
    block_expert = jnp.minimum(jnp.sum(pad_end[None, :] <= block_start[:, None], axis=1), N_EXPERTS - 1).astype(jnp.int32)
    n_active = (pad_end[-1] // tm).astype(jnp.int32).reshape(1)
    rows_used = (pad_start + counts)[block_expert] - block_start
    top_half_only = (rows_used <= tm // 2).astype(jnp.int32)
    x_buf = _dispatch(h, dest1, dest2, n_rows)
    y_buf = _moe_ffn(x_buf, block_expert, n_active, top_half_only, wg, wu, wd)
    return _combine(y_buf, dest1, dest2, info, x2, g_post, mod_l, seq)


def kernel(x, c, positions, ada_w, ada_b, norm_pre_mix, norm_post_mix, norm_pre_ffn, norm_post_ffn, w_in, ret_norm, ssm_a_re, ssm_a_im, ssm_b_re, ssm_b_im, ssm_c_re, ssm_c_im, ssm_d, ssm_log_dt, ssm_glu_w, ssm_glu_b, ssm_norm, mla_q_norm, mla_w_uq, mla_kv_norm, mla_w_ukv, mla_norm, w_out, ffn_w_gate, ffn_w_up, ffn_w_down, moe_router, moe_w_gate, moe_w_up, moe_w_down):
    bsz, seq, d = x.shape
    depth = ada_w.shape[0]
    assert d == D_MODEL and seq % SSM_CHUNK == 0 and seq % RET_CHUNK == 0
    t = bsz * seq
    x2 = x.reshape(t, d)
    mod = _modulation(c, ada_w, ada_b)
    cos128, sin128 = _rope_tables(positions)
    ssm_params = _ssm_params(ssm_a_re, ssm_a_im, ssm_b_re, ssm_b_im, ssm_c_re, ssm_c_im, ssm_log_dt)
    for layer in range(depth):
        mod_l = mod[layer]
        r, u, cq, ckv, kr = _inproj(x2, norm_pre_mix[layer], mod_l, w_in[layer], seq)
        ret = _retention(r, cos128, sin128, ret_norm[layer], bsz, seq)
        ssm = _ssm_post(_ssm_scan(u, ssm_params, layer, seq), u, ssm_d[layer], ssm_glu_w[layer], ssm_glu_b[layer],
                        ssm_norm[layer])
        q3, k3, v3 = _mla_prep(cq, ckv, kr, cos128, sin128, mla_q_norm[layer], mla_w_uq[layer],
                               mla_kv_norm[layer], mla_w_ukv[layer])
        att = _attention(q3, k3, v3, bsz, seq)
        j = layer // 2
        if layer % 2 == 0:
            x2, h = _outproj(x2, ret, ssm, att, mla_norm[layer], w_out[layer], norm_post_mix[layer], mod_l,
                             norm_pre_ffn[layer], seq)
            x2 = _dense_ffn(h, ffn_w_gate[j], ffn_w_up[j], ffn_w_down[j], x2, norm_post_ffn[layer], mod_l, seq)
        else:
            x2, h, logits = _outproj(x2, ret, ssm, att, mla_norm[layer], w_out[layer], norm_post_mix[layer], mod_l,
                                     norm_pre_ffn[layer], seq, router=moe_router[j])
            x2 = _routed_ffn(h, logits, moe_w_gate[j], moe_w_up[j], moe_w_down[j], x2, norm_post_ffn[layer],
                             mod_l, seq)
    return x2.reshape(bsz, seq, d)
```

```python
import functools
import math

import numpy as np
import jax
import jax.numpy as jnp
from jax import lax
from jax.experimental import pallas as pl
from jax.experimental.pallas import tpu as pltpu

D_MODEL = 1024
RET_HEADS = 4
RET_HEAD_DIM = 64
RET_WIDTH = RET_HEADS * RET_HEAD_DIM
RET_CHUNK = 128
SSM_GROUP_CH = 16
SSM_GROUPS = 16
SSM_WIDTH = SSM_GROUPS * SSM_GROUP_CH
SSM_STATE = 64
MLA_HEADS = 4
MLA_NOPE = 128
MLA_ROPE = 64
MLA_V = 128
MLA_QK = MLA_NOPE + MLA_ROPE
MLA_WIDTH = MLA_HEADS * MLA_V
MLA_Q_RANK = 256
MLA_KV_RANK = 128
ROPE_DIM = 64
ROPE_BASE = 10000.0
D_FF = 3584
N_EXPERTS = 8
TOP_K = 2
EPS = 1e-6
NEG_INF = -1e30
IN_COLS = 4 * RET_WIDTH + SSM_WIDTH + MLA_Q_RANK + MLA_KV_RANK + MLA_ROPE

LANES = 128
VMEM_LIMIT_BYTES = 56 * 1024 * 1024

SSM_CHUNK = 64
MOE_TILE = 1024
FF_TILE = 512
DMA_UNROLL = 8

BF16 = jnp.bfloat16
F32 = jnp.float32


def _cparams(sem):
    return pltpu.CompilerParams(dimension_semantics=sem, vmem_limit_bytes=VMEM_LIMIT_BYTES)


def _tile(n, pref):
    t = min(n, pref)
    while n % t:
        t //= 2
    return t


def _dot(a, b):
    return jnp.dot(a, b, preferred_element_type=F32)


def _dot3(a, b):
    a_hi = a.astype(BF16)
    a_lo = (a - a_hi.astype(F32)).astype(BF16)
    b_hi = b.astype(BF16)
    b_lo = (b - b_hi.astype(F32)).astype(BF16)
    return _dot(a_hi, b_hi) + (_dot(a_hi, b_lo) + _dot(a_lo, b_hi))


def _rms(x, g):
    return x * lax.rsqrt(jnp.mean(x * x, axis=-1, keepdims=True) + EPS) * g


def _silu(x):
    return x * (1.0 / (1.0 + jnp.exp(-x)))


def _rope128(x, cos, sin_signed):
    lane = lax.broadcasted_iota(jnp.int32, x.shape, 1)
    first_half = (lane % ROPE_DIM) < (ROPE_DIM // 2)
    partner = jnp.where(first_half, pltpu.roll(x, LANES - ROPE_DIM // 2, 1), pltpu.roll(x, ROPE_DIM // 2, 1))
    return x * cos + partner * sin_signed


def _mod_kernel(c_ref, w_ref, b_ref, o_ref):
    cond = _silu(c_ref[...])
    o_ref[...] = _dot3(cond, w_ref[...]) + b_ref[...]


def _modulation(c, ada_w, ada_b):
    depth, d, n = ada_w.shape
    bsz = c.shape[0]
    rows = 8
    c_pad = jnp.zeros((rows, d), F32).at[:bsz].set(c)
    tn = _tile(n, 1536)
    out = pl.pallas_call(
        _mod_kernel,
        grid=(depth, n // tn),
        in_specs=[pl.BlockSpec((rows, d), lambda l, j: (0, 0)),
                  pl.BlockSpec((None, d, tn), lambda l, j: (l, 0, j)),
                  pl.BlockSpec((None, 1, tn), lambda l, j: (l, 0, j))],
        out_specs=pl.BlockSpec((None, rows, tn), lambda l, j: (l, 0, j)),
        out_shape=jax.ShapeDtypeStruct((depth, rows, n), F32),
        compiler_params=_cparams(("arbitrary", "arbitrary")),
        name="adaln_mod",
    )(c_pad, ada_w, ada_b.reshape(depth, 1, n))
    return out[:, :bsz].reshape(depth, bsz, 1, n)


def _rope_kernel(pos_ref, inv_ref, cos_ref, sin_ref):
    pos = pos_ref[...].astype(F32)
    lane = lax.broadcasted_iota(jnp.int32, cos_ref.shape, 1)
    half = ROPE_DIM // 2
    p = jnp.where(lane < half, pos[:, 0:1],
                  jnp.where(lane < 2 * half, pos[:, 1:2],
                            jnp.where(lane < 3 * half, pos[:, 2:3], pos[:, 3:4])))
    ang = p * inv_ref[...]
    cos_ref[...] = jnp.cos(ang)
    sin_ref[...] = jnp.sin(ang)


def _rope_tables(positions):
    t = positions.size
    half = ROPE_DIM // 2
    per_row = LANES // half
    inv = ROPE_BASE ** (-jnp.arange(0, ROPE_DIM, 2, dtype=F32) / ROPE_DIM)
    inv_row = jnp.tile(inv, per_row).reshape(1, LANES)
    rows = t // per_row
    tr = _tile(rows, 1024)
    cos, sin = pl.pallas_call(
        _rope_kernel,
        grid=(rows // tr,),
        in_specs=[pl.BlockSpec((tr, per_row), lambda i: (i, 0)),
                  pl.BlockSpec((1, LANES), lambda i: (0, 0))],
        out_specs=[pl.BlockSpec((tr, LANES), lambda i: (i, 0))] * 2,
        out_shape=[jax.ShapeDtypeStruct((rows, LANES), F32)] * 2,
        compiler_params=_cparams(("arbitrary",)),
        name="rope_tables",
    )(positions.reshape(rows, per_row), inv_row)
    cos = cos.reshape(t, half)
    sin = sin.reshape(t, half)
    return jnp.tile(cos, (1, per_row)), jnp.concatenate([-sin, sin, -sin, sin], axis=1)


def _inproj_kernel(x_ref, g_ref, mod_ref, w_ref, ret_ref, u_ref, cq_ref, ckv_ref, kr_ref):
    d = D_MODEL
    x = x_ref[...]
    h = _rms(x, g_ref[...]) * (1.0 + mod_ref[:, d:2 * d]) + mod_ref[:, 0:d]
    p = _dot(h.astype(BF16), w_ref[...])
    o = 4 * RET_WIDTH
    ret_ref[...] = p[:, :o].astype(BF16)
    u_ref[...] = p[:, o:o + SSM_WIDTH].astype(BF16)
    o += SSM_WIDTH
    cq_ref[...] = p[:, o:o + MLA_Q_RANK].astype(BF16)
    o += MLA_Q_RANK
    ckv_ref[...] = p[:, o:o + MLA_KV_RANK].astype(BF16)
    o += MLA_KV_RANK
    kr_ref[...] = p[:, o:o + MLA_ROPE].astype(BF16)


def _inproj(x2, g, mod_l, w_in, seq):
    t, d = x2.shape
    tm = _tile(seq, 512)
    tpb = seq // tm
    widths = (4 * RET_WIDTH, SSM_WIDTH, MLA_Q_RANK, MLA_KV_RANK, MLA_ROPE)
    return pl.pallas_call(
        _inproj_kernel,
        grid=(t // tm,),
        in_specs=[pl.BlockSpec((tm, d), lambda i: (i, 0)),
                  pl.BlockSpec((1, d), lambda i: (0, 0)),
                  pl.BlockSpec((None, 1, 6 * d), lambda i: (i // tpb, 0, 0)),
                  pl.BlockSpec((d, IN_COLS), lambda i: (0, 0))],
        out_specs=[pl.BlockSpec((tm, w), lambda i: (i, 0)) for w in widths],
        out_shape=[jax.ShapeDtypeStruct((t, w), BF16) for w in widths],
        compiler_params=_cparams(("arbitrary",)),
        name="mixer_inproj",
    )(x2, g.reshape(1, d), mod_l, w_in.astype(BF16))


def _retention_kernel(r_ref, cos_ref, sin_ref, dmask_ref, qdec_ref, kdec_ref, cdec_ref, bd_ref, gn_ref,
                      o_ref, state_ref, *, chunks):
    w = RET_WIDTH

    @pl.when(pl.program_id(1) == 0)
    def _():
        state_ref[...] = jnp.zeros_like(state_ref)

    lane = lax.broadcasted_iota(jnp.int32, (RET_CHUNK, w), 1)
    bd = bd_ref[...]
    for c in range(chunks):
        rows = pl.ds(c * RET_CHUNK, RET_CHUNK)
        cos = cos_ref[rows, :]
        sin = sin_ref[rows, :]

        def rope(v):
            return jnp.concatenate([_rope128(v[:, :LANES], cos, sin), _rope128(v[:, LANES:], cos, sin)], axis=1)

        q = rope(r_ref[rows, 0:w].astype(F32))
        k = rope(r_ref[rows, w:2 * w].astype(F32)) * (RET_HEAD_DIM ** -0.5)
        v = r_ref[rows, 2 * w:3 * w]
        gate = r_ref[rows, 3 * w:4 * w].astype(F32)
        kb = k.astype(BF16)
        state = state_ref[...]
        y = _dot((q * qdec_ref[...]).astype(BF16), state.astype(BF16))
        for h in range(RET_HEADS):
            in_head = (lane // RET_HEAD_DIM) == h
            qh = jnp.where(in_head, q, 0.0).astype(BF16)
            s = lax.dot_general(qh, kb, (((1,), (1,)), ((), ())), preferred_element_type=F32)
            s = s * dmask_ref[h]
            y = y + jnp.where(in_head, _dot(s.astype(BF16), v), 0.0)
        kv = lax.dot_general((k * kdec_ref[...]).astype(BF16), v, (((0,), (0,)), ((), ())),
                             preferred_element_type=F32)
        state_ref[...] = cdec_ref[...] * state + bd * kv
        y2 = y * y
        y2_hi = y2.astype(BF16)
        y2_lo = (y2 - y2_hi.astype(F32)).astype(BF16)
        bdb = bd.astype(BF16)
        ms = (_dot(y2_hi, bdb) + _dot(y2_lo, bdb)) * (1.0 / RET_HEAD_DIM)
        yn = y * lax.rsqrt(ms + EPS) * gn_ref[...]
        o_ref[rows, :] = (_silu(gate) * yn).astype(BF16)


def _retention_consts():
    c = RET_CHUNK
    log_g = np.log1p(-(2.0 ** (-5.0 - np.arange(RET_HEADS, dtype=np.float32)))).astype(np.float32)
    i = np.arange(c, dtype=np.float32)
    diff = i[:, None] - i[None, :]
    dmask = np.where(diff >= 0, np.exp(log_g[:, None, None] * np.maximum(diff, 0.0)), 0.0).astype(np.float32)
    k_dec = np.exp(log_g[None, :] * (c - 1 - i)[:, None]).astype(np.float32)
    q_dec = np.exp(log_g[None, :] * (i + 1.0)[:, None]).astype(np.float32)
    c_dec = np.exp(log_g * c).astype(np.float32)
    rep = lambda a: np.repeat(a, RET_HEAD_DIM, axis=-1)
    head = np.arange(RET_WIDTH) // RET_HEAD_DIM
    bd = (head[:, None] == head[None, :]).astype(np.float32)
    return dmask, rep(q_dec), rep(k_dec), rep(c_dec[None, :]), bd


def _retention(r, cos128, sin128, ret_norm, bsz, seq):
    t = r.shape[0]
    w = RET_WIDTH
    tr = _tile(seq, 512)
    chunks = tr // RET_CHUNK
    nt = seq // tr
    dmask, qdec, kdec, cdec, bd = (jnp.asarray(a) for a in _retention_consts())
    const = lambda shape: pl.BlockSpec(shape, lambda b, i: (0,) * len(shape))
    return pl.pallas_call(
        functools.partial(_retention_kernel, chunks=chunks),
        grid=(bsz, nt),
        in_specs=[pl.BlockSpec((tr, 4 * w), lambda b, i: (b * nt + i, 0)),
                  pl.BlockSpec((tr, LANES), lambda b, i: (b * nt + i, 0)),
                  pl.BlockSpec((tr, LANES), lambda b, i: (b * nt + i, 0)),
                  const((RET_HEADS, RET_CHUNK, RET_CHUNK)),
                  const((RET_CHUNK, w)), const((RET_CHUNK, w)), const((1, w)), const((w, w)), const((1, w))],
        out_specs=pl.BlockSpec((tr, w), lambda b, i: (b * nt + i, 0)),
        out_shape=jax.ShapeDtypeStruct((t, w), BF16),
        scratch_shapes=[pltpu.VMEM((w, w), F32)],
        compiler_params=_cparams(("arbitrary", "arbitrary")),
        name="retention",
    )(r, cos128, sin128, dmask, qdec, kdec, cdec, bd, ret_norm.reshape(1, w))


def _expand(x, sel):
    hi = x.astype(BF16)
    lo = (x - hi.astype(F32)).astype(BF16)
    return _dot(hi, sel) + _dot(lo, sel)


def _ssm_kernel(u_ref, acol_ref, arow_ref, ldt_ref, b_ref, bt_ref, ct_ref, esel_ref, ech_ref, y_ref, toep_ref,
                *, chunks_per_seq, steps):
    tc, hg, p = SSM_CHUNK, SSM_GROUP_CH, SSM_STATE
    k = tc * hg
    dt = jnp.exp(ldt_ref[...])
    ar, ai = acol_ref[:, 0:1], acol_ref[:, 1:2]
    n = lax.broadcasted_iota(jnp.int32, (p, LANES), 1).astype(F32)
    mag = jnp.exp(ar * dt * n)
    pw_re, pw_im = mag * jnp.cos(ai * dt * n), mag * jnp.sin(ai * dt * n)
    def zoh(a_r, a_i):
        m1 = jnp.exp(a_r * dt)
        l_r, l_i = m1 * jnp.cos(a_i * dt), m1 * jnp.sin(a_i * dt)
        den = a_r * a_r + a_i * a_i
        return ((l_r - 1.0) * a_r + l_i * a_i) / den, (l_i * a_r - (l_r - 1.0) * a_i) / den

    f_re, f_im = zoh(ar, ai)
    bb_re = f_re * b_ref[0] - f_im * b_ref[1]
    bb_im = f_re * b_ref[1] + f_im * b_ref[0]
    fr_re, fr_im = zoh(arow_ref[0:1, 0:p], arow_ref[1:2, 0:p])
    bbt_re = fr_re * bt_ref[0] - fr_im * bt_ref[1]
    bbt_im = fr_re * bt_ref[1] + fr_im * bt_ref[0]
    ech = ech_ref[...]
    pt_re, pt_im = _expand(pw_re, esel_ref[0]), _expand(pw_im, esel_ref[0])
    p1_re, p1_im = _expand(pw_re, esel_ref[1]), _expand(pw_im, esel_ref[1])
    pr_re, pr_im = _expand(pw_re, esel_ref[2]), _expand(pw_im, esel_ref[2])
    bbr, bbi = _expand(bb_re, ech), _expand(bb_im, ech)
    cr, ci = _expand(ct_ref[0], ech), _expand(ct_ref[1], ech)
    z_re = cr * pt_re - ci * pt_im
    z_im = cr * pt_im + ci * pt_re
    strip = _dot3(bbt_re, z_re) - _dot3(bbt_im, z_im)
    w_in_t = jnp.concatenate([pr_re * bbr - pr_im * bbi, pr_re * bbi + pr_im * bbr], axis=0).astype(BF16)
    v_out = jnp.concatenate([cr * p1_re - ci * p1_im, -(cr * p1_im + ci * p1_re)], axis=0).astype(BF16)
    padded = jnp.concatenate([jnp.zeros((hg, k), F32), strip], axis=1)
    for s in range(tc):
        toep_ref[s * hg:(s + 1) * hg, :] = padded[:, k - s * hg:2 * k - s * hg].astype(BF16)
    dist = (tc * (1 << lax.broadcasted_iota(jnp.int32, (steps, 2 * p), 0))).astype(F32)
    lane = lax.broadcasted_iota(jnp.int32, (steps, 2 * p), 1)
    amag = jnp.exp(arow_ref[0:1, :] * dt * dist)
    a_cos = amag * jnp.cos(arow_ref[1:2, :] * dt * dist)
    a_sin = jnp.where(lane < p, -1.0, 1.0) * amag * jnp.sin(arow_ref[1:2, :] * dt * dist)

    u = u_ref[...]
    x = lax.dot_general(u, w_in_t, (((1,), (1,)), ((), ())), preferred_element_type=F32)
    row = lax.broadcasted_iota(jnp.int32, x.shape, 0) % chunks_per_seq

    def shift_rows(a, d):
        return jnp.where(row >= d, pltpu.roll(a, d, 0), 0.0)

    for j in range(steps):
        prev = shift_rows(x, 1 << j)
        x = x + prev * a_cos[j:j + 1, :] + pltpu.roll(prev, p, 1) * a_sin[j:j + 1, :]
    x_in = shift_rows(x, 1)
    y_ref[...] = (_dot(u, toep_ref[...]) + _dot(x_in.astype(BF16), v_out)).astype(y_ref.dtype)


def _ssm_params(a_re, a_im, b_re, b_im, c_re, c_im, log_dt):
    a_col = jnp.stack([a_re, a_im], axis=-1)
    a_row = jnp.stack([jnp.concatenate([a_re, a_re], -1), jnp.concatenate([a_im, a_im], -1)], axis=2)
    b = jnp.stack([b_re, b_im], axis=2)
    bt = jnp.swapaxes(b, -1, -2)
    ct = jnp.swapaxes(jnp.stack([c_re, c_im], axis=2), -1, -2)
    return a_col, a_row, log_dt[..., None, None], b, bt, ct


def _ssm_selectors():
    tc, hg = SSM_CHUNK, SSM_GROUP_CH
    t_of_lane = np.arange(tc * hg) // hg
    n = np.arange(LANES)[:, None]
    esel = np.stack([n == t_of_lane, n == t_of_lane + 1, n == tc - 1 - t_of_lane]).astype(np.float32)
    ech = (np.arange(hg)[:, None] == (np.arange(tc * hg) % hg)[None, :]).astype(np.float32)
    return jnp.asarray(esel, BF16), jnp.asarray(ech, BF16)


def _ssm_scan(u, params, layer, seq):
    t = u.shape[0]
    g, hg, tc, p = SSM_GROUPS, SSM_GROUP_CH, SSM_CHUNK, SSM_STATE
    nc = t // tc
    cps = seq // tc
    steps = max(1, int(math.ceil(math.log2(cps))))
    k = tc * hg
    esel, ech = _ssm_selectors()
    ug = u.reshape(nc, tc, g, hg).transpose(2, 0, 1, 3).reshape(g, nc, k)
    per_group = lambda *blk: pl.BlockSpec((None, None) + blk, lambda i: (layer, i) + (0,) * len(blk))
    y = pl.pallas_call(
        functools.partial(_ssm_kernel, chunks_per_seq=cps, steps=steps),
        grid=(g,),
        in_specs=[pl.BlockSpec((None, nc, k), lambda i: (i, 0, 0)),
                  per_group(p, 2), per_group(2, 2 * p), per_group(1, 1),
                  per_group(2, p, hg), per_group(2, hg, p), per_group(2, p, hg),
                  pl.BlockSpec((3, LANES, k), lambda i: (0, 0, 0)),
                  pl.BlockSpec((hg, k), lambda i: (0, 0))],
        out_specs=pl.BlockSpec((None, nc, k), lambda i: (i, 0, 0)),
        out_shape=jax.ShapeDtypeStruct((g, nc, k), BF16),
        scratch_shapes=[pltpu.VMEM((k, k), BF16)],
        compiler_params=_cparams(("arbitrary",)),
        name="s5_chunk_scan",
    )(ug, *params, esel, ech)
    return y.reshape(g, nc, tc, hg).transpose(1, 2, 0, 3).reshape(t, g * hg)


def _ssm_post_kernel(y_ref, u_ref, d_ref, gw_ref, gb_ref, gn_ref, o_ref):
    y = y_ref[...].astype(F32) + d_ref[...] * u_ref[...].astype(F32)
    y = 0.5 * y * (1.0 + jnp.tanh(math.sqrt(2.0 / math.pi) * (y + 0.044715 * (y * y * y))))
    z = _dot(y.astype(BF16), gw_ref[...]) + gb_ref[...]
    y = y * (1.0 / (1.0 + jnp.exp(-z)))
    o_ref[...] = _rms(y, gn_ref[...]).astype(BF16)


def _ssm_post(y, u, d, glu_w, glu_b, ssm_norm):
    t, w = y.shape
    tm = _tile(t, 1024)
    row = lambda: pl.BlockSpec((1, w), lambda i: (0, 0))
    return pl.pallas_call(
        _ssm_post_kernel,
        grid=(t // tm,),
        in_specs=[pl.BlockSpec((tm, w), lambda i: (i, 0)), pl.BlockSpec((tm, w), lambda i: (i, 0)),
                  row(), pl.BlockSpec((w, w), lambda i: (0, 0)), row(), row()],
        out_specs=pl.BlockSpec((tm, w), lambda i: (i, 0)),
        out_shape=jax.ShapeDtypeStruct((t, w), BF16),
        compiler_params=_cparams(("arbitrary",)),
        name="s5_gelu_glu_norm",
    )(y, u, d.reshape(1, w), glu_w.astype(BF16), glu_b.reshape(1, w), ssm_norm.reshape(1, w))


def _mla_prep_kernel(cq_ref, ckv_ref, kr_ref, cos_ref, sin_ref, qn_ref, kvn_ref, wq_ref, wkv_ref,
                     q_ref, k_ref, v_ref):
    cos = cos_ref[...]
    sin = sin_ref[...]
    nope_w = MLA_HEADS * MLA_NOPE
    q = _dot(_rms(cq_ref[...].astype(F32), qn_ref[...]).astype(BF16), wq_ref[...])
    q = q * (MLA_QK ** -0.5 * math.log2(math.e))
    q_rope = jnp.concatenate([_rope128(q[:, nope_w:nope_w + LANES], cos, sin),
                              _rope128(q[:, nope_w + LANES:], cos, sin)], axis=1)
    kv = _dot(_rms(ckv_ref[...].astype(F32), kvn_ref[...]).astype(BF16), wkv_ref[...])
    kr = kr_ref[...].astype(F32)
    k_pe = _rope128(jnp.concatenate([kr, kr], axis=1), cos, sin)[:, :MLA_ROPE]
    for h in range(MLA_HEADS):
        q_ref[h] = jnp.concatenate([q[:, h * MLA_NOPE:(h + 1) * MLA_NOPE],
                                    q_rope[:, h * MLA_ROPE:(h + 1) * MLA_ROPE]], axis=1).astype(BF16)
        k_ref[h] = jnp.concatenate([kv[:, h * MLA_NOPE:(h + 1) * MLA_NOPE], k_pe], axis=1).astype(BF16)
        v_ref[h, :, 0:MLA_V] = kv[:, nope_w + h * MLA_V:nope_w + (h + 1) * MLA_V].astype(BF16)
        v_ref[h, :, MLA_V:] = jnp.ones((kv.shape[0], LANES), BF16)


def _mla_prep(cq, ckv, kr, cos128, sin128, q_norm, w_uq, kv_norm, w_ukv):
    t = cq.shape[0]
    tm = _tile(t, 512)
    hh = MLA_HEADS
    wq = w_uq.reshape(MLA_Q_RANK, hh, MLA_QK)
    wq = jnp.concatenate([wq[:, :, :MLA_NOPE].reshape(MLA_Q_RANK, -1), wq[:, :, MLA_NOPE:].reshape(MLA_Q_RANK, -1)], 1)
    wkv = w_ukv.reshape(MLA_KV_RANK, hh, MLA_NOPE + MLA_V)
    wkv = jnp.concatenate([wkv[:, :, :MLA_NOPE].reshape(MLA_KV_RANK, -1), wkv[:, :, MLA_NOPE:].reshape(MLA_KV_RANK, -1)], 1)
    tok = lambda w: pl.BlockSpec((tm, w), lambda i: (i, 0))
    full = lambda a, b: pl.BlockSpec((a, b), lambda i: (0, 0))
    return pl.pallas_call(
        _mla_prep_kernel,
        grid=(t // tm,),
        in_specs=[tok(MLA_Q_RANK), tok(MLA_KV_RANK), tok(MLA_ROPE), tok(LANES), tok(LANES),
                  full(1, MLA_Q_RANK), full(1, MLA_KV_RANK),
                  full(MLA_Q_RANK, hh * MLA_QK), full(MLA_KV_RANK, hh * (MLA_NOPE + MLA_V))],
        out_specs=[pl.BlockSpec((hh, tm, MLA_QK), lambda i: (0, i, 0)),
                   pl.BlockSpec((hh, tm, MLA_QK), lambda i: (0, i, 0)),
                   pl.BlockSpec((hh, tm, MLA_V + LANES), lambda i: (0, i, 0))],
        out_shape=[jax.ShapeDtypeStruct((hh, t, MLA_QK), BF16), jax.ShapeDtypeStruct((hh, t, MLA_QK), BF16),
                   jax.ShapeDtypeStruct((hh, t, MLA_V + LANES), BF16)],
        compiler_params=_cparams(("arbitrary",)),
        name="mla_prep",
    )(cq, ckv, kr, cos128, sin128, q_norm.reshape(1, -1), kv_norm.reshape(1, -1), wq.astype(BF16), wkv.astype(BF16))


def _attn_kernel(q_ref, k_ref, v_ref, o_ref, m_ref, l_ref, acc_ref, sa_ref, sb_ref, *, tq, th, tk):
    qi = pl.program_id(2)
    m_ref[...] = jnp.full_like(m_ref, NEG_INF)
    l_ref[...] = jnp.zeros_like(l_ref)
    acc_ref[...] = jnp.zeros_like(acc_ref)

    halves = (0, 1)
    assert tq == 2 * th and tk == th

    def scores(s_ref, j, which=halves):
        k = k_ref[pl.ds(pl.multiple_of(j * tk, tk), tk), :]
        for half in which:
            rows = pl.ds(half * th, th)
            s_ref[rows, :] = lax.dot_general(q_ref[rows, :], k, (((1,), (1,)), ((), ())),
                                             preferred_element_type=F32)

    def consume(s_ref, j, diag_half=None):
        v = v_ref[pl.ds(pl.multiple_of(j * tk, tk), tk), :]
        for half in halves:
            if diag_half is not None and half < diag_half:
                continue
            rows = pl.ds(half * th, th)
            s = s_ref[rows, :]
            if half == diag_half:
                r = lax.broadcasted_iota(jnp.int32, s.shape, 0)
                c = lax.broadcasted_iota(jnp.int32, s.shape, 1)
                s = jnp.where(c <= r, s, NEG_INF)
            m_prev = m_ref[rows, :]
            m_new = jnp.maximum(m_prev, jnp.max(s, axis=-1, keepdims=True))
            alpha = jnp.exp2(m_prev - m_new)
            p = jnp.exp2(s - jnp.tile(m_new, (1, tk // LANES)))
            pv = _dot(p.astype(BF16), v)
            acc_ref[rows, :] = alpha * acc_ref[rows, :] + pv[:, :MLA_V]
            l_ref[rows, :] = alpha * l_ref[rows, :] + pv[:, MLA_V:]
            m_ref[rows, :] = m_new

    scores(sa_ref, 0)

    def pair(i, carry):
        scores(sb_ref, 2 * i + 1)
        consume(sa_ref, 2 * i)
        scores(sa_ref, 2 * i + 2)
        consume(sb_ref, 2 * i + 1)
        return carry

    def two_pairs(i2, carry):
        pair(2 * i2, carry)
        return pair(2 * i2 + 1, carry)

    lax.fori_loop(0, qi // 2, two_pairs, 0)

    @pl.when(qi % 2 == 1)
    def _():
        pair(qi - 1, 0)

    scores(sb_ref, 2 * qi + 1, which=(1,))
    consume(sa_ref, 2 * qi, diag_half=0)
    consume(sb_ref, 2 * qi + 1, diag_half=1)
    o_ref[...] = (acc_ref[...] / l_ref[...]).astype(o_ref.dtype)


def _attention(q3, k3, v3, bsz, seq):
    hh, t, _ = q3.shape
    tq = _tile(seq, 1024)
    th = tq // 2
    tk = th
    nq = seq // tq
    vw = v3.shape[-1]
    return pl.pallas_call(
        functools.partial(_attn_kernel, tq=tq, th=th, tk=tk),
        grid=(bsz, hh, nq),
        in_specs=[pl.BlockSpec((None, tq, MLA_QK), lambda b, h, i: (h, b * nq + i, 0)),
                  pl.BlockSpec((None, seq, MLA_QK), lambda b, h, i: (h, b, 0)),
                  pl.BlockSpec((None, seq, vw), lambda b, h, i: (h, b, 0))],
        out_specs=pl.BlockSpec((tq, MLA_V), lambda b, h, i: (b * nq + i, h)),
        out_shape=jax.ShapeDtypeStruct((t, hh * MLA_V), BF16),
        scratch_shapes=[pltpu.VMEM((tq, LANES), F32), pltpu.VMEM((tq, LANES), F32), pltpu.VMEM((tq, MLA_V), F32),
                        pltpu.VMEM((tq, tk), F32), pltpu.VMEM((tq, tk), F32)],
        compiler_params=_cparams(("arbitrary", "arbitrary", "arbitrary")),
        name="mla_flash_attention",
    )(q3, k3, v3)


def _outproj_kernel(*refs, routed):
    if routed:
        (x_ref, ret_ref, ssm_ref, att_ref, an_ref, w_ref, gpost_ref, mod_ref, gpre_ref, router_ref,
         xo_ref, h_ref, logit_ref) = refs
    else:
        x_ref, ret_ref, ssm_ref, att_ref, an_ref, w_ref, gpost_ref, mod_ref, gpre_ref, xo_ref, h_ref = refs
    d = D_MODEL
    att = _rms(att_ref[...].astype(F32), an_ref[...]).astype(BF16)
    o1 = RET_WIDTH
    o2 = RET_WIDTH + SSM_WIDTH
    y = _dot(ret_ref[...], w_ref[0:o1, :]) + _dot(ssm_ref[...], w_ref[o1:o2, :]) + _dot(att, w_ref[o2:, :])
    x = x_ref[...] + mod_ref[:, 2 * d:3 * d] * _rms(y, gpost_ref[...])
    xo_ref[...] = x
    h = _rms(x, gpre_ref[...]) * (1.0 + mod_ref[:, 4 * d:5 * d]) + mod_ref[:, 3 * d:4 * d]
    h_ref[...] = h.astype(h_ref.dtype)
    if routed:
        logit_ref[...] = _dot3(h, router_ref[...])


def _outproj(x2, ret, ssm, att, mla_norm, w_out, g_post, mod_l, g_pre, seq, router=None):
    t, d = x2.shape
    tm = _tile(seq, 512)
    tpb = seq // tm
    routed = router is not None
    tok = lambda w: pl.BlockSpec((tm, w), lambda i: (i, 0))
    full = lambda a, b: pl.BlockSpec((a, b), lambda i: (0, 0))
    in_specs = [tok(d), tok(RET_WIDTH), tok(SSM_WIDTH), tok(MLA_WIDTH), full(1, MLA_WIDTH), full(d, d), full(1, d),
                pl.BlockSpec((None, 1, 6 * d), lambda i: (i // tpb, 0, 0)), full(1, d)]
    args = [x2, ret, ssm, att, mla_norm.reshape(1, -1), w_out.astype(BF16), g_post.reshape(1, d), mod_l,
            g_pre.reshape(1, d)]
    out_specs = [tok(d), tok(d)]
    out_shape = [jax.ShapeDtypeStruct((t, d), F32), jax.ShapeDtypeStruct((t, d), F32 if routed else BF16)]
    if routed:
        in_specs.append(full(d, LANES))
        args.append(jnp.zeros((d, LANES), F32).at[:, :N_EXPERTS].set(router))
        out_specs.append(tok(LANES))
        out_shape.append(jax.ShapeDtypeStruct((t, LANES), F32))
    return pl.pallas_call(
        functools.partial(_outproj_kernel, routed=routed),
        grid=(t // tm,),
        in_specs=in_specs, out_specs=out_specs, out_shape=out_shape,
        compiler_params=_cparams(("arbitrary",)),
        name="mixer_outproj_routed" if routed else "mixer_outproj",
    )(*args)


def _ffn_kernel(h_ref, wg_ref, wu_ref, wd_ref, x_ref, g_ref, mod_ref, o_ref, acc_ref):
    f = pl.program_id(1)

    @pl.when(f == 0)
    def _():
        acc_ref[...] = jnp.zeros_like(acc_ref)

    h = h_ref[...]
    a = _silu(_dot(h, wg_ref[...].astype(BF16))) * _dot(h, wu_ref[...].astype(BF16))
    acc_ref[...] += _dot(a.astype(BF16), wd_ref[...].astype(BF16))

    @pl.when(f == pl.num_programs(1) - 1)
    def _():
        d = D_MODEL
        o_ref[...] = x_ref[...] + mod_ref[:, 5 * d:6 * d] * _rms(acc_ref[...], g_ref[...])


def _dense_ffn(h, wg, wu, wd, x2, g_post, mod_l, seq):
    t, d = h.shape
    ff = wg.shape[1]
    tm = _tile(seq, 1024)
    tf = _tile(ff, FF_TILE)
    tpb = seq // tm
    return pl.pallas_call(
        _ffn_kernel,
        grid=(t // tm, ff // tf),
        in_specs=[pl.BlockSpec((tm, d), lambda i, f: (i, 0)),
                  pl.BlockSpec((d, tf), lambda i, f: (0, f)),
                  pl.BlockSpec((d, tf), lambda i, f: (0, f)),
                  pl.BlockSpec((tf, d), lambda i, f: (f, 0)),
                  pl.BlockSpec((tm, d), lambda i, f: (i, 0)),
                  pl.BlockSpec((1, d), lambda i, f: (0, 0)),
                  pl.BlockSpec((None, 1, 6 * d), lambda i, f: (i // tpb, 0, 0))],
        out_specs=pl.BlockSpec((tm, d), lambda i, f: (i, 0)),
        out_shape=jax.ShapeDtypeStruct((t, d), F32),
        scratch_shapes=[pltpu.VMEM((tm, d), F32)],
        compiler_params=_cparams(("arbitrary", "arbitrary")),
        name="dense_swiglu",
    )(h, wg, wu, wd, x2, g_post.reshape(1, d), mod_l)


def _route_kernel(logit_ref, tri_ref, info_ref, info_t_ref, count_ref, carry_ref):
    @pl.when(pl.program_id(0) == 0)
    def _():
        carry_ref[...] = jnp.zeros_like(carry_ref)

    lg = logit_ref[...]
    lane = lax.broadcasted_iota(jnp.int32, lg.shape, 1)
    lanef = lane.astype(F32)
    valid = lane < N_EXPERTS
    big = float(LANES)
    lg = jnp.where(valid, lg, -jnp.inf)
    m1 = jnp.max(lg, axis=-1, keepdims=True)
    e1 = jnp.min(jnp.where(lg == m1, lanef, big), axis=-1, keepdims=True)
    lg2 = jnp.where(lanef == e1, -jnp.inf, lg)
    m2 = jnp.max(lg2, axis=-1, keepdims=True)
    e2 = jnp.min(jnp.where(lg2 == m2, lanef, big), axis=-1, keepdims=True)
    z = jnp.exp(m2 - m1)
    w1 = 1.0 / (1.0 + z)
    w2 = z / (1.0 + z)
    oh1 = (lanef == e1).astype(F32)
    oh2 = (lanef == e2).astype(F32)
    both = oh1 + oh2
    before = _dot(tri_ref[...], both.astype(BF16)) + carry_ref[0:1, :]
    r1 = jnp.sum(before * oh1, axis=-1, keepdims=True)
    r2 = jnp.sum(before * oh2, axis=-1, keepdims=True)
    carry_ref[0:1, :] = carry_ref[0:1, :] + jnp.sum(both, axis=0, keepdims=True)
    count_ref[...] = carry_ref[...]
    cols = (e1, e2, r1, r2, w1, w2)
    info = jnp.zeros(lg.shape, F32)
    for idx, col in enumerate(cols):
        info = jnp.where(lane == idx, col, info)
    info_ref[...] = info
    info_t_ref[...] = jnp.transpose(info)[0:8, :]


def _route(logits):
    t = logits.shape[0]
    tm = _tile(t, 512)
    tri = jnp.asarray(np.tril(np.ones((tm, tm), np.float32), -1), BF16)
    info, info_t, counts = pl.pallas_call(
        _route_kernel,
        grid=(t // tm,),
        in_specs=[pl.BlockSpec((tm, LANES), lambda i: (i, 0)), pl.BlockSpec((tm, tm), lambda i: (0, 0))],
        out_specs=[pl.BlockSpec((tm, LANES), lambda i: (i, 0)), pl.BlockSpec((8, tm), lambda i: (0, i)),
                   pl.BlockSpec((8, LANES), lambda i: (0, 0))],
        out_shape=[jax.ShapeDtypeStruct((t, LANES), F32), jax.ShapeDtypeStruct((8, t), F32),
                   jax.ShapeDtypeStruct((8, LANES), F32)],
        scratch_shapes=[pltpu.VMEM((8, LANES), F32)],
        compiler_params=_cparams(("arbitrary",)),
        name="moe_route",
    )(logits, tri)
    return info, info_t, counts[0, :N_EXPERTS]


def _dispatch_kernel(d1_ref, d2_ref, h_ref, xin_ref, xbuf_ref, sem, *, tm):
    del xin_ref

    def row_copy(r, dref):
        return pltpu.make_async_copy(h_ref.at[pl.ds(r, 1), :], xbuf_ref.at[pl.ds(dref[0, 0, r], 1), :], sem)

    def start(g, carry):
        for j in range(DMA_UNROLL):
            r = g * DMA_UNROLL + j
            row_copy(r, d1_ref).start(priority=0)
            row_copy(r, d2_ref).start(priority=1)
        return carry

    lax.fori_loop(0, tm // DMA_UNROLL, start, 0)
    for _ in range(TOP_K):
        pltpu.make_async_copy(h_ref, xbuf_ref.at[pl.ds(0, tm), :], sem).wait()


def _dispatch(h, dest1, dest2, n_rows):
    t, d = h.shape
    tm = _tile(t, 512)
    nt = t // tm
    smem = lambda: pl.BlockSpec((1, 1, tm), lambda i: (i, 0, 0), memory_space=pltpu.SMEM)
    return pl.pallas_call(
        functools.partial(_dispatch_kernel, tm=tm),
        grid=(nt,),
        in_specs=[smem(), smem(), pl.BlockSpec((tm, d), lambda i: (i, 0)), pl.BlockSpec(memory_space=pl.ANY)],
        out_specs=pl.BlockSpec(memory_space=pl.ANY),
        out_shape=jax.ShapeDtypeStruct((n_rows, d), h.dtype),
        scratch_shapes=[pltpu.SemaphoreType.DMA(())],
        input_output_aliases={3: 0},
        compiler_params=_cparams(("arbitrary",)),
        name="moe_dispatch",
    )(dest1.reshape(nt, 1, tm), dest2.reshape(nt, 1, tm), h, jnp.zeros((n_rows, d), h.dtype))


def _moe_kernel(be_ref, nb_ref, top_ref, x_ref, wg_ref, wu_ref, wd_ref, o_ref, xb_ref, acc_ref):
    i = pl.program_id(0)
    f = pl.program_id(1)
    active = i < nb_ref[0]
    top_half_only = top_ref[i] == 1

    @pl.when(jnp.logical_and(active, f == 0))
    def _():
        xb_ref[...] = x_ref[...].astype(BF16)
        acc_ref[...] = jnp.zeros_like(acc_ref)

    def ffn(rows):
        xb = xb_ref[rows, :]
        a = _silu(_dot(xb, wg_ref[...].astype(BF16))) * _dot(xb, wu_ref[...].astype(BF16))
        acc_ref[rows, :] += _dot(a.astype(BF16), wd_ref[...].astype(BF16))

    @pl.when(jnp.logical_and(active, jnp.logical_not(top_half_only)))
    def _():
        ffn(pl.ds(0, xb_ref.shape[0]))

    @pl.when(jnp.logical_and(active, top_half_only))
    def _():
        ffn(pl.ds(0, xb_ref.shape[0] // 2))

    @pl.when(f == pl.num_programs(1) - 1)
    def _():
        o_ref[...] = jnp.where(active, acc_ref[...], 0.0)


def _moe_ffn(x_buf, block_expert, n_active, top_half_only, wg, wu, wd):
    n_rows, d = x_buf.shape
    ff = wg.shape[2]
    tm = MOE_TILE
    tf = _tile(ff, FF_TILE)
    nf = ff // tf

    def wmap(i, f, be, nb):
        return be[i], jnp.where(i < nb[0], f, nf - 1)

    grid_spec = pltpu.PrefetchScalarGridSpec(
        num_scalar_prefetch=3,
        grid=(n_rows // tm, nf),
        in_specs=[pl.BlockSpec((tm, d), lambda i, f, be, nb, top: (jnp.minimum(i, nb[0] - 1), 0)),
                  pl.BlockSpec((None, d, tf),
                               lambda i, f, be, nb, top: (wmap(i, f, be, nb)[0], 0, wmap(i, f, be, nb)[1])),
                  pl.BlockSpec((None, d, tf),
                               lambda i, f, be, nb, top: (wmap(i, f, be, nb)[0], 0, wmap(i, f, be, nb)[1])),
                  pl.BlockSpec((None, tf, d),
                               lambda i, f, be, nb, top: (wmap(i, f, be, nb)[0], wmap(i, f, be, nb)[1], 0))],
        out_specs=pl.BlockSpec((tm, d), lambda i, f, be, nb, top: (i, 0)),
        scratch_shapes=[pltpu.VMEM((tm, d), BF16), pltpu.VMEM((tm, d), F32)],
    )
    return pl.pallas_call(
        _moe_kernel,
        grid_spec=grid_spec,
        out_shape=jax.ShapeDtypeStruct((n_rows, d), F32),
        compiler_params=_cparams(("arbitrary", "arbitrary")),
        name="moe_grouped_swiglu",
    )(block_expert, n_active, top_half_only, x_buf, wg, wu, wd)


def _combine_kernel(d1_ref, d2_ref, d1n_ref, d2n_ref, ybuf_ref, info_ref, x_ref, g_ref, mod_ref, o_ref,
                    y1_ref, y2_ref, sem, *, tm):
    i = pl.program_id(0)
    slot = i % 2

    def gather(da_ref, db_ref, s):
        def start(g, carry):
            for j in range(DMA_UNROLL):
                r = g * DMA_UNROLL + j
                pltpu.make_async_copy(ybuf_ref.at[pl.ds(da_ref[0, 0, r], 1), :], y1_ref.at[s, pl.ds(r, 1), :],
                                      sem.at[s]).start(priority=0)
                pltpu.make_async_copy(ybuf_ref.at[pl.ds(db_ref[0, 0, r], 1), :], y2_ref.at[s, pl.ds(r, 1), :],
                                      sem.at[s]).start(priority=1)
            return carry

        lax.fori_loop(0, tm // DMA_UNROLL, start, 0)

    @pl.when(i == 0)
    def _():
        gather(d1_ref, d2_ref, 0)

    @pl.when(i + 1 < pl.num_programs(0))
    def _():
        gather(d1n_ref, d2n_ref, 1 - slot)

    for dst in (y1_ref, y2_ref):
        pltpu.make_async_copy(ybuf_ref.at[pl.ds(0, tm), :], dst.at[slot], sem.at[slot]).wait()
    d = D_MODEL
    info = info_ref[...]
    y = y1_ref[slot] * info[:, 4:5] + y2_ref[slot] * info[:, 5:6]
    o_ref[...] = x_ref[...] + mod_ref[:, 5 * d:6 * d] * _rms(y, g_ref[...])


def _combine(y_buf, dest1, dest2, info, x2, g_post, mod_l, seq):
    t, d = x2.shape
    tm = _tile(seq, 512)
    nt = t // tm
    tpb = seq // tm
    smem = lambda: pl.BlockSpec((1, 1, tm), lambda i: (i, 0, 0), memory_space=pltpu.SMEM)
    smem_next = lambda: pl.BlockSpec((1, 1, tm), lambda i: (jnp.minimum(i + 1, nt - 1), 0, 0),
                                     memory_space=pltpu.SMEM)
    dest1, dest2 = dest1.reshape(nt, 1, tm), dest2.reshape(nt, 1, tm)
    return pl.pallas_call(
        functools.partial(_combine_kernel, tm=tm),
        grid=(nt,),
        in_specs=[smem(), smem(), smem_next(), smem_next(), pl.BlockSpec(memory_space=pl.ANY),
                  pl.BlockSpec((tm, LANES), lambda i: (i, 0)),
                  pl.BlockSpec((tm, d), lambda i: (i, 0)),
                  pl.BlockSpec((1, d), lambda i: (0, 0)),
                  pl.BlockSpec((None, 1, 6 * d), lambda i: (i // tpb, 0, 0))],
        out_specs=pl.BlockSpec((tm, d), lambda i: (i, 0)),
        out_shape=jax.ShapeDtypeStruct((t, d), F32),
        scratch_shapes=[pltpu.VMEM((2, tm, d), F32), pltpu.VMEM((2, tm, d), F32), pltpu.SemaphoreType.DMA((2,))],
        compiler_params=_cparams(("arbitrary",)),
        name="moe_combine",
    )(dest1, dest2, dest1, dest2, y_buf, info, x2, g_post.reshape(1, d), mod_l)


def _routed_ffn(h, logits, wg, wu, wd, x2, g_post, mod_l, seq):
    t, d = h.shape
    tm = MOE_TILE
    info, info_t, counts = _route(logits)
    counts = counts.astype(jnp.int32)
    padded = ((counts + tm - 1) // tm) * tm
    pad_end = jnp.cumsum(padded)
    pad_start = pad_end - padded
    fields = info_t[0:4].astype(jnp.int32)
    dest1 = pad_start[fields[0]] + fields[2]
    dest2 = pad_start[fields[1]] + fields[3]
    n_rows = t * TOP_K + N_EXPERTS * tm
    n_blocks = n_rows // tm
    block_start = jnp.arange(n_blocks, dtype=jnp.int32) * tm
    block_expert = jnp.minimum(jnp.sum(pad_end[None, :] <= block_start[:, None], axis=1), N_EXPERTS - 1).astype(jnp.int32)
    n_active = (pad_end[-1] // tm).astype(jnp.int32).reshape(1)
    rows_used = (pad_start + counts)[block_expert] - block_start
    top_half_only = (rows_used <= tm // 2).astype(jnp.int32)
    x_buf = _dispatch(h, dest1, dest2, n_rows)
    y_buf = _moe_ffn(x_buf, block_expert, n_active, top_half_only, wg, wu, wd)
    return _combine(y_buf, dest1, dest2, info, x2, g_post, mod_l, seq)


def kernel(x, c, positions, ada_w, ada_b, norm_pre_mix, norm_post_mix, norm_pre_ffn, norm_post_ffn, w_in, ret_norm, ssm_a_re, ssm_a_im, ssm_b_re, ssm_b_im, ssm_c_re, ssm_c_im, ssm_d, ssm_log_dt, ssm_glu_w, ssm_glu_b, ssm_norm, mla_q_norm, mla_w_uq, mla_kv_norm, mla_w_ukv, mla_norm, w_out, ffn_w_gate, ffn_w_up, ffn_w_down, moe_router, moe_w_gate, moe_w_up, moe_w_down):
    bsz, seq, d = x.shape
    depth = ada_w.shape[0]
    assert d == D_MODEL and seq % SSM_CHUNK == 0 and seq % RET_CHUNK == 0
    t = bsz * seq
    x2 = x.reshape(t, d)
    mod = _modulation(c, ada_w, ada_b)
    cos128, sin128 = _rope_tables(positions)
    ssm_params = _ssm_params(ssm_a_re, ssm_a_im, ssm_b_re, ssm_b_im, ssm_c_re, ssm_c_im, ssm_log_dt)
    for layer in range(depth):
        mod_l = mod[layer]
        r, u, cq, ckv, kr = _inproj(x2, norm_pre_mix[layer], mod_l, w_in[layer], seq)
        ret = _retention(r, cos128, sin128, ret_norm[layer], bsz, seq)
        ssm = _ssm_post(_ssm_scan(u, ssm_params, layer, seq), u, ssm_d[layer], ssm_glu_w[layer], ssm_glu_b[layer],
                        ssm_norm[layer])
        q3, k3, v3 = _mla_prep(cq, ckv, kr, cos128, sin128, mla_q_norm[layer], mla_w_uq[layer],
                               mla_kv_norm[layer], mla_w_ukv[layer])
        att = _attention(q3, k3, v3, bsz, seq)
        j = layer // 2
        if layer % 2 == 0:
            x2, h = _outproj(x2, ret, ssm, att, mla_norm[layer], w_out[layer], norm_post_mix[layer], mod_l,
                             norm_pre_ffn[layer], seq)
            x2 = _dense_ffn(h, ffn_w_gate[j], ffn_w_up[j], ffn_w_down[j], x2, norm_post_ffn[layer], mod_l, seq)
        else:
            x2, h, logits = _outproj(x2, ret, ssm, att, mla_norm[layer], w_out[layer], norm_post_mix[layer], mod_l,
                                     norm_pre_ffn[layer], seq, router=moe_router[j])
            x2 = _routed_ffn(h, logits, moe_w_gate[j], moe_w_up[j], moe_w_down[j], x2, norm_post_ffn[layer],
                             mod_l, seq)
    return x2.reshape(bsz, seq, d)
```

```python
import functools
import math

import numpy as np
import jax
import jax.numpy as jnp
from jax import lax
from jax.experimental import pallas as pl
from jax.experimental.pallas import tpu as pltpu

D_MODEL = 1024
RET_HEADS = 4
RET_HEAD_DIM = 64
RET_WIDTH = RET_HEADS * RET_HEAD_DIM
RET_CHUNK = 128
SSM_GROUP_CH = 16
SSM_GROUPS = 16
SSM_WIDTH = SSM_GROUPS * SSM_GROUP_CH
SSM_STATE = 64
MLA_HEADS = 4
MLA_NOPE = 128
MLA_ROPE = 64
MLA_V = 128
MLA_QK = MLA_NOPE + MLA_ROPE
MLA_WIDTH = MLA_HEADS * MLA_V
MLA_Q_RANK = 256
MLA_KV_RANK = 128
ROPE_DIM = 64
ROPE_BASE = 10000.0
D_FF = 3584
N_EXPERTS = 8
TOP_K = 2
EPS = 1e-6
NEG_INF = -1e30
IN_COLS = 4 * RET_WIDTH + SSM_WIDTH + MLA_Q_RANK + MLA_KV_RANK + MLA_ROPE

LANES = 128
VMEM_LIMIT_BYTES = 56 * 1024 * 1024

SSM_CHUNK = 64
MOE_TILE = 1024
FF_TILE = 512
DMA_UNROLL = 32

BF16 = jnp.bfloat16
F32 = jnp.float32


def _cparams(sem):
    return pltpu.CompilerParams(dimension_semantics=sem, vmem_limit_bytes=VMEM_LIMIT_BYTES)


def _tile(n, pref):
    t = min(n, pref)
    while n % t:
        t //= 2
    return t


def _dot(a, b):
    return jnp.dot(a, b, preferred_element_type=F32)


def _dot3(a, b):
    a_hi = a.astype(BF16)
    a_lo = (a - a_hi.astype(F32)).astype(BF16)
    b_hi = b.astype(BF16)
    b_lo = (b - b_hi.astype(F32)).astype(BF16)
    return _dot(a_hi, b_hi) + (_dot(a_hi, b_lo) + _dot(a_lo, b_hi))


def _rms(x, g):
    return x * lax.rsqrt(jnp.mean(x * x, axis=-1, keepdims=True) + EPS) * g


def _silu(x):
    return x * (1.0 / (1.0 + jnp.exp(-x)))


def _rope128(x, cos, sin_signed):
    lane = lax.broadcasted_iota(jnp.int32, x.shape, 1)
    first_half = (lane % ROPE_DIM) < (ROPE_DIM // 2)
    partner = jnp.where(first_half, pltpu.roll(x, LANES - ROPE_DIM // 2, 1), pltpu.roll(x, ROPE_DIM // 2, 1))
    return x * cos + partner * sin_signed


def _mod_kernel(c_ref, w_ref, b_ref, o_ref):
    cond = _silu(c_ref[...])
    o_ref[...] = _dot3(cond, w_ref[...]) + b_ref[...]


def _modulation(c, ada_w, ada_b):
    depth, d, n = ada_w.shape
    bsz = c.shape[0]
    rows = 8
    c_pad = jnp.zeros((rows, d), F32).at[:bsz].set(c)
    tn = _tile(n, 1536)
    out = pl.pallas_call(
        _mod_kernel,
        grid=(depth, n // tn),
        in_specs=[pl.BlockSpec((rows, d), lambda l, j: (0, 0)),
                  pl.BlockSpec((None, d, tn), lambda l, j: (l, 0, j)),
                  pl.BlockSpec((None, 1, tn), lambda l, j: (l, 0, j))],
        out_specs=pl.BlockSpec((None, rows, tn), lambda l, j: (l, 0, j)),
        out_shape=jax.ShapeDtypeStruct((depth, rows, n), F32),
        compiler_params=_cparams(("arbitrary", "arbitrary")),
        name="adaln_mod",
    )(c_pad, ada_w, ada_b.reshape(depth, 1, n))
    return out[:, :bsz].reshape(depth, bsz, 1, n)


def _rope_kernel(pos_ref, inv_ref, cos_ref, sin_ref):
    pos = pos_ref[...].astype(F32)
    lane = lax.broadcasted_iota(jnp.int32, cos_ref.shape, 1)
    half = ROPE_DIM // 2
    p = jnp.where(lane < half, pos[:, 0:1],
                  jnp.where(lane < 2 * half, pos[:, 1:2],
                            jnp.where(lane < 3 * half, pos[:, 2:3], pos[:, 3:4])))
    ang = p * inv_ref[...]
    cos_ref[...] = jnp.cos(ang)
    sin_ref[...] = jnp.sin(ang)


def _rope_tables(positions):
    t = positions.size
    half = ROPE_DIM // 2
    per_row = LANES // half
    inv = ROPE_BASE ** (-jnp.arange(0, ROPE_DIM, 2, dtype=F32) / ROPE_DIM)
    inv_row = jnp.tile(inv, per_row).reshape(1, LANES)
    rows = t // per_row
    tr = _tile(rows, 1024)
    cos, sin = pl.pallas_call(
        _rope_kernel,
        grid=(rows // tr,),
        in_specs=[pl.BlockSpec((tr, per_row), lambda i: (i, 0)),
                  pl.BlockSpec((1, LANES), lambda i: (0, 0))],
        out_specs=[pl.BlockSpec((tr, LANES), lambda i: (i, 0))] * 2,
        out_shape=[jax.ShapeDtypeStruct((rows, LANES), F32)] * 2,
        compiler_params=_cparams(("arbitrary",)),
        name="rope_tables",
    )(positions.reshape(rows, per_row), inv_row)
    cos = cos.reshape(t, half)
    sin = sin.reshape(t, half)
    return jnp.tile(cos, (1, per_row)), jnp.concatenate([-sin, sin, -sin, sin], axis=1)


def _inproj_kernel(x_ref, g_ref, mod_ref, w_ref, ret_ref, u_ref, cq_ref, ckv_ref, kr_ref):
    d = D_MODEL
    x = x_ref[...]
    h = _rms(x, g_ref[...]) * (1.0 + mod_ref[:, d:2 * d]) + mod_ref[:, 0:d]
    p = _dot(h.astype(BF16), w_ref[...])
    o = 4 * RET_WIDTH
    ret_ref[...] = p[:, :o].astype(BF16)
    u_ref[...] = p[:, o:o + SSM_WIDTH].astype(BF16)
    o += SSM_WIDTH
    cq_ref[...] = p[:, o:o + MLA_Q_RANK].astype(BF16)
    o += MLA_Q_RANK
    ckv_ref[...] = p[:, o:o + MLA_KV_RANK].astype(BF16)
    o += MLA_KV_RANK
    kr_ref[...] = p[:, o:o + MLA_ROPE].astype(BF16)


def _inproj(x2, g, mod_l, w_in, seq):
    t, d = x2.shape
    tm = _tile(seq, 512)
    tpb = seq // tm
    widths = (4 * RET_WIDTH, SSM_WIDTH, MLA_Q_RANK, MLA_KV_RANK, MLA_ROPE)
    return pl.pallas_call(
        _inproj_kernel,
        grid=(t // tm,),
        in_specs=[pl.BlockSpec((tm, d), lambda i: (i, 0)),
                  pl.BlockSpec((1, d), lambda i: (0, 0)),
                  pl.BlockSpec((None, 1, 6 * d), lambda i: (i // tpb, 0, 0)),
                  pl.BlockSpec((d, IN_COLS), lambda i: (0, 0))],
        out_specs=[pl.BlockSpec((tm, w), lambda i: (i, 0)) for w in widths],
        out_shape=[jax.ShapeDtypeStruct((t, w), BF16) for w in widths],
        compiler_params=_cparams(("arbitrary",)),
        name="mixer_inproj",
    )(x2, g.reshape(1, d), mod_l, w_in.astype(BF16))


def _retention_kernel(r_ref, cos_ref, sin_ref, dmask_ref, qdec_ref, kdec_ref, cdec_ref, bd_ref, gn_ref,
                      o_ref, state_ref, *, chunks):
    w = RET_WIDTH

    @pl.when(pl.program_id(1) == 0)
    def _():
        state_ref[...] = jnp.zeros_like(state_ref)

    lane = lax.broadcasted_iota(jnp.int32, (RET_CHUNK, w), 1)
    bd = bd_ref[...]
    for c in range(chunks):
        rows = pl.ds(c * RET_CHUNK, RET_CHUNK)
        cos = cos_ref[rows, :]
        sin = sin_ref[rows, :]

        def rope(v):
            return jnp.concatenate([_rope128(v[:, :LANES], cos, sin), _rope128(v[:, LANES:], cos, sin)], axis=1)

        q = rope(r_ref[rows, 0:w].astype(F32))
        k = rope(r_ref[rows, w:2 * w].astype(F32)) * (RET_HEAD_DIM ** -0.5)
        v = r_ref[rows, 2 * w:3 * w]
        gate = r_ref[rows, 3 * w:4 * w].astype(F32)
        kb = k.astype(BF16)
        state = state_ref[...]
        y = _dot((q * qdec_ref[...]).astype(BF16), state.astype(BF16))
        for h in range(RET_HEADS):
            in_head = (lane // RET_HEAD_DIM) == h
            qh = jnp.where(in_head, q, 0.0).astype(BF16)
            s = lax.dot_general(qh, kb, (((1,), (1,)), ((), ())), preferred_element_type=F32)
            s = s * dmask_ref[h]
            y = y + jnp.where(in_head, _dot(s.astype(BF16), v), 0.0)
        kv = lax.dot_general((k * kdec_ref[...]).astype(BF16), v, (((0,), (0,)), ((), ())),
                             preferred_element_type=F32)
        state_ref[...] = cdec_ref[...] * state + bd * kv
        y2 = y * y
        y2_hi = y2.astype(BF16)
        y2_lo = (y2 - y2_hi.astype(F32)).astype(BF16)
        bdb = bd.astype(BF16)
        ms = (_dot(y2_hi, bdb) + _dot(y2_lo, bdb)) * (1.0 / RET_HEAD_DIM)
        yn = y * lax.rsqrt(ms + EPS) * gn_ref[...]
        o_ref[rows, :] = (_silu(gate) * yn).astype(BF16)


def _retention_consts():
    c = RET_CHUNK
    log_g = np.log1p(-(2.0 ** (-5.0 - np.arange(RET_HEADS, dtype=np.float32)))).astype(np.float32)
    i = np.arange(c, dtype=np.float32)
    diff = i[:, None] - i[None, :]
    dmask = np.where(diff >= 0, np.exp(log_g[:, None, None] * np.maximum(diff, 0.0)), 0.0).astype(np.float32)
    k_dec = np.exp(log_g[None, :] * (c - 1 - i)[:, None]).astype(np.float32)
    q_dec = np.exp(log_g[None, :] * (i + 1.0)[:, None]).astype(np.float32)
    c_dec = np.exp(log_g * c).astype(np.float32)
    rep = lambda a: np.repeat(a, RET_HEAD_DIM, axis=-1)
    head = np.arange(RET_WIDTH) // RET_HEAD_DIM
    bd = (head[:, None] == head[None, :]).astype(np.float32)
    return dmask, rep(q_dec), rep(k_dec), rep(c_dec[None, :]), bd


def _retention(r, cos128, sin128, ret_norm, bsz, seq):
    t = r.shape[0]
    w = RET_WIDTH
    tr = _tile(seq, 512)
    chunks = tr // RET_CHUNK
    nt = seq // tr
    dmask, qdec, kdec, cdec, bd = (jnp.asarray(a) for a in _retention_consts())
    const = lambda shape: pl.BlockSpec(shape, lambda b, i: (0,) * len(shape))
    return pl.pallas_call(
        functools.partial(_retention_kernel, chunks=chunks),
        grid=(bsz, nt),
        in_specs=[pl.BlockSpec((tr, 4 * w), lambda b, i: (b * nt + i, 0)),
                  pl.BlockSpec((tr, LANES), lambda b, i: (b * nt + i, 0)),
                  pl.BlockSpec((tr, LANES), lambda b, i: (b * nt + i, 0)),
                  const((RET_HEADS, RET_CHUNK, RET_CHUNK)),
                  const((RET_CHUNK, w)), const((RET_CHUNK, w)), const((1, w)), const((w, w)), const((1, w))],
        out_specs=pl.BlockSpec((tr, w), lambda b, i: (b * nt + i, 0)),
        out_shape=jax.ShapeDtypeStruct((t, w), BF16),
        scratch_shapes=[pltpu.VMEM((w, w), F32)],
        compiler_params=_cparams(("arbitrary", "arbitrary")),
        name="retention",
    )(r, cos128, sin128, dmask, qdec, kdec, cdec, bd, ret_norm.reshape(1, w))


def _expand(x, sel):
    hi = x.astype(BF16)
    lo = (x - hi.astype(F32)).astype(BF16)
    return _dot(hi, sel) + _dot(lo, sel)


def _ssm_kernel(u_ref, acol_ref, arow_ref, ldt_ref, b_ref, bt_ref, ct_ref, esel_ref, ech_ref, y_ref, toep_ref,
                *, chunks_per_seq, steps):
    tc, hg, p = SSM_CHUNK, SSM_GROUP_CH, SSM_STATE
    k = tc * hg
    dt = jnp.exp(ldt_ref[...])
    ar, ai = acol_ref[:, 0:1], acol_ref[:, 1:2]
    n = lax.broadcasted_iota(jnp.int32, (p, LANES), 1).astype(F32)
    mag = jnp.exp(ar * dt * n)
    pw_re, pw_im = mag * jnp.cos(ai * dt * n), mag * jnp.sin(ai * dt * n)
    def zoh(a_r, a_i):
        m1 = jnp.exp(a_r * dt)
        l_r, l_i = m1 * jnp.cos(a_i * dt), m1 * jnp.sin(a_i * dt)
        den = a_r * a_r + a_i * a_i
        return ((l_r - 1.0) * a_r + l_i * a_i) / den, (l_i * a_r - (l_r - 1.0) * a_i) / den

    f_re, f_im = zoh(ar, ai)
    bb_re = f_re * b_ref[0] - f_im * b_ref[1]
    bb_im = f_re * b_ref[1] + f_im * b_ref[0]
    fr_re, fr_im = zoh(arow_ref[0:1, 0:p], arow_ref[1:2, 0:p])
    bbt_re = fr_re * bt_ref[0] - fr_im * bt_ref[1]
    bbt_im = fr_re * bt_ref[1] + fr_im * bt_ref[0]
    ech = ech_ref[...]
    pt_re, pt_im = _expand(pw_re, esel_ref[0]), _expand(pw_im, esel_ref[0])
    p1_re, p1_im = _expand(pw_re, esel_ref[1]), _expand(pw_im, esel_ref[1])
    pr_re, pr_im = _expand(pw_re, esel_ref[2]), _expand(pw_im, esel_ref[2])
    bbr, bbi = _expand(bb_re, ech), _expand(bb_im, ech)
    cr, ci = _expand(ct_ref[0], ech), _expand(ct_ref[1], ech)
    z_re = cr * pt_re - ci * pt_im
    z_im = cr * pt_im + ci * pt_re
    strip = _dot3(bbt_re, z_re) - _dot3(bbt_im, z_im)
    w_in_t = jnp.concatenate([pr_re * bbr - pr_im * bbi, pr_re * bbi + pr_im * bbr], axis=0).astype(BF16)
    v_out = jnp.concatenate([cr * p1_re - ci * p1_im, -(cr * p1_im + ci * p1_re)], axis=0).astype(BF16)
    padded = jnp.concatenate([jnp.zeros((hg, k), F32), strip], axis=1)
    for s in range(tc):
        toep_ref[s * hg:(s + 1) * hg, :] = padded[:, k - s * hg:2 * k - s * hg].astype(BF16)
    dist = (tc * (1 << lax.broadcasted_iota(jnp.int32, (steps, 2 * p), 0))).astype(F32)
    lane = lax.broadcasted_iota(jnp.int32, (steps, 2 * p), 1)
    amag = jnp.exp(arow_ref[0:1, :] * dt * dist)
    a_cos = amag * jnp.cos(arow_ref[1:2, :] * dt * dist)
    a_sin = jnp.where(lane < p, -1.0, 1.0) * amag * jnp.sin(arow_ref[1:2, :] * dt * dist)

    u = u_ref[...]
    x = lax.dot_general(u, w_in_t, (((1,), (1,)), ((), ())), preferred_element_type=F32)
    row = lax.broadcasted_iota(jnp.int32, x.shape, 0) % chunks_per_seq

    def shift_rows(a, d):
        return jnp.where(row >= d, pltpu.roll(a, d, 0), 0.0)

    for j in range(steps):
        prev = shift_rows(x, 1 << j)
        x = x + prev * a_cos[j:j + 1, :] + pltpu.roll(prev, p, 1) * a_sin[j:j + 1, :]
    x_in = shift_rows(x, 1)
    y_ref[...] = (_dot(u, toep_ref[...]) + _dot(x_in.astype(BF16), v_out)).astype(y_ref.dtype)


def _ssm_params(a_re, a_im, b_re, b_im, c_re, c_im, log_dt):
    a_col = jnp.stack([a_re, a_im], axis=-1)
    a_row = jnp.stack([jnp.concatenate([a_re, a_re], -1), jnp.concatenate([a_im, a_im], -1)], axis=2)
    b = jnp.stack([b_re, b_im], axis=2)
    bt = jnp.swapaxes(b, -1, -2)
    ct = jnp.swapaxes(jnp.stack([c_re, c_im], axis=2), -1, -2)
    return a_col, a_row, log_dt[..., None, None], b, bt, ct


def _ssm_selectors():
    tc, hg = SSM_CHUNK, SSM_GROUP_CH
    t_of_lane = np.arange(tc * hg) // hg
    n = np.arange(LANES)[:, None]
    esel = np.stack([n == t_of_lane, n == t_of_lane + 1, n == tc - 1 - t_of_lane]).astype(np.float32)
    ech = (np.arange(hg)[:, None] == (np.arange(tc * hg) % hg)[None, :]).astype(np.float32)
    return jnp.asarray(esel, BF16), jnp.asarray(ech, BF16)


def _ssm_scan(u, params, layer, seq):
    t = u.shape[0]
    g, hg, tc, p = SSM_GROUPS, SSM_GROUP_CH, SSM_CHUNK, SSM_STATE
    nc = t // tc
    cps = seq // tc
    steps = max(1, int(math.ceil(math.log2(cps))))
    k = tc * hg
    esel, ech = _ssm_selectors()
    ug = u.reshape(nc, tc, g, hg).transpose(2, 0, 1, 3).reshape(g, nc, k)
    per_group = lambda *blk: pl.BlockSpec((None, None) + blk, lambda i: (layer, i) + (0,) * len(blk))
    y = pl.pallas_call(
        functools.partial(_ssm_kernel, chunks_per_seq=cps, steps=steps),
        grid=(g,),
        in_specs=[pl.BlockSpec((None, nc, k), lambda i: (i, 0, 0)),
                  per_group(p, 2), per_group(2, 2 * p), per_group(1, 1),
                  per_group(2, p, hg), per_group(2, hg, p), per_group(2, p, hg),
                  pl.BlockSpec((3, LANES, k), lambda i: (0, 0, 0)),
                  pl.BlockSpec((hg, k), lambda i: (0, 0))],
        out_specs=pl.BlockSpec((None, nc, k), lambda i: (i, 0, 0)),
        out_shape=jax.ShapeDtypeStruct((g, nc, k), BF16),
        scratch_shapes=[pltpu.VMEM((k, k), BF16)],
        compiler_params=_cparams(("arbitrary",)),
        name="s5_chunk_scan",
    )(ug, *params, esel, ech)
    return y.reshape(g, nc, tc, hg).transpose(1, 2, 0, 3).reshape(t, g * hg)


def _ssm_post_kernel(y_ref, u_ref, d_ref, gw_ref, gb_ref, gn_ref, o_ref):
    y = y_ref[...].astype(F32) + d_ref[...] * u_ref[...].astype(F32)
    y = 0.5 * y * (1.0 + jnp.tanh(math.sqrt(2.0 / math.pi) * (y + 0.044715 * (y * y * y))))
    z = _dot(y.astype(BF16), gw_ref[...]) + gb_ref[...]
    y = y * (1.0 / (1.0 + jnp.exp(-z)))
    o_ref[...] = _rms(y, gn_ref[...]).astype(BF16)


def _ssm_post(y, u, d, glu_w, glu_b, ssm_norm):
    t, w = y.shape
    tm = _tile(t, 1024)
    row = lambda: pl.BlockSpec((1, w), lambda i: (0, 0))
    return pl.pallas_call(
        _ssm_post_kernel,
        grid=(t // tm,),
        in_specs=[pl.BlockSpec((tm, w), lambda i: (i, 0)), pl.BlockSpec((tm, w), lambda i: (i, 0)),
                  row(), pl.BlockSpec((w, w), lambda i: (0, 0)), row(), row()],
        out_specs=pl.BlockSpec((tm, w), lambda i: (i, 0)),
        out_shape=jax.ShapeDtypeStruct((t, w), BF16),
        compiler_params=_cparams(("arbitrary",)),
        name="s5_gelu_glu_norm",
    )(y, u, d.reshape(1, w), glu_w.astype(BF16), glu_b.reshape(1, w), ssm_norm.reshape(1, w))


def _mla_prep_kernel(cq_ref, ckv_ref, kr_ref, cos_ref, sin_ref, qn_ref, kvn_ref, wq_ref, wkv_ref,
                     q_ref, k_ref, v_ref):
    cos = cos_ref[...]
    sin = sin_ref[...]
    nope_w = MLA_HEADS * MLA_NOPE
    q = _dot(_rms(cq_ref[...].astype(F32), qn_ref[...]).astype(BF16), wq_ref[...])
    q = q * (MLA_QK ** -0.5 * math.log2(math.e))
    q_rope = jnp.concatenate([_rope128(q[:, nope_w:nope_w + LANES], cos, sin),
                              _rope128(q[:, nope_w + LANES:], cos, sin)], axis=1)
    kv = _dot(_rms(ckv_ref[...].astype(F32), kvn_ref[...]).astype(BF16), wkv_ref[...])
    kr = kr_ref[...].astype(F32)
    k_pe = _rope128(jnp.concatenate([kr, kr], axis=1), cos, sin)[:, :MLA_ROPE]
    for h in range(MLA_HEADS):
        q_ref[h] = jnp.concatenate([q[:, h * MLA_NOPE:(h + 1) * MLA_NOPE],
                                    q_rope[:, h * MLA_ROPE:(h + 1) * MLA_ROPE]], axis=1).astype(BF16)
        k_ref[h] = jnp.concatenate([kv[:, h * MLA_NOPE:(h + 1) * MLA_NOPE], k_pe], axis=1).astype(BF16)
        v_ref[h, :, 0:MLA_V] = kv[:, nope_w + h * MLA_V:nope_w + (h + 1) * MLA_V].astype(BF16)
        v_ref[h, :, MLA_V:] = jnp.ones((kv.shape[0], LANES), BF16)


def _mla_prep(cq, ckv, kr, cos128, sin128, q_norm, w_uq, kv_norm, w_ukv):
    t = cq.shape[0]
    tm = _tile(t, 512)
    hh = MLA_HEADS
    wq = w_uq.reshape(MLA_Q_RANK, hh, MLA_QK)
    wq = jnp.concatenate([wq[:, :, :MLA_NOPE].reshape(MLA_Q_RANK, -1), wq[:, :, MLA_NOPE:].reshape(MLA_Q_RANK, -1)], 1)
    wkv = w_ukv.reshape(MLA_KV_RANK, hh, MLA_NOPE + MLA_V)
    wkv = jnp.concatenate([wkv[:, :, :MLA_NOPE].reshape(MLA_KV_RANK, -1), wkv[:, :, MLA_NOPE:].reshape(MLA_KV_RANK, -1)], 1)
    tok = lambda w: pl.BlockSpec((tm, w), lambda i: (i, 0))
    full = lambda a, b: pl.BlockSpec((a, b), lambda i: (0, 0))
    return pl.pallas_call(
        _mla_prep_kernel,
        grid=(t // tm,),
        in_specs=[tok(MLA_Q_RANK), tok(MLA_KV_RANK), tok(MLA_ROPE), tok(LANES), tok(LANES),
                  full(1, MLA_Q_RANK), full(1, MLA_KV_RANK),
                  full(MLA_Q_RANK, hh * MLA_QK), full(MLA_KV_RANK, hh * (MLA_NOPE + MLA_V))],
        out_specs=[pl.BlockSpec((hh, tm, MLA_QK), lambda i: (0, i, 0)),
                   pl.BlockSpec((hh, tm, MLA_QK), lambda i: (0, i, 0)),
                   pl.BlockSpec((hh, tm, MLA_V + LANES), lambda i: (0, i, 0))],
        out_shape=[jax.ShapeDtypeStruct((hh, t, MLA_QK), BF16), jax.ShapeDtypeStruct((hh, t, MLA_QK), BF16),
                   jax.ShapeDtypeStruct((hh, t, MLA_V + LANES), BF16)],
        compiler_params=_cparams(("arbitrary",)),
        name="mla_prep",
    )(cq, ckv, kr, cos128, sin128, q_norm.reshape(1, -1), kv_norm.reshape(1, -1), wq.astype(BF16), wkv.astype(BF16))


def _attn_kernel(q_ref, k_ref, v_ref, o_ref, m_ref, l_ref, acc_ref, sa_ref, sb_ref, *, tq, th, tk):
    qi = pl.program_id(2)
    m_ref[...] = jnp.full_like(m_ref, NEG_INF)
    l_ref[...] = jnp.zeros_like(l_ref)
    acc_ref[...] = jnp.zeros_like(acc_ref)

    halves = (0, 1)
    assert tq == 2 * th and tk == th

    def scores(s_ref, j, which=halves):
        k = k_ref[pl.ds(pl.multiple_of(j * tk, tk), tk), :]
        for half in which:
            rows = pl.ds(half * th, th)
            s_ref[rows, :] = lax.dot_general(q_ref[rows, :], k, (((1,), (1,)), ((), ())),
                                             preferred_element_type=F32)

    def consume(s_ref, j, diag_half=None):
        v = v_ref[pl.ds(pl.multiple_of(j * tk, tk), tk), :]
        for half in halves:
            if diag_half is not None and half < diag_half:
                continue
            rows = pl.ds(half * th, th)
            s = s_ref[rows, :]
            if half == diag_half:
                r = lax.broadcasted_iota(jnp.int32, s.shape, 0)
                c = lax.broadcasted_iota(jnp.int32, s.shape, 1)
                s = jnp.where(c <= r, s, NEG_INF)
            m_prev = m_ref[rows, :]
            m_new = jnp.maximum(m_prev, jnp.max(s, axis=-1, keepdims=True))
            alpha = jnp.exp2(m_prev - m_new)
            p = jnp.exp2(s - jnp.tile(m_new, (1, tk // LANES)))
            pv = _dot(p.astype(BF16), v)
            acc_ref[rows, :] = alpha * acc_ref[rows, :] + pv[:, :MLA_V]
            l_ref[rows, :] = alpha * l_ref[rows, :] + pv[:, MLA_V:]
            m_ref[rows, :] = m_new

    scores(sa_ref, 0)

    def pair(i, carry):
        scores(sb_ref, 2 * i + 1)
        consume(sa_ref, 2 * i)
        scores(sa_ref, 2 * i + 2)
        consume(sb_ref, 2 * i + 1)
        return carry

    def two_pairs(i2, carry):
        pair(2 * i2, carry)
        return pair(2 * i2 + 1, carry)

    lax.fori_loop(0, qi // 2, two_pairs, 0)

    @pl.when(qi % 2 == 1)
    def _():
        pair(qi - 1, 0)

    scores(sb_ref, 2 * qi + 1, which=(1,))
    consume(sa_ref, 2 * qi, diag_half=0)
    consume(sb_ref, 2 * qi + 1, diag_half=1)
    o_ref[...] = (acc_ref[...] / l_ref[...]).astype(o_ref.dtype)


def _attention(q3, k3, v3, bsz, seq):
    hh, t, _ = q3.shape
    tq = _tile(seq, 1024)
    th = tq // 2
    tk = th
    nq = seq // tq
    vw = v3.shape[-1]
    return pl.pallas_call(
        functools.partial(_attn_kernel, tq=tq, th=th, tk=tk),
        grid=(bsz, hh, nq),
        in_specs=[pl.BlockSpec((None, tq, MLA_QK), lambda b, h, i: (h, b * nq + i, 0)),
                  pl.BlockSpec((None, seq, MLA_QK), lambda b, h, i: (h, b, 0)),
                  pl.BlockSpec((None, seq, vw), lambda b, h, i: (h, b, 0))],
        out_specs=pl.BlockSpec((tq, MLA_V), lambda b, h, i: (b * nq + i, h)),
        out_shape=jax.ShapeDtypeStruct((t, hh * MLA_V), BF16),
        scratch_shapes=[pltpu.VMEM((tq, LANES), F32), pltpu.VMEM((tq, LANES), F32), pltpu.VMEM((tq, MLA_V), F32),
                        pltpu.VMEM((tq, tk), F32), pltpu.VMEM((tq, tk), F32)],
        compiler_params=_cparams(("arbitrary", "arbitrary", "arbitrary")),
        name="mla_flash_attention",
    )(q3, k3, v3)


def _outproj_kernel(*refs, routed):
    if routed:
        (x_ref, ret_ref, ssm_ref, att_ref, an_ref, w_ref, gpost_ref, mod_ref, gpre_ref, router_ref,
         xo_ref, h_ref, logit_ref) = refs
    else:
        x_ref, ret_ref, ssm_ref, att_ref, an_ref, w_ref, gpost_ref, mod_ref, gpre_ref, xo_ref, h_ref = refs
    d = D_MODEL
    att = _rms(att_ref[...].astype(F32), an_ref[...]).astype(BF16)
    o1 = RET_WIDTH
    o2 = RET_WIDTH + SSM_WIDTH
    y = _dot(ret_ref[...], w_ref[0:o1, :]) + _dot(ssm_ref[...], w_ref[o1:o2, :]) + _dot(att, w_ref[o2:, :])
    x = x_ref[...] + mod_ref[:, 2 * d:3 * d] * _rms(y, gpost_ref[...])
    xo_ref[...] = x
    h = _rms(x, gpre_ref[...]) * (1.0 + mod_ref[:, 4 * d:5 * d]) + mod_ref[:, 3 * d:4 * d]
    h_ref[...] = h.astype(h_ref.dtype)
    if routed:
        h_hi = h.astype(BF16)
        h_lo = (h - h_hi.astype(F32)).astype(BF16)
        hi_part = _dot(h_hi, router_ref[...])
        logit_ref[...] = hi_part[:, :LANES] + (hi_part[:, LANES:] + _dot(h_lo, router_ref[:, 0:LANES]))


def _outproj(x2, ret, ssm, att, mla_norm, w_out, g_post, mod_l, g_pre, seq, router=None):
    t, d = x2.shape
    tm = _tile(seq, 512)
    tpb = seq // tm
    routed = router is not None
    tok = lambda w: pl.BlockSpec((tm, w), lambda i: (i, 0))
    full = lambda a, b: pl.BlockSpec((a, b), lambda i: (0, 0))
    in_specs = [tok(d), tok(RET_WIDTH), tok(SSM_WIDTH), tok(MLA_WIDTH), full(1, MLA_WIDTH), full(d, d), full(1, d),
                pl.BlockSpec((None, 1, 6 * d), lambda i: (i // tpb, 0, 0)), full(1, d)]
    args = [x2, ret, ssm, att, mla_norm.reshape(1, -1), w_out.astype(BF16), g_post.reshape(1, d), mod_l,
            g_pre.reshape(1, d)]
    out_specs = [tok(d), tok(d)]
    out_shape = [jax.ShapeDtypeStruct((t, d), F32), jax.ShapeDtypeStruct((t, d), F32 if routed else BF16)]
    if routed:
        in_specs.append(full(d, 2 * LANES))
        r_pad = jnp.zeros((d, LANES), F32).at[:, :N_EXPERTS].set(router)
        r_hi = r_pad.astype(BF16)
        args.append(jnp.concatenate([r_hi, (r_pad - r_hi.astype(F32)).astype(BF16)], axis=1))
        out_specs.append(tok(LANES))
        out_shape.append(jax.ShapeDtypeStruct((t, LANES), F32))
    return pl.pallas_call(
        functools.partial(_outproj_kernel, routed=routed),
        grid=(t // tm,),
        in_specs=in_specs, out_specs=out_specs, out_shape=out_shape,
        compiler_params=_cparams(("arbitrary",)),
        name="mixer_outproj_routed" if routed else "mixer_outproj",
    )(*args)


def _ffn_kernel(h_ref, wg_ref, wu_ref, wd_ref, x_ref, g_ref, mod_ref, o_ref, acc_ref):
    f = pl.program_id(1)

    @pl.when(f == 0)
    def _():
        acc_ref[...] = jnp.zeros_like(acc_ref)

    h = h_ref[...]
    a = _silu(_dot(h, wg_ref[...].astype(BF16))) * _dot(h, wu_ref[...].astype(BF16))
    acc_ref[...] += _dot(a.astype(BF16), wd_ref[...].astype(BF16))

    @pl.when(f == pl.num_programs(1) - 1)
    def _():
        d = D_MODEL
        o_ref[...] = x_ref[...] + mod_ref[:, 5 * d:6 * d] * _rms(acc_ref[...], g_ref[...])


def _dense_ffn(h, wg, wu, wd, x2, g_post, mod_l, seq):
    t, d = h.shape
    ff = wg.shape[1]
    tm = _tile(seq, 1024)
    tf = _tile(ff, FF_TILE)
    tpb = seq // tm
    return pl.pallas_call(
        _ffn_kernel,
        grid=(t // tm, ff // tf),
        in_specs=[pl.BlockSpec((tm, d), lambda i, f: (i, 0)),
                  pl.BlockSpec((d, tf), lambda i, f: (0, f)),
                  pl.BlockSpec((d, tf), lambda i, f: (0, f)),
                  pl.BlockSpec((tf, d), lambda i, f: (f, 0)),
                  pl.BlockSpec((tm, d), lambda i, f: (i, 0)),
                  pl.BlockSpec((1, d), lambda i, f: (0, 0)),
                  pl.BlockSpec((None, 1, 6 * d), lambda i, f: (i // tpb, 0, 0))],
        out_specs=pl.BlockSpec((tm, d), lambda i, f: (i, 0)),
        out_shape=jax.ShapeDtypeStruct((t, d), F32),
        scratch_shapes=[pltpu.VMEM((tm, d), F32)],
        compiler_params=_cparams(("arbitrary", "arbitrary")),
        name="dense_swiglu",
    )(h, wg, wu, wd, x2, g_post.reshape(1, d), mod_l)


def _route_kernel(logit_ref, tri_ref, info_ref, info_t_ref, count_ref, carry_ref):
    @pl.when(pl.program_id(0) == 0)
    def _():
        carry_ref[...] = jnp.zeros_like(carry_ref)

    lg = logit_ref[...]
    lane = lax.broadcasted_iota(jnp.int32, lg.shape, 1)
    lanef = lane.astype(F32)
    valid = lane < N_EXPERTS
    big = float(LANES)
    lg = jnp.where(valid, lg, -jnp.inf)
    m1 = jnp.max(lg, axis=-1, keepdims=True)
    e1 = jnp.min(jnp.where(lg == m1, lanef, big), axis=-1, keepdims=True)
    lg2 = jnp.where(lanef == e1, -jnp.inf, lg)
    m2 = jnp.max(lg2, axis=-1, keepdims=True)
    e2 = jnp.min(jnp.where(lg2 == m2, lanef, big), axis=-1, keepdims=True)
    z = jnp.exp(m2 - m1)
    w1 = 1.0 / (1.0 + z)
    w2 = z / (1.0 + z)
    oh1 = (lanef == e1).astype(F32)
    oh2 = (lanef == e2).astype(F32)
    both = oh1 + oh2
    before = _dot(tri_ref[...], both.astype(BF16)) + carry_ref[0:1, :]
    r1 = jnp.sum(before * oh1, axis=-1, keepdims=True)
    r2 = jnp.sum(before * oh2, axis=-1, keepdims=True)
    carry_ref[0:1, :] = carry_ref[0:1, :] + jnp.sum(both, axis=0, keepdims=True)
    count_ref[...] = carry_ref[...]
    cols = (e1, e2, r1, r2, w1, w2)
    info = jnp.zeros(lg.shape, F32)
    for idx, col in enumerate(cols):
        info = jnp.where(lane == idx, col, info)
    info_ref[...] = info
    info_t_ref[...] = jnp.transpose(info)[0:8, :]


def _route(logits):
    t = logits.shape[0]
    tm = _tile(t, 512)
    tri = jnp.asarray(np.tril(np.ones((tm, tm), np.float32), -1), BF16)
    info, info_t, counts = pl.pallas_call(
        _route_kernel,
        grid=(t // tm,),
        in_specs=[pl.BlockSpec((tm, LANES), lambda i: (i, 0)), pl.BlockSpec((tm, tm), lambda i: (0, 0))],
        out_specs=[pl.BlockSpec((tm, LANES), lambda i: (i, 0)), pl.BlockSpec((8, tm), lambda i: (0, i)),
                   pl.BlockSpec((8, LANES), lambda i: (0, 0))],
        out_shape=[jax.ShapeDtypeStruct((t, LANES), F32), jax.ShapeDtypeStruct((8, t), F32),
                   jax.ShapeDtypeStruct((8, LANES), F32)],
        scratch_shapes=[pltpu.VMEM((8, LANES), F32)],
        compiler_params=_cparams(("arbitrary",)),
        name="moe_route",
    )(logits, tri)
    return info, info_t, counts[0, :N_EXPERTS]


def _dispatch_kernel(d1_ref, d2_ref, h_ref, xin_ref, xbuf_ref, sem, *, tm):
    del xin_ref

    def row_copy(r, dref):
        return pltpu.make_async_copy(h_ref.at[pl.ds(r, 1), :], xbuf_ref.at[pl.ds(dref[0, 0, r], 1), :], sem)

    def start(g, carry):
        for j in range(DMA_UNROLL):
            r = g * DMA_UNROLL + j
            row_copy(r, d1_ref).start(priority=0)
            row_copy(r, d2_ref).start(priority=1)
        return carry

    lax.fori_loop(0, tm // DMA_UNROLL, start, 0)
    for _ in range(TOP_K):
        pltpu.make_async_copy(h_ref, xbuf_ref.at[pl.ds(0, tm), :], sem).wait()


def _dispatch(h, dest1, dest2, n_rows):
    t, d = h.shape
    tm = _tile(t, 512)
    nt = t // tm
    smem = lambda: pl.BlockSpec((1, 1, tm), lambda i: (i, 0, 0), memory_space=pltpu.SMEM)
    return pl.pallas_call(
        functools.partial(_dispatch_kernel, tm=tm),
        grid=(nt,),
        in_specs=[smem(), smem(), pl.BlockSpec((tm, d), lambda i: (i, 0)), pl.BlockSpec(memory_space=pl.ANY)],
        out_specs=pl.BlockSpec(memory_space=pl.ANY),
        out_shape=jax.ShapeDtypeStruct((n_rows, d), h.dtype),
        scratch_shapes=[pltpu.SemaphoreType.DMA(())],
        input_output_aliases={3: 0},
        compiler_params=_cparams(("arbitrary",)),
        name="moe_dispatch",
    )(dest1.reshape(nt, 1, tm), dest2.reshape(nt, 1, tm), h, jnp.zeros((n_rows, d), h.dtype))


def _moe_kernel(be_ref, nb_ref, top_ref, x_ref, wg_ref, wu_ref, wd_ref, o_ref, xb_ref, acc_ref):
    i = pl.program_id(0)
    f = pl.program_id(1)
    active = i < nb_ref[0]
    top_half_only = top_ref[i] == 1

    @pl.when(jnp.logical_and(active, f == 0))
    def _():
        xb_ref[...] = x_ref[...].astype(BF16)
        acc_ref[...] = jnp.zeros_like(acc_ref)

    def ffn(rows):
        xb = xb_ref[rows, :]
        a = _silu(_dot(xb, wg_ref[...].astype(BF16))) * _dot(xb, wu_ref[...].astype(BF16))
        acc_ref[rows, :] += _dot(a.astype(BF16), wd_ref[...].astype(BF16))

    @pl.when(jnp.logical_and(active, jnp.logical_not(top_half_only)))
    def _():
        ffn(pl.ds(0, xb_ref.shape[0]))

    @pl.when(jnp.logical_and(active, top_half_only))
    def _():
        ffn(pl.ds(0, xb_ref.shape[0] // 2))

    @pl.when(f == pl.num_programs(1) - 1)
    def _():
        o_ref[...] = jnp.where(active, acc_ref[...], 0.0)


def _moe_ffn(x_buf, block_expert, n_active, top_half_only, wg, wu, wd):
    n_rows, d = x_buf.shape
    ff = wg.shape[2]
    tm = MOE_TILE
    tf = _tile(ff, FF_TILE)
    nf = ff // tf

    def wmap(i, f, be, nb):
        return be[i], jnp.where(i < nb[0], f, nf - 1)

    grid_spec = pltpu.PrefetchScalarGridSpec(
        num_scalar_prefetch=3,
        grid=(n_rows // tm, nf),
        in_specs=[pl.BlockSpec((tm, d), lambda i, f, be, nb, top: (jnp.minimum(i, nb[0] - 1), 0)),
                  pl.BlockSpec((None, d, tf),
                               lambda i, f, be, nb, top: (wmap(i, f, be, nb)[0], 0, wmap(i, f, be, nb)[1])),
                  pl.BlockSpec((None, d, tf),
                               lambda i, f, be, nb, top: (wmap(i, f, be, nb)[0], 0, wmap(i, f, be, nb)[1])),
                  pl.BlockSpec((None, tf, d),
                               lambda i, f, be, nb, top: (wmap(i, f, be, nb)[0], wmap(i, f, be, nb)[1], 0))],
        out_specs=pl.BlockSpec((tm, d), lambda i, f, be, nb, top: (i, 0)),
        scratch_shapes=[pltpu.VMEM((tm, d), BF16), pltpu.VMEM((tm, d), F32)],
    )
    return pl.pallas_call(
        _moe_kernel,
        grid_spec=grid_spec,
        out_shape=jax.ShapeDtypeStruct((n_rows, d), F32),
        compiler_params=_cparams(("arbitrary", "arbitrary")),
        name="moe_grouped_swiglu",
    )(block_expert, n_active, top_half_only, x_buf, wg, wu, wd)


def _combine_kernel(d1_ref, d2_ref, d1n_ref, d2n_ref, ybuf_ref, info_ref, x_ref, g_ref, mod_ref, o_ref,
                    y1_ref, y2_ref, sem, *, tm):
    i = pl.program_id(0)
    slot = i % 2

    def gather(da_ref, db_ref, s):
        def start(g, carry):
            for j in range(DMA_UNROLL):
                r = g * DMA_UNROLL + j
                pltpu.make_async_copy(ybuf_ref.at[pl.ds(da_ref[0, 0, r], 1), :], y1_ref.at[s, pl.ds(r, 1), :],
                                      sem.at[s]).start(priority=0)
                pltpu.make_async_copy(ybuf_ref.at[pl.ds(db_ref[0, 0, r], 1), :], y2_ref.at[s, pl.ds(r, 1), :],
                                      sem.at[s]).start(priority=1)
            return carry

        lax.fori_loop(0, tm // DMA_UNROLL, start, 0)

    @pl.when(i == 0)
    def _():
        gather(d1_ref, d2_ref, 0)

    @pl.when(i + 1 < pl.num_programs(0))
    def _():
        gather(d1n_ref, d2n_ref, 1 - slot)

    for dst in (y1_ref, y2_ref):
        pltpu.make_async_copy(ybuf_ref.at[pl.ds(0, tm), :], dst.at[slot], sem.at[slot]).wait()
    d = D_MODEL
    info = info_ref[...]
    y = y1_ref[slot] * info[:, 4:5] + y2_ref[slot] * info[:, 5:6]
    o_ref[...] = x_ref[...] + mod_ref[:, 5 * d:6 * d] * _rms(y, g_ref[...])


def _combine(y_buf, dest1, dest2, info, x2, g_post, mod_l, seq):
    t, d = x2.shape
    tm = _tile(seq, 512)
    nt = t // tm
    tpb = seq // tm
    smem = lambda: pl.BlockSpec((1, 1, tm), lambda i: (i, 0, 0), memory_space=pltpu.SMEM)
    smem_next = lambda: pl.BlockSpec((1, 1, tm), lambda i: (jnp.minimum(i + 1, nt - 1), 0, 0),
                                     memory_space=pltpu.SMEM)
    dest1, dest2 = dest1.reshape(nt, 1, tm), dest2.reshape(nt, 1, tm)
    return pl.pallas_call(
        functools.partial(_combine_kernel, tm=tm),
        grid=(nt,),
        in_specs=[smem(), smem(), smem_next(), smem_next(), pl.BlockSpec(memory_space=pl.ANY),
                  pl.BlockSpec((tm, LANES), lambda i: (i, 0)),
                  pl.BlockSpec((tm, d), lambda i: (i, 0)),
                  pl.BlockSpec((1, d), lambda i: (0, 0)),
                  pl.BlockSpec((None, 1, 6 * d), lambda i: (i // tpb, 0, 0))],
        out_specs=pl.BlockSpec((tm, d), lambda i: (i, 0)),
        out_shape=jax.ShapeDtypeStruct((t, d), F32),
        scratch_shapes=[pltpu.VMEM((2, tm, d), F32), pltpu.VMEM((2, tm, d), F32), pltpu.SemaphoreType.DMA((2,))],
        compiler_params=_cparams(("arbitrary",)),
        name="moe_combine",
    )(dest1, dest2, dest1, dest2, y_buf, info, x2, g_post.reshape(1, d), mod_l)


def _routed_ffn(h, logits, wg, wu, wd, x2, g_post, mod_l, seq):
    t, d = h.shape
    tm = MOE_TILE
    info, info_t, counts = _route(logits)
    counts = counts.astype(jnp.int32)
    padded = ((counts + tm - 1) // tm) * tm
    pad_end = jnp.cumsum(padded)
    pad_start = pad_end - padded
    fields = info_t[0:4].astype(jnp.int32)
    dest1 = pad_start[fields[0]] + fields[2]
    dest2 = pad_start[fields[1]] + fields[3]
    n_rows = t * TOP_K + N_EXPERTS * tm
    n_blocks = n_rows // tm
    block_start = jnp.arange(n_blocks, dtype=jnp.int32) * tm
    block_expert = jnp.minimum(jnp.sum(pad_end[None, :] <= block_start[:, None], axis=1), N_EXPERTS - 1).astype(jnp.int32)
    n_active = (pad_end[-1] // tm).astype(jnp.int32).reshape(1)
    rows_used = (pad_start + counts)[block_expert] - block_start
    top_half_only = (rows_used <= tm // 2).astype(jnp.int32)
    x_buf = _dispatch(h, dest1, dest2, n_rows)
    y_buf = _moe_ffn(x_buf, block_expert, n_active, top_half_only, wg, wu, wd)
    return _combine(y_buf, dest1, dest2, info, x2, g_post, mod_l, seq)


def kernel(x, c, positions, ada_w, ada_b, norm_pre_mix, norm_post_mix, norm_pre_ffn, norm_post_ffn, w_in, ret_norm, ssm_a_re, ssm_a_im, ssm_b_re, ssm_b_im, ssm_c_re, ssm_c_im, ssm_d, ssm_log_dt, ssm_glu_w, ssm_glu_b, ssm_norm, mla_q_norm, mla_w_uq, mla_kv_norm, mla_w_ukv, mla_norm, w_out, ffn_w_gate, ffn_w_up, ffn_w_down, moe_router, moe_w_gate, moe_w_up, moe_w_down):
    bsz, seq, d = x.shape
    depth = ada_w.shape[0]
    assert d == D_MODEL and seq % SSM_CHUNK == 0 and seq % RET_CHUNK == 0
    t = bsz * seq
    x2 = x.reshape(t, d)
    mod = _modulation(c, ada_w, ada_b)
    cos128, sin128 = _rope_tables(positions)
    ssm_params = _ssm_params(ssm_a_re, ssm_a_im, ssm_b_re, ssm_b_im, ssm_c_re, ssm_c_im, ssm_log_dt)
    for layer in range(depth):
        mod_l = mod[layer]
        r, u, cq, ckv, kr = _inproj(x2, norm_pre_mix[layer], mod_l, w_in[layer], seq)
        ret = _retention(r, cos128, sin128, ret_norm[layer], bsz, seq)
        ssm = _ssm_post(_ssm_scan(u, ssm_params, layer, seq), u, ssm_d[layer], ssm_glu_w[layer], ssm_glu_b[layer],
                        ssm_norm[layer])
        q3, k3, v3 = _mla_prep(cq, ckv, kr, cos128, sin128, mla_q_norm[layer], mla_w_uq[layer],
                               mla_kv_norm[layer], mla_w_ukv[layer])
        att = _attention(q3, k3, v3, bsz, seq)
        j = layer // 2
        if layer % 2 == 0:
            x2, h = _outproj(x2, ret, ssm, att, mla_norm[layer], w_out[layer], norm_post_mix[layer], mod_l,
                             norm_pre_ffn[layer], seq)
            x2 = _dense_ffn(h, ffn_w_gate[j], ffn_w_up[j], ffn_w_down[j], x2, norm_post_ffn[layer], mod_l, seq)
        else:
            x2, h, logits = _outproj(x2, ret, ssm, att, mla_norm[layer], w_out[layer], norm_post_mix[layer], mod_l,
                                     norm_pre_ffn[layer], seq, router=moe_router[j])
            x2 = _routed_ffn(h, logits, moe_w_gate[j], moe_w_up[j], moe_w_down[j], x2, norm_post_ffn[layer],
                             mod_l, seq)
    return x2.reshape(bsz, seq, d)
```
